```python
import math
import jax
import jax.numpy as jnp
from jax import lax
import numpy as np

D_MODEL = 1024
BATCH = 8
SEQ = 2048
DEPTH = 2

CTX_LEN = 256
GRID_W = 64
RMS_EPS = 1e-6
N_MOD = 6

SHORT_CONV = 3
FILTER_BANDS = 8
FILTER_EMB = 1 + 2 * FILTER_BANDS
FILTER_HIDDEN = 64
FILTER_INNER = 2
DECAY_TARGET = 1e-2
FAST_DECAY_PCT = 0.3
SLOW_DECAY_PCT = 1.5

N_HEADS = 16
QK_NOPE = 64
QK_ROPE = 32
QK_HEAD = QK_NOPE + QK_ROPE
V_HEAD = 64
Q_LORA = 384
KV_LORA = 256
ROPE_AXIS = QK_ROPE // 2
ROPE_BASE = 10000.0
Q_BLOCK = 128

D_FF = 3584
N_EXPERTS = 8
TOP_K = 2

kernel_name = 'hybrid_hyena_mla_moe_flow_block'


def rmsnorm(x, g):
    xf = x.astype(jnp.float32)
    y = xf * lax.rsqrt(jnp.mean(xf * xf, axis=-1, keepdims=True) + RMS_EPS)
    return (y * g.astype(jnp.float32)).astype(x.dtype)


def modulate(h, shift, scale):
    return h * (1 + scale) + shift


def swiglu(h, w1, w3, w2):
    return (jax.nn.silu(h @ w1) * (h @ w3)) @ w2


def short_conv(z, w, b):
    L = z.shape[1]
    pad = SHORT_CONV // 2
    zp = jnp.pad(z, ((0, 0), (pad, SHORT_CONV - 1 - pad), (0, 0)))
    return sum(zp[:, j:j + L] * w[j] for j in range(SHORT_CONV)) + b


def hyena_filter(L, w0, b0, wi, bi, freq, wout):
    pos = jnp.arange(L, dtype=jnp.float32)
    t = (pos / max(L - 1, 1))[:, None]
    w = 2.0 * math.pi * pos / L
    f = jnp.linspace(1e-4, FILTER_BANDS - 1, FILTER_BANDS, dtype=jnp.float32)
    ang = w[:, None] * f[None, :]
    z = jnp.concatenate([t, jnp.cos(ang), -jnp.sin(ang)], axis=-1)
    fr = freq.astype(jnp.float32)
    h = jnp.sin(fr * (z @ w0.astype(jnp.float32) + b0.astype(jnp.float32)))
    for n in range(FILTER_INNER):
        h = jnp.sin(fr * (h @ wi[n].astype(jnp.float32) + bi[n].astype(jnp.float32)))
    h = h @ wout.astype(jnp.float32)
    D = wout.shape[-1] // 2
    deltas = jnp.abs(jnp.linspace(math.log(DECAY_TARGET) / SLOW_DECAY_PCT,
                                  math.log(DECAY_TARGET) / FAST_DECAY_PCT, D, dtype=jnp.float32))
    decay = jnp.exp(-t * deltas[None, :])
    h = h.reshape(L, 2, D) * decay[:, None, :]
    k_full = jnp.concatenate([h[:, 0], jnp.zeros((1, D), jnp.float32), h[1:, 1][::-1]], axis=0)
    return k_full / jnp.sum(jnp.abs(k_full), axis=0, keepdims=True)


def long_conv(v, k_full):
    L = v.shape[1]
    vf = jnp.fft.rfft(v.astype(jnp.float32), n=2 * L, axis=1)
    kf = jnp.fft.rfft(k_full, n=2 * L, axis=0)
    y = jnp.fft.irfft(vf * kf[None], n=2 * L, axis=1)[:, :L]
    return y.astype(v.dtype)


def hyena_mixer(h, p):
    in_w, in_b, sc_w, sc_b, f_w0, f_b0, f_wi, f_bi, f_freq, f_wout, f_bias, out_w, out_b = p
    L = h.shape[1]
    z = short_conv(h @ in_w + in_b, sc_w, sc_b)
    x0, x1, v = jnp.split(z, 3, axis=-1)
    v = v * x1
    k_full = hyena_filter(L, f_w0, f_b0, f_wi, f_bi, f_freq, f_wout)
    v = long_conv(v, k_full) + v * f_bias
    return (x0 * v) @ out_w + out_b


def axial_rope(L):
    rows = L // GRID_W
    row = jnp.broadcast_to(jnp.arange(rows, dtype=jnp.float32)[:, None], (rows, GRID_W)).reshape(L)
    col = jnp.broadcast_to(jnp.arange(GRID_W, dtype=jnp.float32)[None, :], (rows, GRID_W)).reshape(L)
    inv = ROPE_BASE ** (-jnp.arange(0, ROPE_AXIS, 2, dtype=jnp.float32) / ROPE_AXIS)
    ang = jnp.concatenate([row[:, None] * inv, col[:, None] * inv], axis=-1)
    return jnp.cos(ang), jnp.sin(ang)


def apply_rope(x, cos, sin):
    xf = x.astype(jnp.float32).reshape(x.shape[:-1] + (QK_ROPE // 2, 2))
    a, b = xf[..., 0], xf[..., 1]
    out = jnp.stack([a * cos - b * sin, a * sin + b * cos], axis=-1)
    return out.reshape(x.shape).astype(x.dtype)


def mla_q(h, wq_a, q_norm, wq_b, rope):
    B, L, _ = h.shape
    q = (rmsnorm(h @ wq_a, q_norm) @ wq_b).reshape(B, L, N_HEADS, QK_HEAD)
    q_nope, q_pe = q[..., :QK_NOPE], q[..., QK_NOPE:]
    if rope is not None:
        cos, sin = rope
        q_pe = apply_rope(q_pe, cos[None, :, None], sin[None, :, None])
    return jnp.concatenate([q_nope, q_pe], axis=-1)


def mla_kv(h, wkv_a, kv_norm, wkv_b, rope):
    B, L, _ = h.shape
    kv = h @ wkv_a
    c_kv = rmsnorm(kv[..., :KV_LORA], kv_norm)
    k_pe = kv[..., KV_LORA:]
    if rope is not None:
        cos, sin = rope
        k_pe = apply_rope(k_pe, cos[None], sin[None])
    kvb = (c_kv @ wkv_b).reshape(B, L, N_HEADS, QK_NOPE + V_HEAD)
    k = jnp.concatenate([kvb[..., :QK_NOPE],
                         jnp.broadcast_to(k_pe[:, :, None, :], (B, L, N_HEADS, QK_ROPE))], axis=-1)
    return k, kvb[..., QK_NOPE:]


def attention(q, k, v):
    B, Lq, H, Dq = q.shape
    nb = Lq // Q_BLOCK
    scale = 1.0 / math.sqrt(Dq)
    qb = q.reshape(B, nb, Q_BLOCK, H, Dq).transpose(1, 0, 2, 3, 4)

    def block(qblk):
        s = jnp.einsum('bqhd,bkhd->bhqk', qblk, k).astype(jnp.float32) * scale
        p = jax.nn.softmax(s, axis=-1).astype(v.dtype)
        return jnp.einsum('bhqk,bkhd->bqhd', p, v)

    o = lax.map(block, qb)
    return o.transpose(1, 0, 2, 3, 4).reshape(B, Lq, H, v.shape[-1])


def mla_mixer(hx, hc, p, rope, ctx_queries):
    wq_a, q_norm, wq_b, wkv_a, kv_norm, wkv_b, wo = p
    B, L, _ = hx.shape
    kc, vc = mla_kv(hc, wkv_a, kv_norm, wkv_b, None)
    kx, vx = mla_kv(hx, wkv_a, kv_norm, wkv_b, rope)
    qx = mla_q(hx, wq_a, q_norm, wq_b, rope)
    ox = attention(qx, jnp.concatenate([kc, kx], axis=1), jnp.concatenate([vc, vx], axis=1))
    yx = ox.reshape(B, L, N_HEADS * V_HEAD) @ wo
    yc = None
    if ctx_queries:
        qc = mla_q(hc, wq_a, q_norm, wq_b, None)
        oc = attention(qc, kc, vc)
        yc = oc.reshape(B, hc.shape[1], N_HEADS * V_HEAD) @ wo
    return yx, yc


def moe(h, router, w1, w3, w2):
    B, L, D = h.shape
    t = h.reshape(B * L, D)
    logits = (t @ router).astype(jnp.float32)
    vals, idx = lax.top_k(logits, TOP_K)
    wts = jax.nn.softmax(vals, axis=-1)
    gates = jnp.sum(jax.nn.one_hot(idx, N_EXPERTS, dtype=jnp.float32) * wts[..., None], axis=1)
    gates = gates.astype(t.dtype)
    out = jnp.zeros_like(t)
    for e in range(N_EXPERTS):
        out = out + gates[:, e:e + 1] * swiglu(t, w1[e], w3[e], w2[e])
    return out.reshape(B, L, D)


def setup_inputs(seed: int = 0) -> dict:
    key = jax.random.key(seed)
    ks = iter(jax.random.split(key, 48))
    nh = (DEPTH + 1) // 2
    nm = DEPTH // 2
    D = D_MODEL

    def nrm(shape, scale):
        return jax.random.normal(next(ks), shape, jnp.float32) * scale

    def gain(shape):
        return 1.0 + nrm(shape, 0.05)

    return {
        'x': nrm((BATCH, SEQ, D), 1.0),
        'c': nrm((BATCH, D), 1.0),
        'ctx': nrm((BATCH, CTX_LEN, D), 1.0),
        'c_ctx': nrm((D,), 1.0),
        'ada_w': nrm((DEPTH, D, N_MOD * D), 0.5 * D ** -0.5),
        'ada_b': nrm((DEPTH, N_MOD * D), 0.02),
        'norm_mix': gain((DEPTH, D)),
        'norm_ffn': gain((DEPTH, D)),
        'hy_in_w': nrm((nh, D, 3 * D), D ** -0.5),
        'hy_in_b': nrm((nh, 3 * D), 0.02),
        'hy_sc_w': nrm((nh, SHORT_CONV, 3 * D), SHORT_CONV ** -0.5),
        'hy_sc_b': nrm((nh, 3 * D), 0.02),
        'hy_f_w0': nrm((nh, FILTER_EMB, FILTER_HIDDEN), FILTER_EMB ** -0.5),
        'hy_f_b0': nrm((nh, FILTER_HIDDEN), 0.02),
        'hy_f_wi': nrm((nh, FILTER_INNER, FILTER_HIDDEN, FILTER_HIDDEN), FILTER_HIDDEN ** -0.5),
        'hy_f_bi': nrm((nh, FILTER_INNER, FILTER_HIDDEN), 0.02),
        'hy_f_freq': gain((nh, FILTER_HIDDEN)),
        'hy_f_wout': nrm((nh, FILTER_HIDDEN, 2 * D), FILTER_HIDDEN ** -0.5),
        'hy_f_bias': nrm((nh, D), 0.1),
        'hy_out_w': nrm((nh, D, D), D ** -0.5),
        'hy_out_b': nrm((nh, D), 0.02),
        'mla_wq_a': nrm((nm, D, Q_LORA), D ** -0.5),
        'mla_q_norm': gain((nm, Q_LORA)),
        'mla_wq_b': nrm((nm, Q_LORA, N_HEADS * QK_HEAD), Q_LORA ** -0.5),
        'mla_wkv_a': nrm((nm, D, KV_LORA + QK_ROPE), D ** -0.5),
        'mla_kv_norm': gain((nm, KV_LORA)),
        'mla_wkv_b': nrm((nm, KV_LORA, N_HEADS * (QK_NOPE + V_HEAD)), KV_LORA ** -0.5),
        'mla_wo': nrm((nm, N_HEADS * V_HEAD, D), (N_HEADS * V_HEAD) ** -0.5),
        'ffn_w1': nrm((nh, D, D_FF), D ** -0.5),
        'ffn_w3': nrm((nh, D, D_FF), D ** -0.5),
        'ffn_w2': nrm((nh, D_FF, D), D_FF ** -0.5),
        'moe_router': nrm((nm, D, N_EXPERTS), D ** -0.5),
        'moe_w1': nrm((nm, N_EXPERTS, D, D_FF), D ** -0.5),
        'moe_w3': nrm((nm, N_EXPERTS, D, D_FF), D ** -0.5),
        'moe_w2': nrm((nm, N_EXPERTS, D_FF, D), D_FF ** -0.5),
        'norm_final': gain((D,)),
    }


def reference(x, c, ctx, c_ctx, ada_w, ada_b, norm_mix, norm_ffn,
              hy_in_w, hy_in_b, hy_sc_w, hy_sc_b, hy_f_w0, hy_f_b0, hy_f_wi, hy_f_bi,
              hy_f_freq, hy_f_wout, hy_f_bias, hy_out_w, hy_out_b,
              mla_wq_a, mla_q_norm, mla_wq_b, mla_wkv_a, mla_kv_norm, mla_wkv_b, mla_wo,
              ffn_w1, ffn_w3, ffn_w2, moe_router, moe_w1, moe_w3, moe_w2, norm_final):
    rope = axial_rope(x.shape[1])
    for i in range(DEPTH):
        j = i // 2
        last = i == DEPTH - 1
        mod_x = (jax.nn.silu(c) @ ada_w[i] + ada_b[i])[:, None, :]
        mod_c = (jax.nn.silu(c_ctx) @ ada_w[i] + ada_b[i])[None, None, :]
        sh1x, sc1x, g1x, sh2x, sc2x, g2x = jnp.split(mod_x, N_MOD, axis=-1)
        sh1c, sc1c, g1c, sh2c, sc2c, g2c = jnp.split(mod_c, N_MOD, axis=-1)

        hx = modulate(rmsnorm(x, norm_mix[i]), sh1x, sc1x)
        hc = modulate(rmsnorm(ctx, norm_mix[i]), sh1c, sc1c)
        if i % 2 == 0:
            hp = (hy_in_w[j], hy_in_b[j], hy_sc_w[j], hy_sc_b[j], hy_f_w0[j], hy_f_b0[j],
                  hy_f_wi[j], hy_f_bi[j], hy_f_freq[j], hy_f_wout[j], hy_f_bias[j],
                  hy_out_w[j], hy_out_b[j])
            yx = hyena_mixer(hx, hp)
            yc = None if last else hyena_mixer(hc, hp)
        else:
            mp = (mla_wq_a[j], mla_q_norm[j], mla_wq_b[j], mla_wkv_a[j], mla_kv_norm[j],
                  mla_wkv_b[j], mla_wo[j])
            yx, yc = mla_mixer(hx, hc, mp, rope, not last)
        x = x + g1x * yx
        if not last:
            ctx = ctx + g1c * yc

        hx = modulate(rmsnorm(x, norm_ffn[i]), sh2x, sc2x)
        if i % 2 == 0:
            x = x + g2x * swiglu(hx, ffn_w1[j], ffn_w3[j], ffn_w2[j])
        else:
            x = x + g2x * moe(hx, moe_router[j], moe_w1[j], moe_w3[j], moe_w2[j])
        if not last:
            hc = modulate(rmsnorm(ctx, norm_ffn[i]), sh2c, sc2c)
            if i % 2 == 0:
                ctx = ctx + g2c * swiglu(hc, ffn_w1[j], ffn_w3[j], ffn_w2[j])
            else:
                ctx = ctx + g2c * moe(hc, moe_router[j], moe_w1[j], moe_w3[j], moe_w2[j])
    return rmsnorm(x, norm_final)
```

```python
import functools
import math

import jax
import jax.numpy as jnp
import numpy as np
from jax import lax
from jax.experimental import pallas as pl
from jax.experimental.pallas import tpu as pltpu

F32 = jnp.float32
BF16 = jnp.bfloat16
HIGHEST = lax.Precision.HIGHEST

RMS_EPS = 1e-6
N_MOD = 6
MOD_ROWS = 8
GRID_W = 64
SHORT_CONV = 3
FILTER_BANDS = 8
FILTER_EMB = 1 + 2 * FILTER_BANDS
FILTER_EMB_PAD = 32
DECAY_TARGET = 1e-2
FAST_DECAY_PCT = 0.3
SLOW_DECAY_PCT = 1.5
N_HEADS = 16
QK_NOPE = 64
QK_ROPE = 32
QK_HEAD = QK_NOPE + QK_ROPE
V_HEAD = 64
ROPE_AXIS = QK_ROPE // 2
ROPE_BASE = 10000.0
TOP_K = 2
LANES = 128
HEAD_SLOT = 128
VMEM_LIMIT = 56 * 1024 * 1024


def _cparams(sem, vmem=VMEM_LIMIT):
    return pltpu.CompilerParams(dimension_semantics=sem, vmem_limit_bytes=vmem)


def _rms(x, g):
    return x * lax.rsqrt(jnp.mean(x * x, axis=-1, keepdims=True) + RMS_EPS) * g


def _silu(x):
    return x * (1.0 / (1.0 + jnp.exp(-x)))


def _pick(total, pref):
    t = min(total, pref)
    while total % t:
        t //= 2
    return t


def _ada_kernel(c_ref, w_ref, b_ref, o_ref):
    c = c_ref[...]
    o_ref[0] = jnp.dot(_silu(c), w_ref[0], precision=HIGHEST, preferred_element_type=F32) + b_ref[0]


def _ada_mod(cvec, ada_w, ada_b):
    depth, d, n = ada_w.shape
    r = cvec.shape[0]
    tn = _pick(n, 1536)
    return pl.pallas_call(
        _ada_kernel,
        out_shape=jax.ShapeDtypeStruct((depth, r, n), F32),
        grid=(depth, n // tn),
        in_specs=[pl.BlockSpec((r, d), lambda i, j: (0, 0)),
                  pl.BlockSpec((1, d, tn), lambda i, j: (i, 0, j)),
                  pl.BlockSpec((1, 1, tn), lambda i, j: (i, 0, j))],
        out_specs=pl.BlockSpec((1, r, tn), lambda i, j: (i, 0, j)),
        compiler_params=_cparams(("arbitrary", "arbitrary")),
        name="ada_mod",
    )(cvec, ada_w, ada_b.reshape(depth, 1, n))


def _nmm_kernel(x_ref, mod_ref, g_ref, w_ref, b_ref, o_ref, *, sh_row, sc_row):
    x = x_ref[0]
    mod = mod_ref[0]
    h = _rms(x, g_ref[...]) * (1.0 + mod[sc_row:sc_row + 1]) + mod[sh_row:sh_row + 1]
    o_ref[0] = jnp.dot(h.astype(BF16), w_ref[...], preferred_element_type=F32) + b_ref[...]


def _norm_mod_matmul(x, mod, gnorm, w_bf16, bias, sh_row, sc_row, tm_pref=512):
    bx, l, d = x.shape
    n = w_bf16.shape[1]
    tm = _pick(l, tm_pref)
    per_b = mod.shape[0] > 1
    return pl.pallas_call(
        functools.partial(_nmm_kernel, sh_row=sh_row, sc_row=sc_row),
        out_shape=jax.ShapeDtypeStruct((bx, l, n), F32),
        grid=(bx, l // tm),
        in_specs=[pl.BlockSpec((1, tm, d), lambda b, i: (b, i, 0)),
                  pl.BlockSpec((1, MOD_ROWS, d), (lambda b, i: (b, 0, 0)) if per_b else (lambda b, i: (0, 0, 0))),
                  pl.BlockSpec((1, d), lambda b, i: (0, 0)),
                  pl.BlockSpec((d, n), lambda b, i: (0, 0)),
                  pl.BlockSpec((1, n), lambda b, i: (0, 0))],
        out_specs=pl.BlockSpec((1, tm, n), lambda b, i: (b, i, 0)),
        compiler_params=_cparams(("arbitrary", "arbitrary")),
        name="norm_mod_matmul",
    )(x, mod, gnorm.reshape(1, d), w_bf16, bias.reshape(1, n))


def _mm_res_kernel(a_ref, w_ref, b_ref, r_ref, mod_ref, o_ref, *, gate_row):
    y = jnp.dot(a_ref[0], w_ref[...], preferred_element_type=F32) + b_ref[...]
    o_ref[0] = r_ref[0] + mod_ref[0][gate_row:gate_row + 1] * y


def _matmul_gated_residual(a_bf16, w_bf16, bias, resid, mod, gate_row, tm_pref=512):
    bx, l, k = a_bf16.shape
    d = w_bf16.shape[1]
    tm = _pick(l, tm_pref)
    per_b = mod.shape[0] > 1
    return pl.pallas_call(
        functools.partial(_mm_res_kernel, gate_row=gate_row),
        out_shape=jax.ShapeDtypeStruct((bx, l, d), F32),
        grid=(bx, l // tm),
        in_specs=[pl.BlockSpec((1, tm, k), lambda b, i: (b, i, 0)),
                  pl.BlockSpec((k, d), lambda b, i: (0, 0)),
                  pl.BlockSpec((1, d), lambda b, i: (0, 0)),
                  pl.BlockSpec((1, tm, d), lambda b, i: (b, i, 0)),
                  pl.BlockSpec((1, MOD_ROWS, d), (lambda b, i: (b, 0, 0)) if per_b else (lambda b, i: (0, 0, 0)))],
        out_specs=pl.BlockSpec((1, tm, d), lambda b, i: (b, i, 0)),
        compiler_params=_cparams(("arbitrary", "arbitrary")),
        name="matmul_gated_residual",
    )(a_bf16, w_bf16, bias.reshape(1, d), resid, mod)


def _filter_kernel(z_ref, w0_ref, b0_ref, wi_ref, bi_ref, fr_ref, wt_ref, wb_ref, dec_ref, o_ref, *, l):
    fr = fr_ref[...]
    h = jnp.sin(fr * (jnp.dot(z_ref[...], w0_ref[...], precision=HIGHEST, preferred_element_type=F32) + b0_ref[...]))
    for n in range(wi_ref.shape[0]):
        h = jnp.sin(fr * (jnp.dot(h, wi_ref[n], precision=HIGHEST, preferred_element_type=F32) + bi_ref[n]))
    top = jnp.dot(h[:l], wt_ref[...], precision=HIGHEST, preferred_element_type=F32)
    bot = jnp.dot(h[l:], wb_ref[...], precision=HIGHEST, preferred_element_type=F32)
    k = jnp.concatenate([top, bot], axis=0) * dec_ref[...]
    o_ref[...] = k / jnp.sum(jnp.abs(k), axis=0, keepdims=True)


def _hyena_filter_full(l, d, w0, b0, wi, bi, freq, wout):
    pos = jnp.arange(l, dtype=F32)
    t = (pos / max(l - 1, 1))[:, None]
    w = 2.0 * math.pi * pos / l
    f = jnp.linspace(1e-4, FILTER_BANDS - 1, FILTER_BANDS, dtype=F32)
    ang = w[:, None] * f[None, :]
    z = jnp.concatenate([t, jnp.cos(ang), -jnp.sin(ang)], axis=-1)
    deltas = jnp.abs(jnp.linspace(math.log(DECAY_TARGET) / SLOW_DECAY_PCT,
                                  math.log(DECAY_TARGET) / FAST_DECAY_PCT, d, dtype=F32))
    decay = jnp.exp(-t * deltas[None, :])
    idx = np.concatenate([np.arange(l), [0], np.arange(l - 1, 0, -1)])
    zc = jnp.pad(z[idx], ((0, 0), (0, FILTER_EMB_PAD - FILTER_EMB)))
    dec = decay[idx].at[l].set(0.0)
    w0p = jnp.pad(w0, ((0, FILTER_EMB_PAD - FILTER_EMB), (0, 0)))
    hid = w0.shape[1]
    n_in = wi.shape[0]
    ct = _pick(d, 256)
    nct = d // ct
    return pl.pallas_call(
        functools.partial(_filter_kernel, l=l),
        out_shape=jax.ShapeDtypeStruct((2 * l, d), F32),
        grid=(nct,),
        in_specs=[pl.BlockSpec((2 * l, FILTER_EMB_PAD), lambda j: (0, 0)),
                  pl.BlockSpec((FILTER_EMB_PAD, hid), lambda j: (0, 0)),
                  pl.BlockSpec((1, hid), lambda j: (0, 0)),
                  pl.BlockSpec((n_in, hid, hid), lambda j: (0, 0, 0)),
                  pl.BlockSpec((n_in, 1, hid), lambda j: (0, 0, 0)),
                  pl.BlockSpec((1, hid), lambda j: (0, 0)),
                  pl.BlockSpec((hid, ct), lambda j: (0, j)),
                  pl.BlockSpec((hid, ct), lambda j: (0, nct + j)),
                  pl.BlockSpec((2 * l, ct), lambda j: (0, j))],
        out_specs=pl.BlockSpec((2 * l, ct), lambda j: (0, j)),
        compiler_params=_cparams(("arbitrary",)),
        name="hyena_filter",
    )(zc, w0p, b0.reshape(1, hid), wi, bi.reshape(n_in, 1, hid), freq.reshape(1, hid), wout, wout, dec)


@functools.lru_cache(maxsize=None)
def _dft_mats(p):
    n = 2 * p
    f = np.arange(p)[:, None]
    t = np.arange(n)[None, :]
    ang = 2.0 * np.pi * (((2 * f + 1) * t) % (4 * p)) / (4 * p)
    fwd = np.concatenate([np.cos(ang), -np.sin(ang)], axis=0)
    q = np.arange(p)[:, None]
    ff = np.arange(p)[None, :]
    ang2 = 2.0 * np.pi * (((2 * ff + 1) * (q + p)) % (4 * p)) / (4 * p)
    inv = np.concatenate([np.cos(ang2), -np.sin(ang2)], axis=1) / p
    return fwd.astype(np.float32), inv.astype(np.float32)


def _hyena_conv_kernel(z0_ref, z1_ref, z2_ref, w0_ref, w1_ref, w2_ref, b0_ref, b1_ref, b2_ref,
                       kf_ref, fb_ref, fwd_ref, inv_ref, o_ref, ks_ref, vs_ref, ys_ref, *, l, p, rc):
    nb = l // p
    b = pl.program_id(1)

    @pl.when(b == 0)
    def _():
        for di in range(2 * nb - 1):
            start = (p * (di - nb)) % (2 * l)
            if start + 2 * p <= 2 * l:
                seg = kf_ref[start:start + 2 * p, :]
            else:
                seg = jnp.concatenate([kf_ref[start:, :], kf_ref[:start + 2 * p - 2 * l, :]], axis=0)
            ks_ref[di] = jnp.dot(fwd_ref[...], seg.astype(BF16), preferred_element_type=F32)

    rows = lax.broadcasted_iota(jnp.int32, (l, z0_ref.shape[2]), 0)

    def sconv(z_ref, w_ref, b_ref):
        z = z_ref[0]
        w = w_ref[...]
        zm = jnp.where(rows == 0, 0.0, pltpu.roll(z, 1, 0))
        zp = jnp.where(rows == l - 1, 0.0, pltpu.roll(z, l - 1, 0))
        return zm * w[0:1] + z * w[1:2] + zp * w[2:3] + b_ref[...]

    u = sconv(z2_ref, w2_ref, b2_ref) * sconv(z1_ref, w1_ref, b1_ref)
    ub = u.astype(BF16)
    for j in range(nb):
        vs_ref[j] = jnp.dot(fwd_ref[:, :p], ub[j * p:(j + 1) * p], preferred_element_type=F32)

    def chunk(c, carry):
        r0 = pl.multiple_of(c * rc, rc)
        re = pl.ds(r0, rc)
        im = pl.ds(p + r0, rc)
        for i in range(nb):
            yr = None
            yi = None
            for j in range(nb):
                di = i - j + nb - 1
                kr = ks_ref[di, re, :]
                ki = ks_ref[di, im, :]
                vr = vs_ref[j, re, :]
                vi = vs_ref[j, im, :]
                tr = kr * vr - ki * vi
                ti = kr * vi + ki * vr
                yr = tr if yr is None else yr + tr
                yi = ti if yi is None else yi + ti
            ys_ref[i, re, :] = yr
            ys_ref[i, im, :] = yi
        return carry

    lax.fori_loop(0, p // rc, chunk, 0)

    x0 = sconv(z0_ref, w0_ref, b0_ref)
    fb = fb_ref[...]
    for i in range(nb):
        y = jnp.dot(inv_ref[...], ys_ref[i].astype(BF16), preferred_element_type=F32)
        sl = slice(i * p, (i + 1) * p)
        o_ref[0, sl, :] = (x0[sl] * (y + u[sl] * fb)).astype(BF16)


def _hyena_conv(z, sc_w, sc_b, k_full, f_bias, p_pref=512, ct_pref=128):
    bx, l, d3 = z.shape
    d = d3 // 3
    p = min(p_pref, l)
    nb = l // p
    ct = _pick(d, ct_pref)
    nct = d // ct
    fwd, inv = _dft_mats(p)
    fwd = jnp.asarray(fwd).astype(BF16)
    inv = jnp.asarray(inv).astype(BF16)
    zspec = lambda part: pl.BlockSpec((1, l, ct), lambda j, b, part=part: (b, 0, part * nct + j))
    wspec = lambda part: pl.BlockSpec((SHORT_CONV, ct), lambda j, b, part=part: (0, part * nct + j))
    bspec = lambda part: pl.BlockSpec((1, ct), lambda j, b, part=part: (0, part * nct + j))
    scb = sc_b.reshape(1, d3)
    return pl.pallas_call(
        functools.partial(_hyena_conv_kernel, l=l, p=p, rc=8),
        out_shape=jax.ShapeDtypeStruct((bx, l, d), BF16),
        grid=(nct, bx),
        in_specs=[zspec(0), zspec(1), zspec(2), wspec(0), wspec(1), wspec(2), bspec(0), bspec(1), bspec(2),
                  pl.BlockSpec((2 * l, ct), lambda j, b: (0, j)),
                  pl.BlockSpec((1, ct), lambda j, b: (0, j)),
                  pl.BlockSpec((2 * p, 2 * p), lambda j, b: (0, 0)),
                  pl.BlockSpec((p, 2 * p), lambda j, b: (0, 0))],
        out_specs=pl.BlockSpec((1, l, ct), lambda j, b: (b, 0, j)),
        scratch_shapes=[pltpu.VMEM((2 * nb - 1, 2 * p, ct), F32),
                        pltpu.VMEM((nb, 2 * p, ct), F32),
                        pltpu.VMEM((nb, 2 * p, ct), F32)],
        compiler_params=_cparams(("arbitrary", "arbitrary")),
        name="hyena_conv",
    )(z, z, z, sc_w, sc_w, sc_w, scb, scb, scb, k_full, f_bias.reshape(1, d), fwd, inv)


def _swiglu_step(h_bf16, w1_ref, w3_ref, w2_ref, acc_ref):
    a = jnp.dot(h_bf16, w1_ref[0].astype(BF16), preferred_element_type=F32)
    b = jnp.dot(h_bf16, w3_ref[0].astype(BF16), preferred_element_type=F32)
    g = (_silu(a) * b).astype(BF16)
    acc_ref[...] += jnp.dot(g, w2_ref[0].astype(BF16), preferred_element_type=F32)


def _ffn_dense_kernel(x_ref, mod_ref, g_ref, w1_ref, w3_ref, w2_ref, o_ref, h_ref, acc_ref,
                      *, sh_row, sc_row, gate_row):
    f = pl.program_id(2)

    @pl.when(f == 0)
    def _():
        mod = mod_ref[0]
        h = _rms(x_ref[0], g_ref[...]) * (1.0 + mod[sc_row:sc_row + 1]) + mod[sh_row:sh_row + 1]
        h_ref[...] = h.astype(BF16)
        acc_ref[...] = jnp.zeros_like(acc_ref)

    _swiglu_step(h_ref[...], w1_ref, w3_ref, w2_ref, acc_ref)

    @pl.when(f == pl.num_programs(2) - 1)
    def _():
        o_ref[0] = x_ref[0] + mod_ref[0][gate_row:gate_row + 1] * acc_ref[...]


def _ffn_dense(x, mod, gnorm, w1, w3, w2, sh_row, sc_row, gate_row, tm_pref=1024, tf_pref=512):
    bx, l, d = x.shape
    ff = w1.shape[1]
    tm = _pick(l, tm_pref)
    tf = _pick(ff, tf_pref)
    per_b = mod.shape[0] > 1
    return pl.pallas_call(
        functools.partial(_ffn_dense_kernel, sh_row=sh_row, sc_row=sc_row, gate_row=gate_row),
        out_shape=jax.ShapeDtypeStruct((bx, l, d), F32),
        grid=(bx, l // tm, ff // tf),
        in_specs=[pl.BlockSpec((1, tm, d), lambda b, i, f: (b, i, 0)),
                  pl.BlockSpec((1, MOD_ROWS, d), (lambda b, i, f: (b, 0, 0)) if per_b else (lambda b, i, f: (0, 0, 0))),
                  pl.BlockSpec((1, d), lambda b, i, f: (0, 0)),
                  pl.BlockSpec((1, d, tf), lambda b, i, f: (0, 0, f)),
                  pl.BlockSpec((1, d, tf), lambda b, i, f: (0, 0, f)),
                  pl.BlockSpec((1, tf, d), lambda b, i, f: (0, f, 0))],
        out_specs=pl.BlockSpec((1, tm, d), lambda b, i, f: (b, i, 0)),
        scratch_shapes=[pltpu.VMEM((tm, d), BF16), pltpu.VMEM((tm, d), F32)],
        compiler_params=_cparams(("arbitrary", "arbitrary", "arbitrary")),
        name="ffn_dense",
    )(x, mod, gnorm.reshape(1, d), w1[None], w3[None], w2[None])


def _ffn_expert_kernel(te_ref, tv_ref, x_ref, w1_ref, w3_ref, w2_ref, o_ref, h_ref, acc_ref):
    i = pl.program_id(0)
    f = pl.program_id(1)

    @pl.when(tv_ref[i] > 0)
    def _():
        @pl.when(f == 0)
        def _():
            h_ref[...] = x_ref[...].astype(BF16)
            acc_ref[...] = jnp.zeros_like(acc_ref)

        _swiglu_step(h_ref[...], w1_ref, w3_ref, w2_ref, acc_ref)

        @pl.when(f == pl.num_programs(1) - 1)
        def _():
            o_ref[...] = acc_ref[...]

    @pl.when((tv_ref[i] == 0) & (f == 0))
    def _():
        o_ref[...] = jnp.zeros_like(o_ref)


def _ffn_experts(xs, tile_expert, tile_valid, w1, w3, w2, tm, tf_pref=512):
    np_rows, d = xs.shape
    ff = w1.shape[2]
    tf = _pick(ff, tf_pref)
    grid_spec = pltpu.PrefetchScalarGridSpec(
        num_scalar_prefetch=2,
        grid=(np_rows // tm, ff // tf),
        in_specs=[pl.BlockSpec((tm, d), lambda i, f, te, tv: (i, 0)),
                  pl.BlockSpec((1, d, tf), lambda i, f, te, tv: (te[i], 0, f)),
                  pl.BlockSpec((1, d, tf), lambda i, f, te, tv: (te[i], 0, f)),
                  pl.BlockSpec((1, tf, d), lambda i, f, te, tv: (te[i], f, 0))],
        out_specs=pl.BlockSpec((tm, d), lambda i, f, te, tv: (i, 0)),
        scratch_shapes=[pltpu.VMEM((tm, d), BF16), pltpu.VMEM((tm, d), F32)],
    )
    return pl.pallas_call(
        _ffn_expert_kernel,
        out_shape=jax.ShapeDtypeStruct((np_rows, d), F32),
        grid_spec=grid_spec,
        compiler_params=_cparams(("arbitrary", "arbitrary")),
        name="ffn_experts",
    )(tile_expert, tile_valid, xs, w1, w3, w2)


def _mla_proj_kernel(x_ref, mod_ref, g_ref, wa_ref, qg_ref, kvg_ref, wqb_ref, wk_ref, wv_ref, tab_ref,
                     q_ref, k_ref, v_ref, *, q_lora, kv_lora):
    mod = mod_ref[0, 0]
    h = _rms(x_ref[0], g_ref[...]) * (1.0 + mod[1:2]) + mod[0:1]
    a = jnp.dot(h.astype(BF16), wa_ref[...], preferred_element_type=F32)
    qn = _rms(a[:, :q_lora], qg_ref[...]).astype(BF16)
    cn = _rms(a[:, q_lora:q_lora + kv_lora], kvg_ref[...]).astype(BF16)
    kpe = a[:, q_lora + kv_lora:]
    tab = tab_ref[...]
    cq, s1q, s2q, ck, s1k, s2k = (tab[:, n * LANES:(n + 1) * LANES] for n in range(6))
    q = jnp.dot(qn, wqb_ref[...], preferred_element_type=F32)
    wq = q.shape[1]
    rep = wq // LANES
    q = (q * jnp.tile(cq, (1, rep)) + pltpu.roll(q, ROPE_AXIS, 1) * jnp.tile(s1q, (1, rep))
         + pltpu.roll(q, wq - ROPE_AXIS, 1) * jnp.tile(s2q, (1, rep)))
    q_ref[0] = q.astype(BF16)
    kr = kpe * ck + pltpu.roll(kpe, ROPE_AXIS, 1) * s1k + pltpu.roll(kpe, LANES - ROPE_AXIS, 1) * s2k
    k = jnp.dot(cn, wk_ref[...], preferred_element_type=F32) + jnp.tile(kr, (1, rep))
    k_ref[0] = k.astype(BF16)
    v_ref[0] = jnp.dot(cn, wv_ref[...], preferred_element_type=F32).astype(BF16)


def _rope_tables(l, n_ctx):
    rows = l // GRID_W
    row = jnp.broadcast_to(jnp.arange(rows, dtype=F32)[:, None], (rows, GRID_W)).reshape(l)
    col = jnp.broadcast_to(jnp.arange(GRID_W, dtype=F32)[None, :], (rows, GRID_W)).reshape(l)
    inv = ROPE_BASE ** (-jnp.arange(0, ROPE_AXIS, 2, dtype=F32) / ROPE_AXIS)
    ang = jnp.concatenate([row[:, None] * inv, col[:, None] * inv], axis=-1)
    cos = jnp.concatenate([jnp.ones((n_ctx, ROPE_AXIS), F32), jnp.cos(ang)], axis=0)
    sin = jnp.concatenate([jnp.zeros((n_ctx, ROPE_AXIS), F32), jnp.sin(ang)], axis=0)
    n = n_ctx + l
    ones = jnp.ones((n, QK_NOPE), F32)
    z16 = jnp.zeros((n, ROPE_AXIS), F32)
    z64 = jnp.zeros((n, QK_NOPE), F32)
    zpad = jnp.zeros((n, HEAD_SLOT - QK_HEAD), F32)
    c = jnp.concatenate([ones, cos, cos, zpad], axis=1)
    s1 = jnp.concatenate([z64, z16, sin, zpad], axis=1)
    s2 = jnp.concatenate([z64, -sin, z16, zpad], axis=1)
    scale = 1.0 / math.sqrt(QK_HEAD)
    return jnp.concatenate([c * scale, s1 * scale, s2 * scale, c, s1, s2], axis=1)


def _mla_weights(wq_a, wq_b, wkv_a, wkv_b):
    d, q_lora = wq_a.shape
    kv_lora = wkv_a.shape[1] - QK_ROPE
    ev = np.arange(0, QK_ROPE, 2)
    od = np.arange(1, QK_ROPE, 2)
    kpe = wkv_a[:, kv_lora:]
    kpe_slot = jnp.concatenate([jnp.zeros((d, QK_NOPE), F32), kpe[:, ev], kpe[:, od],
                                jnp.zeros((d, HEAD_SLOT - QK_HEAD), F32)], axis=1)
    wa = jnp.concatenate([wq_a, wkv_a[:, :kv_lora], kpe_slot], axis=1)
    qb = wq_b.reshape(q_lora, N_HEADS, QK_HEAD)
    qb = jnp.concatenate([qb[..., :QK_NOPE], qb[..., QK_NOPE + ev], qb[..., QK_NOPE + od],
                          jnp.zeros((q_lora, N_HEADS, HEAD_SLOT - QK_HEAD), F32)], axis=-1)
    kvb = wkv_b.reshape(kv_lora, N_HEADS, QK_NOPE + V_HEAD)
    wk = jnp.concatenate([kvb[..., :QK_NOPE], jnp.zeros((kv_lora, N_HEADS, HEAD_SLOT - QK_NOPE), F32)], axis=-1)
    wv = kvb[..., QK_NOPE:]
    return (wa.astype(BF16), qb.reshape(q_lora, N_HEADS * HEAD_SLOT).astype(BF16),
            wk.reshape(kv_lora, N_HEADS * HEAD_SLOT).astype(BF16),
            wv.reshape(kv_lora, N_HEADS * V_HEAD).astype(BF16))


def _mla_proj(tok, mod_pair, gnorm, wa, q_norm, kv_norm, wqb, wk, wv, tables, n_ctx):
    bx, n, d = tok.shape
    tm = _pick(n_ctx, 256)
    q_lora = q_norm.shape[0]
    kv_lora = kv_norm.shape[0]
    wq = wqb.shape[1]
    wvn = wv.shape[1]
    ctx_tiles = n_ctx // tm
    const = lambda b, i: (0, 0)
    return pl.pallas_call(
        functools.partial(_mla_proj_kernel, q_lora=q_lora, kv_lora=kv_lora),
        out_shape=(jax.ShapeDtypeStruct((bx, n, wq), BF16),
                   jax.ShapeDtypeStruct((bx, n, wq), BF16),
                   jax.ShapeDtypeStruct((bx, n, wvn), BF16)),
        grid=(bx, n // tm),
        in_specs=[pl.BlockSpec((1, tm, d), lambda b, i: (b, i, 0)),
                  pl.BlockSpec((1, 1, MOD_ROWS, d), lambda b, i: (b, jnp.where(i < ctx_tiles, 0, 1), 0, 0)),
                  pl.BlockSpec((1, d), const),
                  pl.BlockSpec(wa.shape, const),
                  pl.BlockSpec((1, q_lora), const),
                  pl.BlockSpec((1, kv_lora), const),
                  pl.BlockSpec(wqb.shape, const),
                  pl.BlockSpec(wk.shape, const),
                  pl.BlockSpec(wv.shape, const),
                  pl.BlockSpec((tm, 6 * LANES), lambda b, i: (i, 0))],
        out_specs=(pl.BlockSpec((1, tm, wq), lambda b, i: (b, i, 0)),
                   pl.BlockSpec((1, tm, wq), lambda b, i: (b, i, 0)),
                   pl.BlockSpec((1, tm, wvn), lambda b, i: (b, i, 0))),
        compiler_params=_cparams(("arbitrary", "arbitrary")),
        name="mla_proj",
    )(tok, mod_pair, gnorm.reshape(1, d), wa, q_norm.reshape(1, q_lora), kv_norm.reshape(1, kv_lora),
      wqb, wk, wv, tables)


def _attn_kernel(q_ref, k_ref, v_ref, o_ref):
    q2 = q_ref[0]
    k2 = k_ref[0]
    v2 = v_ref[0]
    outs = []
    for h in range(2):
        q = q2[:, h * HEAD_SLOT:(h + 1) * HEAD_SLOT]
        k = k2[:, h * HEAD_SLOT:(h + 1) * HEAD_SLOT]
        s = lax.dot_general(q, k, (((1,), (1,)), ((), ())), preferred_element_type=F32)
        m = jnp.max(s, axis=-1, keepdims=True)
        e = jnp.exp(s - m)
        den = jnp.sum(e, axis=-1, keepdims=True)
        outs.append(jnp.dot(e.astype(BF16), v2, preferred_element_type=F32) / den)
    lane = lax.broadcasted_iota(jnp.int32, outs[0].shape, 1)
    o_ref[0] = jnp.where(lane < V_HEAD, outs[0], outs[1]).astype(BF16)


def _attention(q, k, v, l, n_ctx, tq_pref=256):
    bx, n, _ = q.shape
    tq = _pick(n_ctx, tq_pref)
    off = n_ctx // tq
    hp = N_HEADS // 2
    return pl.pallas_call(
        _attn_kernel,
        out_shape=jax.ShapeDtypeStruct((bx, l, N_HEADS * V_HEAD), BF16),
        grid=(bx, hp, l // tq),
        in_specs=[pl.BlockSpec((1, tq, 2 * HEAD_SLOT), lambda b, h, i: (b, i + off, h)),
                  pl.BlockSpec((1, n, 2 * HEAD_SLOT), lambda b, h, i: (b, 0, h)),
                  pl.BlockSpec((1, n, 2 * V_HEAD), lambda b, h, i: (b, 0, h))],
        out_specs=pl.BlockSpec((1, tq, 2 * V_HEAD), lambda b, h, i: (b, i, h)),
        compiler_params=_cparams(("arbitrary", "arbitrary", "arbitrary")),
        name="mla_attention",
    )(q, k, v)


def _route_kernel(x_ref, mod_ref, g_ref, r_ref, tri_ref, h_ref, meta_ref, cnt_ref, carry_ref, *, n_exp):
    i = pl.program_id(0)

    @pl.when(i == 0)
    def _():
        carry_ref[...] = jnp.zeros_like(carry_ref)

    mod = mod_ref[0]
    h = _rms(x_ref[...], g_ref[...]) * (1.0 + mod[4:5]) + mod[3:4]
    h_ref[...] = h
    logits = jnp.dot(h, r_ref[...], precision=HIGHEST, preferred_element_type=F32)
    lane = lax.broadcasted_iota(jnp.int32, logits.shape, 1)
    lane_f = lane.astype(F32)
    neg = jnp.float32(-jnp.inf)
    lg = jnp.where(lane < n_exp, logits, neg)
    v1 = jnp.max(lg, axis=-1, keepdims=True)
    i1 = jnp.min(jnp.where(lg == v1, lane_f, float(LANES)), axis=-1, keepdims=True)
    oh1 = lane_f == i1
    lg2 = jnp.where(oh1, neg, lg)
    v2 = jnp.max(lg2, axis=-1, keepdims=True)
    i2 = jnp.min(jnp.where(lg2 == v2, lane_f, float(LANES)), axis=-1, keepdims=True)
    oh2 = lane_f == i2
    e = jnp.exp(v2 - v1)
    g1 = 1.0 / (1.0 + e)
    g2 = e / (1.0 + e)
    oh = jnp.where(oh1 | oh2, 1.0, 0.0)
    pref = jnp.dot(tri_ref[...], oh.astype(BF16), preferred_element_type=F32)
    excl = pref - oh + carry_ref[...]
    r1 = jnp.sum(jnp.where(oh1, excl, 0.0), axis=-1, keepdims=True)
    r2 = jnp.sum(jnp.where(oh2, excl, 0.0), axis=-1, keepdims=True)
    carry_ref[...] += jnp.sum(oh, axis=0, keepdims=True)
    cnt_ref[...] = carry_ref[...]
    meta = jnp.where(lane == 0, i1, 0.0)
    meta = jnp.where(lane == 1, i2, meta)
    meta = jnp.where(lane == 2, g1, meta)
    meta = jnp.where(lane == 3, g2, meta)
    meta = jnp.where(lane == 4, r1, meta)
    meta = jnp.where(lane == 5, r2, meta)
    meta_ref[...] = meta


def _route(x_flat, mod, gnorm, router, tokens_per_batch, tr_pref=512):
    n, d = x_flat.shape
    n_exp = router.shape[1]
    tr = _pick(tokens_per_batch, tr_pref)
    per_b = tokens_per_batch // tr
    rpad = jnp.pad(router, ((0, 0), (0, LANES - n_exp)))
    tri = jnp.asarray(np.tril(np.ones((tr, tr), np.float32))).astype(BF16)
    return pl.pallas_call(
        functools.partial(_route_kernel, n_exp=n_exp),
        out_shape=(jax.ShapeDtypeStruct((n, d), F32),
                   jax.ShapeDtypeStruct((n, LANES), F32),
                   jax.ShapeDtypeStruct((1, LANES), F32)),
        grid=(n // tr,),
        in_specs=[pl.BlockSpec((tr, d), lambda i: (i, 0)),
                  pl.BlockSpec((1, MOD_ROWS, d), lambda i: (i // per_b, 0, 0)),
                  pl.BlockSpec((1, d), lambda i: (0, 0)),
                  pl.BlockSpec((d, LANES), lambda i: (0, 0)),
                  pl.BlockSpec((tr, tr), lambda i: (0, 0))],
        out_specs=(pl.BlockSpec((tr, d), lambda i: (i, 0)),
                   pl.BlockSpec((tr, LANES), lambda i: (i, 0)),
                   pl.BlockSpec((1, LANES), lambda i: (0, 0))),
        scratch_shapes=[pltpu.VMEM((1, LANES), F32)],
        compiler_params=_cparams(("arbitrary",)),
        name="moe_route",
    )(x_flat, mod, gnorm.reshape(1, d), rpad, tri)


def _row_copy(src_ref, src_row, dst_ref, dst_row, sem):
    return pltpu.make_async_copy(src_ref.at[pl.ds(src_row, 1)], dst_ref.at[pl.ds(dst_row, 1)], sem)


def _dispatch_kernel(pos_ref, h_ref, xs_in_ref, xs_ref, sem, *, td):
    del xs_in_ref
    base = pl.program_id(0) * td

    def issue(t, c):
        for k in range(TOP_K):
            _row_copy(h_ref, base + t, xs_ref, pos_ref[0, 0, k * td + t], sem).start()
        return c

    lax.fori_loop(0, td, issue, 0)

    def drain(t, c):
        for k in range(TOP_K):
            _row_copy(h_ref, base + t, xs_ref, pos_ref[0, 0, k * td + t], sem).wait()
        return c

    lax.fori_loop(0, td, drain, 0)


def _dispatch(h, pos_tiles, np_rows, td):
    n, d = h.shape
    xs0 = jnp.zeros((np_rows, d), F32)
    return pl.pallas_call(
        functools.partial(_dispatch_kernel, td=td),
        out_shape=jax.ShapeDtypeStruct((np_rows, d), F32),
        grid=(n // td,),
        in_specs=[pl.BlockSpec((1, 1, TOP_K * td), lambda i: (i, 0, 0), memory_space=pltpu.SMEM),
                  pl.BlockSpec(memory_space=pl.ANY),
                  pl.BlockSpec(memory_space=pl.ANY)],
        out_specs=pl.BlockSpec(memory_space=pl.ANY),
        scratch_shapes=[pltpu.SemaphoreType.DMA(())],
        input_output_aliases={2: 0},
        compiler_params=_cparams(("arbitrary",)),
        name="moe_dispatch",
    )(pos_tiles, h, xs0)


def _combine_kernel(pos_ref, ys_ref, meta_ref, x_ref, mod_ref, g_ref, o_ref, buf_ref, sem, *, td):
    def issue(t, c):
        for k in range(TOP_K):
            pltpu.make_async_copy(ys_ref.at[pl.ds(pos_ref[0, 0, k * td + t], 1)],
                                  buf_ref.at[k, pl.ds(t, 1)], sem).start()
        return c

    lax.fori_loop(0, td, issue, 0)

    def drain(t, c):
        for k in range(TOP_K):
            pltpu.make_async_copy(ys_ref.at[pl.ds(pos_ref[0, 0, k * td + t], 1)],
                                  buf_ref.at[k, pl.ds(t, 1)], sem).wait()
        return c

    lax.fori_loop(0, td, drain, 0)

    meta = meta_ref[...]
    y = meta[:, 2:3] * buf_ref[0] + meta[:, 3:4] * buf_ref[1]
    x = x_ref[...] + mod_ref[0][5:6] * y
    o_ref[...] = _rms(x, g_ref[...])


def _combine(ys, pos_tiles, meta, x_flat, mod, norm_final, tokens_per_batch, td):
    n, d = x_flat.shape
    per_b = tokens_per_batch // td
    return pl.pallas_call(
        functools.partial(_combine_kernel, td=td),
        out_shape=jax.ShapeDtypeStruct((n, d), F32),
        grid=(n // td,),
        in_specs=[pl.BlockSpec((1, 1, TOP_K * td), lambda i: (i, 0, 0), memory_space=pltpu.SMEM),
                  pl.BlockSpec(memory_space=pl.ANY),
                  pl.BlockSpec((td, LANES), lambda i: (i, 0)),
                  pl.BlockSpec((td, d), lambda i: (i, 0)),
                  pl.BlockSpec((1, MOD_ROWS, d), lambda i: (i // per_b, 0, 0)),
                  pl.BlockSpec((1, d), lambda i: (0, 0))],
        out_specs=pl.BlockSpec((td, d), lambda i: (i, 0)),
        scratch_shapes=[pltpu.VMEM((TOP_K, td, d), F32), pltpu.SemaphoreType.DMA(())],
        compiler_params=_cparams(("arbitrary",)),
        name="moe_combine",
    )(pos_tiles, ys, meta, x_flat, mod, norm_final.reshape(1, d))


def _moe(x, mod, gnorm, router, w1, w3, w2, norm_final, tm=1024, td=256):
    bx, l, d = x.shape
    n = bx * l
    n_exp = router.shape[1]
    x_flat = x.reshape(n, d)
    h, meta, counts = _route(x_flat, mod, gnorm, router, l)
    idx = meta[:, 0:TOP_K].astype(jnp.int32)
    rank = meta[:, 4:4 + TOP_K].astype(jnp.int32)
    cnt = counts[0, :n_exp].astype(jnp.int32)
    tiles_e = (cnt + tm - 1) // tm
    tile_end = jnp.cumsum(tiles_e)
    start_rows = (tile_end - tiles_e) * tm
    pos = start_rows[idx] + rank
    n_tiles = (n * TOP_K) // tm + n_exp
    np_rows = n_tiles * tm
    tile_ids = jnp.arange(n_tiles, dtype=jnp.int32)
    used = tile_end[-1]
    tile_expert = jnp.sum((jnp.minimum(tile_ids, used - 1)[:, None] >= tile_end[None, :]).astype(jnp.int32), axis=1)
    tile_expert = jnp.minimum(tile_expert, n_exp - 1)
    tile_valid = (tile_ids < used).astype(jnp.int32)
    pos_tiles = pos.reshape(n // td, td, TOP_K).transpose(0, 2, 1).reshape(n // td, 1, TOP_K * td)
    xs = _dispatch(h, pos_tiles, np_rows, td)
    ys = _ffn_experts(xs, tile_expert, tile_valid, w1, w3, w2, tm)
    out = _combine(ys, pos_tiles, meta, x_flat, mod, norm_final, l, td)
    return out.reshape(bx, l, d)


def _mod_rows(m):
    r, n = m.shape
    return jnp.pad(m.reshape(r, N_MOD, n // N_MOD), ((0, 0), (0, MOD_ROWS - N_MOD), (0, 0)))


def _hyena_layer(x, mod, gnorm, in_w, in_b, sc_w, sc_b, k_full, f_bias, out_w, out_b):
    z = _norm_mod_matmul(x, mod, gnorm, in_w, in_b, 0, 1)
    g = _hyena_conv(z, sc_w, sc_b, k_full, f_bias)
    return _matmul_gated_residual(g, out_w, out_b, x, mod, 2)


@jax.jit
def kernel(x, c, ctx, c_ctx, ada_w, ada_b, norm_mix, norm_ffn, hy_in_w, hy_in_b, hy_sc_w, hy_sc_b, hy_f_w0, hy_f_b0, hy_f_wi, hy_f_bi, hy_f_freq, hy_f_wout, hy_f_bias, hy_out_w, hy_out_b, mla_wq_a, mla_q_norm, mla_wq_b, mla_wkv_a, mla_kv_norm, mla_wkv_b, mla_wo, ffn_w1, ffn_w3, ffn_w2, moe_router, moe_w1, moe_w3, moe_w2, norm_final):
    bsz, l, d = x.shape
    n_ctx = ctx.shape[1]
    depth = ada_w.shape[0]
    assert depth == 2, "layer 0 = Hyena + dense SwiGLU, layer 1 = MLA + expert SwiGLU"

    rows = -(-(bsz + 1) // 8) * 8
    cvec = jnp.zeros((rows, d), F32).at[:bsz].set(c).at[bsz].set(c_ctx)
    mods = _ada_mod(cvec, ada_w, ada_b)
    modx = [_mod_rows(mods[i, :bsz]) for i in range(depth)]
    modc = [_mod_rows(mods[i, bsz:bsz + 1]) for i in range(depth)]

    in_w = hy_in_w[0].astype(BF16)
    out_w = hy_out_w[0].astype(BF16)
    fargs = (hy_f_w0[0], hy_f_b0[0], hy_f_wi[0], hy_f_bi[0], hy_f_freq[0], hy_f_wout[0])
    kx = _hyena_filter_full(l, d, *fargs)
    kc = _hyena_filter_full(n_ctx, d, *fargs)
    x = _hyena_layer(x, modx[0], norm_mix[0], in_w, hy_in_b[0], hy_sc_w[0], hy_sc_b[0], kx,
                     hy_f_bias[0], out_w, hy_out_b[0])
    ctx = _hyena_layer(ctx, modc[0], norm_mix[0], in_w, hy_in_b[0], hy_sc_w[0], hy_sc_b[0], kc,
                       hy_f_bias[0], out_w, hy_out_b[0])
    x = _ffn_dense(x, modx[0], norm_ffn[0], ffn_w1[0], ffn_w3[0], ffn_w2[0], 3, 4, 5)
    ctx = _ffn_dense(ctx.reshape(1, bsz * n_ctx, d), modc[0], norm_ffn[0], ffn_w1[0], ffn_w3[0], ffn_w2[0],
                     3, 4, 5).reshape(bsz, n_ctx, d)

    tok = jnp.concatenate([ctx, x], axis=1)
    mod_pair = jnp.stack([jnp.broadcast_to(modc[1], modx[1].shape), modx[1]], axis=1)
    wa, wqb, wk, wv = _mla_weights(mla_wq_a[0], mla_wq_b[0], mla_wkv_a[0], mla_wkv_b[0])
    tables = _rope_tables(l, n_ctx)
    q, k, v = _mla_proj(tok, mod_pair, norm_mix[1], wa, mla_q_norm[0], mla_kv_norm[0], wqb, wk, wv, tables, n_ctx)
    o = _attention(q, k, v, l, n_ctx)
    x = _matmul_gated_residual(o, mla_wo[0].astype(BF16), jnp.zeros((d,), F32), x, modx[1], 2)
    return _moe(x, modx[1], norm_ffn[1], moe_router[0], moe_w1[0], moe_w3[0], moe_w2[0], norm_final)
```

```python
import functools
import math

import jax
import jax.numpy as jnp
import numpy as np
from jax import lax
from jax.experimental import pallas as pl
from jax.experimental.pallas import tpu as pltpu

F32 = jnp.float32
BF16 = jnp.bfloat16
HIGHEST = lax.Precision.HIGHEST

RMS_EPS = 1e-6
N_MOD = 6
MOD_ROWS = 8
GRID_W = 64
SHORT_CONV = 3
FILTER_BANDS = 8
FILTER_EMB = 1 + 2 * FILTER_BANDS
FILTER_EMB_PAD = 32
DECAY_TARGET = 1e-2
FAST_DECAY_PCT = 0.3
SLOW_DECAY_PCT = 1.5
N_HEADS = 16
QK_NOPE = 64
QK_ROPE = 32
QK_HEAD = QK_NOPE + QK_ROPE
V_HEAD = 64
ROPE_AXIS = QK_ROPE // 2
ROPE_BASE = 10000.0
TOP_K = 2
LANES = 128
HEAD_SLOT = 128
VMEM_LIMIT = 56 * 1024 * 1024


def _cparams(sem, vmem=VMEM_LIMIT):
    return pltpu.CompilerParams(dimension_semantics=sem, vmem_limit_bytes=vmem)


def _rms(x, g):
    return x * lax.rsqrt(jnp.mean(x * x, axis=-1, keepdims=True) + RMS_EPS) * g


def _silu(x):
    return x * (1.0 / (1.0 + jnp.exp(-x)))


def _pick(total, pref):
    t = min(total, pref)
    while total % t:
        t //= 2
    return t


def _ada_kernel(c_ref, w_ref, b_ref, o_ref):
    c = c_ref[...]
    o_ref[0] = jnp.dot(_silu(c), w_ref[0], precision=HIGHEST, preferred_element_type=F32) + b_ref[0]


def _ada_mod(cvec, ada_w, ada_b):
    depth, d, n = ada_w.shape
    r = cvec.shape[0]
    tn = _pick(n, 1536)
    return pl.pallas_call(
        _ada_kernel,
        out_shape=jax.ShapeDtypeStruct((depth, r, n), F32),
        grid=(depth, n // tn),
        in_specs=[pl.BlockSpec((r, d), lambda i, j: (0, 0)),
                  pl.BlockSpec((1, d, tn), lambda i, j: (i, 0, j)),
                  pl.BlockSpec((1, 1, tn), lambda i, j: (i, 0, j))],
        out_specs=pl.BlockSpec((1, r, tn), lambda i, j: (i, 0, j)),
        compiler_params=_cparams(("arbitrary", "arbitrary")),
        name="ada_mod",
    )(cvec, ada_w, ada_b.reshape(depth, 1, n))


def _nmm_kernel(x_ref, mod_ref, g_ref, w_ref, b_ref, o_ref, *, sh_row, sc_row):
    x = x_ref[0]
    mod = mod_ref[0]
    h = _rms(x, g_ref[...]) * (1.0 + mod[sc_row:sc_row + 1]) + mod[sh_row:sh_row + 1]
    o_ref[0] = jnp.dot(h.astype(BF16), w_ref[...], preferred_element_type=F32) + b_ref[...]


def _norm_mod_matmul(x, mod, gnorm, w_bf16, bias, sh_row, sc_row, tm_pref=512):
    bx, l, d = x.shape
    n = w_bf16.shape[1]
    tm = _pick(l, tm_pref)
    per_b = mod.shape[0] > 1
    return pl.pallas_call(
        functools.partial(_nmm_kernel, sh_row=sh_row, sc_row=sc_row),
        out_shape=jax.ShapeDtypeStruct((bx, l, n), F32),
        grid=(bx, l // tm),
        in_specs=[pl.BlockSpec((1, tm, d), lambda b, i: (b, i, 0)),
                  pl.BlockSpec((1, MOD_ROWS, d), (lambda b, i: (b, 0, 0)) if per_b else (lambda b, i: (0, 0, 0))),
                  pl.BlockSpec((1, d), lambda b, i: (0, 0)),
                  pl.BlockSpec((d, n), lambda b, i: (0, 0)),
                  pl.BlockSpec((1, n), lambda b, i: (0, 0))],
        out_specs=pl.BlockSpec((1, tm, n), lambda b, i: (b, i, 0)),
        compiler_params=_cparams(("arbitrary", "arbitrary")),
        name="norm_mod_matmul",
    )(x, mod, gnorm.reshape(1, d), w_bf16, bias.reshape(1, n))


def _mm_res_kernel(a_ref, w_ref, b_ref, r_ref, mod_ref, o_ref, *, gate_row):
    y = jnp.dot(a_ref[0], w_ref[...], preferred_element_type=F32) + b_ref[...]
    o_ref[0] = r_ref[0] + mod_ref[0][gate_row:gate_row + 1] * y


def _matmul_gated_residual(a_bf16, w_bf16, bias, resid, mod, gate_row, tm_pref=512):
    bx, l, k = a_bf16.shape
    d = w_bf16.shape[1]
    tm = _pick(l, tm_pref)
    per_b = mod.shape[0] > 1
    return pl.pallas_call(
        functools.partial(_mm_res_kernel, gate_row=gate_row),
        out_shape=jax.ShapeDtypeStruct((bx, l, d), F32),
        grid=(bx, l // tm),
        in_specs=[pl.BlockSpec((1, tm, k), lambda b, i: (b, i, 0)),
                  pl.BlockSpec((k, d), lambda b, i: (0, 0)),
                  pl.BlockSpec((1, d), lambda b, i: (0, 0)),
                  pl.BlockSpec((1, tm, d), lambda b, i: (b, i, 0)),
                  pl.BlockSpec((1, MOD_ROWS, d), (lambda b, i: (b, 0, 0)) if per_b else (lambda b, i: (0, 0, 0)))],
        out_specs=pl.BlockSpec((1, tm, d), lambda b, i: (b, i, 0)),
        compiler_params=_cparams(("arbitrary", "arbitrary")),
        name="matmul_gated_residual",
    )(a_bf16, w_bf16, bias.reshape(1, d), resid, mod)


def _filter_kernel(z_ref, w0_ref, b0_ref, wi_ref, bi_ref, fr_ref, wt_ref, wb_ref, dec_ref, o_ref, h_ref, *, l):
    @pl.when(pl.program_id(0) == 0)
    def _():
        fr = fr_ref[...]
        h = jnp.sin(fr * (jnp.dot(z_ref[...], w0_ref[...], precision=HIGHEST, preferred_element_type=F32)
                          + b0_ref[...]))
        for n in range(wi_ref.shape[0]):
            h = jnp.sin(fr * (jnp.dot(h, wi_ref[n], precision=HIGHEST, preferred_element_type=F32) + bi_ref[n]))
        h_ref[...] = h

    top = jnp.dot(h_ref[:l], wt_ref[...], precision=HIGHEST, preferred_element_type=F32)
    bot = jnp.dot(h_ref[l:], wb_ref[...], precision=HIGHEST, preferred_element_type=F32)
    k = jnp.concatenate([top, bot], axis=0) * dec_ref[...]
    o_ref[...] = k / jnp.sum(jnp.abs(k), axis=0, keepdims=True)


def _hyena_filter_full(l, d, w0, b0, wi, bi, freq, wout):
    pos = jnp.arange(l, dtype=F32)
    t = (pos / max(l - 1, 1))[:, None]
    w = 2.0 * math.pi * pos / l
    f = jnp.linspace(1e-4, FILTER_BANDS - 1, FILTER_BANDS, dtype=F32)
    ang = w[:, None] * f[None, :]
    z = jnp.concatenate([t, jnp.cos(ang), -jnp.sin(ang)], axis=-1)
    deltas = jnp.abs(jnp.linspace(math.log(DECAY_TARGET) / SLOW_DECAY_PCT,
                                  math.log(DECAY_TARGET) / FAST_DECAY_PCT, d, dtype=F32))
    decay = jnp.exp(-t * deltas[None, :])
    idx = np.concatenate([np.arange(l), [0], np.arange(l - 1, 0, -1)])
    zc = jnp.pad(z[idx], ((0, 0), (0, FILTER_EMB_PAD - FILTER_EMB)))
    dec = decay[idx].at[l].set(0.0)
    w0p = jnp.pad(w0, ((0, FILTER_EMB_PAD - FILTER_EMB), (0, 0)))
    hid = w0.shape[1]
    n_in = wi.shape[0]
    ct = _pick(d, 256)
    nct = d // ct
    return pl.pallas_call(
        functools.partial(_filter_kernel, l=l),
        out_shape=jax.ShapeDtypeStruct((2 * l, d), F32),
        grid=(nct,),
        in_specs=[pl.BlockSpec((2 * l, FILTER_EMB_PAD), lambda j: (0, 0)),
                  pl.BlockSpec((FILTER_EMB_PAD, hid), lambda j: (0, 0)),
                  pl.BlockSpec((1, hid), lambda j: (0, 0)),
                  pl.BlockSpec((n_in, hid, hid), lambda j: (0, 0, 0)),
                  pl.BlockSpec((n_in, 1, hid), lambda j: (0, 0, 0)),
                  pl.BlockSpec((1, hid), lambda j: (0, 0)),
                  pl.BlockSpec((hid, ct), lambda j: (0, j)),
                  pl.BlockSpec((hid, ct), lambda j: (0, nct + j)),
                  pl.BlockSpec((2 * l, ct), lambda j: (0, j))],
        out_specs=pl.BlockSpec((2 * l, ct), lambda j: (0, j)),
        scratch_shapes=[pltpu.VMEM((2 * l, hid), F32)],
        compiler_params=_cparams(("arbitrary",)),
        name="hyena_filter",
    )(zc, w0p, b0.reshape(1, hid), wi, bi.reshape(n_in, 1, hid), freq.reshape(1, hid), wout, wout, dec)


@functools.lru_cache(maxsize=None)
def _dft_mats(p):
    n = 2 * p
    f = np.arange(p)[:, None]
    t = np.arange(n)[None, :]
    ang = 2.0 * np.pi * (((2 * f + 1) * t) % (4 * p)) / (4 * p)
    fwd = np.concatenate([np.cos(ang), -np.sin(ang)], axis=0)
    q = np.arange(p)[:, None]
    ff = np.arange(p)[None, :]
    ang2 = 2.0 * np.pi * (((2 * ff + 1) * (q + p)) % (4 * p)) / (4 * p)
    inv = np.concatenate([np.cos(ang2), -np.sin(ang2)], axis=1) / p
    return fwd.astype(np.float32), inv.astype(np.float32)


def _hyena_conv_kernel(z0_ref, z1_ref, z2_ref, w0_ref, w1_ref, w2_ref, b0_ref, b1_ref, b2_ref,
                       kf_ref, fb_ref, fwd_ref, inv_ref, o_ref, ks_ref, vs_ref, ys_ref, *, l, p, rc):
    nb = l // p
    b = pl.program_id(1)

    @pl.when(b == 0)
    def _():
        for di in range(2 * nb - 1):
            start = (p * (di - nb)) % (2 * l)
            if start + 2 * p <= 2 * l:
                seg = kf_ref[start:start + 2 * p, :]
            else:
                seg = jnp.concatenate([kf_ref[start:, :], kf_ref[:start + 2 * p - 2 * l, :]], axis=0)
            ks_ref[di] = jnp.dot(fwd_ref[...], seg.astype(BF16), preferred_element_type=F32)

    rows = lax.broadcasted_iota(jnp.int32, (l, z0_ref.shape[2]), 0)

    def sconv(z_ref, w_ref, b_ref):
        z = z_ref[0]
        w = w_ref[...]
        zm = jnp.where(rows == 0, 0.0, pltpu.roll(z, 1, 0))
        zp = jnp.where(rows == l - 1, 0.0, pltpu.roll(z, l - 1, 0))
        return zm * w[0:1] + z * w[1:2] + zp * w[2:3] + b_ref[...]

    u = sconv(z2_ref, w2_ref, b2_ref) * sconv(z1_ref, w1_ref, b1_ref)
    ub = u.astype(BF16)
    for j in range(nb):
        vs_ref[j] = jnp.dot(fwd_ref[:, :p], ub[j * p:(j + 1) * p], preferred_element_type=F32)

    def chunk(c, carry):
        r0 = pl.multiple_of(c * rc, rc)
        re = pl.ds(r0, rc)
        im = pl.ds(p + r0, rc)
        for i in range(nb):
            yr = None
            yi = None
            for j in range(nb):
                di = i - j + nb - 1
                kr = ks_ref[di, re, :]
                ki = ks_ref[di, im, :]
                vr = vs_ref[j, re, :]
                vi = vs_ref[j, im, :]
                tr = kr * vr - ki * vi
                ti = kr * vi + ki * vr
                yr = tr if yr is None else yr + tr
                yi = ti if yi is None else yi + ti
            ys_ref[i, re, :] = yr
            ys_ref[i, im, :] = yi
        return carry

    lax.fori_loop(0, p // rc, chunk, 0)

    x0 = sconv(z0_ref, w0_ref, b0_ref)
    fb = fb_ref[...]
    for i in range(nb):
        y = jnp.dot(inv_ref[...], ys_ref[i].astype(BF16), preferred_element_type=F32)
        sl = slice(i * p, (i + 1) * p)
        o_ref[0, sl, :] = (x0[sl] * (y + u[sl] * fb)).astype(BF16)


def _hyena_conv(z, sc_w, sc_b, k_full, f_bias, p_pref=512, ct_pref=128):
    bx, l, d3 = z.shape
    d = d3 // 3
    p = min(p_pref, l)
    nb = l // p
    ct = _pick(d, ct_pref)
    nct = d // ct
    fwd, inv = _dft_mats(p)
    fwd = jnp.asarray(fwd).astype(BF16)
    inv = jnp.asarray(inv).astype(BF16)
    zspec = lambda part: pl.BlockSpec((1, l, ct), lambda j, b, part=part: (b, 0, part * nct + j))
    wspec = lambda part: pl.BlockSpec((SHORT_CONV, ct), lambda j, b, part=part: (0, part * nct + j))
    bspec = lambda part: pl.BlockSpec((1, ct), lambda j, b, part=part: (0, part * nct + j))
    scb = sc_b.reshape(1, d3)
    return pl.pallas_call(
        functools.partial(_hyena_conv_kernel, l=l, p=p, rc=8),
        out_shape=jax.ShapeDtypeStruct((bx, l, d), BF16),
        grid=(nct, bx),
        in_specs=[zspec(0), zspec(1), zspec(2), wspec(0), wspec(1), wspec(2), bspec(0), bspec(1), bspec(2),
                  pl.BlockSpec((2 * l, ct), lambda j, b: (0, j)),
                  pl.BlockSpec((1, ct), lambda j, b: (0, j)),
                  pl.BlockSpec((2 * p, 2 * p), lambda j, b: (0, 0)),
                  pl.BlockSpec((p, 2 * p), lambda j, b: (0, 0))],
        out_specs=pl.BlockSpec((1, l, ct), lambda j, b: (b, 0, j)),
        scratch_shapes=[pltpu.VMEM((2 * nb - 1, 2 * p, ct), F32),
                        pltpu.VMEM((nb, 2 * p, ct), F32),
                        pltpu.VMEM((nb, 2 * p, ct), F32)],
        compiler_params=_cparams(("arbitrary", "arbitrary")),
        name="hyena_conv",
    )(z, z, z, sc_w, sc_w, sc_w, scb, scb, scb, k_full, f_bias.reshape(1, d), fwd, inv)


def _swiglu_step(h_bf16, w1_ref, w3_ref, w2_ref, acc_ref):
    a = jnp.dot(h_bf16, w1_ref[0].astype(BF16), preferred_element_type=F32)
    b = jnp.dot(h_bf16, w3_ref[0].astype(BF16), preferred_element_type=F32)
    g = (_silu(a) * b).astype(BF16)
    acc_ref[...] += jnp.dot(g, w2_ref[0].astype(BF16), preferred_element_type=F32)


def _ffn_dense_kernel(x_ref, mod_ref, g_ref, w1_ref, w3_ref, w2_ref, o_ref, h_ref, acc_ref,
                      *, sh_row, sc_row, gate_row):
    f = pl.program_id(2)

    @pl.when(f == 0)
    def _():
        mod = mod_ref[0]
        h = _rms(x_ref[0], g_ref[...]) * (1.0 + mod[sc_row:sc_row + 1]) + mod[sh_row:sh_row + 1]
        h_ref[...] = h.astype(BF16)
        acc_ref[...] = jnp.zeros_like(acc_ref)

    _swiglu_step(h_ref[...], w1_ref, w3_ref, w2_ref, acc_ref)

    @pl.when(f == pl.num_programs(2) - 1)
    def _():
        o_ref[0] = x_ref[0] + mod_ref[0][gate_row:gate_row + 1] * acc_ref[...]


def _ffn_dense(x, mod, gnorm, w1, w3, w2, sh_row, sc_row, gate_row, tm_pref=1024, tf_pref=512):
    bx, l, d = x.shape
    ff = w1.shape[1]
    tm = _pick(l, tm_pref)
    tf = _pick(ff, tf_pref)
    per_b = mod.shape[0] > 1
    return pl.pallas_call(
        functools.partial(_ffn_dense_kernel, sh_row=sh_row, sc_row=sc_row, gate_row=gate_row),
        out_shape=jax.ShapeDtypeStruct((bx, l, d), F32),
        grid=(bx, l // tm, ff // tf),
        in_specs=[pl.BlockSpec((1, tm, d), lambda b, i, f: (b, i, 0)),
                  pl.BlockSpec((1, MOD_ROWS, d), (lambda b, i, f: (b, 0, 0)) if per_b else (lambda b, i, f: (0, 0, 0))),
                  pl.BlockSpec((1, d), lambda b, i, f: (0, 0)),
                  pl.BlockSpec((1, d, tf), lambda b, i, f: (0, 0, f)),
                  pl.BlockSpec((1, d, tf), lambda b, i, f: (0, 0, f)),
                  pl.BlockSpec((1, tf, d), lambda b, i, f: (0, f, 0))],
        out_specs=pl.BlockSpec((1, tm, d), lambda b, i, f: (b, i, 0)),
        scratch_shapes=[pltpu.VMEM((tm, d), BF16), pltpu.VMEM((tm, d), F32)],
        compiler_params=_cparams(("arbitrary", "arbitrary", "arbitrary")),
        name="ffn_dense",
    )(x, mod, gnorm.reshape(1, d), w1[None], w3[None], w2[None])


def _ffn_expert_kernel(te_ref, tv_ref, x_ref, w1_ref, w3_ref, w2_ref, o_ref, h_ref, acc_ref):
    i = pl.program_id(0)
    f = pl.program_id(1)

    @pl.when(tv_ref[i] > 0)
    def _():
        @pl.when(f == 0)
        def _():
            h_ref[...] = x_ref[...].astype(BF16)
            acc_ref[...] = jnp.zeros_like(acc_ref)

        _swiglu_step(h_ref[...], w1_ref, w3_ref, w2_ref, acc_ref)

        @pl.when(f == pl.num_programs(1) - 1)
        def _():
            o_ref[...] = acc_ref[...]

    @pl.when((tv_ref[i] == 0) & (f == 0))
    def _():
        o_ref[...] = jnp.zeros_like(o_ref)


def _ffn_experts(xs, tile_expert, tile_valid, w1, w3, w2, tm, tf_pref=512):
    np_rows, d = xs.shape
    ff = w1.shape[2]
    tf = _pick(ff, tf_pref)
    grid_spec = pltpu.PrefetchScalarGridSpec(
        num_scalar_prefetch=2,
        grid=(np_rows // tm, ff // tf),
        in_specs=[pl.BlockSpec((tm, d), lambda i, f, te, tv: (i, 0)),
                  pl.BlockSpec((1, d, tf), lambda i, f, te, tv: (te[i], 0, f)),
                  pl.BlockSpec((1, d, tf), lambda i, f, te, tv: (te[i], 0, f)),
                  pl.BlockSpec((1, tf, d), lambda i, f, te, tv: (te[i], f, 0))],
        out_specs=pl.BlockSpec((tm, d), lambda i, f, te, tv: (i, 0)),
        scratch_shapes=[pltpu.VMEM((tm, d), BF16), pltpu.VMEM((tm, d), F32)],
    )
    return pl.pallas_call(
        _ffn_expert_kernel,
        out_shape=jax.ShapeDtypeStruct((np_rows, d), F32),
        grid_spec=grid_spec,
        compiler_params=_cparams(("arbitrary", "arbitrary")),
        name="ffn_experts",
    )(tile_expert, tile_valid, xs, w1, w3, w2)


def _mla_proj_kernel(x_ref, mod_ref, g_ref, wa_ref, qg_ref, kvg_ref, wqb_ref, wk_ref, wv_ref, tab_ref,
                     q_ref, k_ref, v_ref, *, q_lora, kv_lora):
    mod = mod_ref[0, 0]
    h = _rms(x_ref[0], g_ref[...]) * (1.0 + mod[1:2]) + mod[0:1]
    a = jnp.dot(h.astype(BF16), wa_ref[...], preferred_element_type=F32)
    qn = _rms(a[:, :q_lora], qg_ref[...]).astype(BF16)
    cn = _rms(a[:, q_lora:q_lora + kv_lora], kvg_ref[...]).astype(BF16)
    kpe = a[:, q_lora + kv_lora:]
    tab = tab_ref[...]
    cq, s1q, s2q, ck, s1k, s2k = (tab[:, n * LANES:(n + 1) * LANES] for n in range(6))
    q = jnp.dot(qn, wqb_ref[...], preferred_element_type=F32)
    wq = q.shape[1]
    rep = wq // LANES
    q = (q * jnp.tile(cq, (1, rep)) + pltpu.roll(q, ROPE_AXIS, 1) * jnp.tile(s1q, (1, rep))
         + pltpu.roll(q, wq - ROPE_AXIS, 1) * jnp.tile(s2q, (1, rep)))
    q_ref[0] = q.astype(BF16)
    kr = kpe * ck + pltpu.roll(kpe, ROPE_AXIS, 1) * s1k + pltpu.roll(kpe, LANES - ROPE_AXIS, 1) * s2k
    k = jnp.dot(cn, wk_ref[...], preferred_element_type=F32) + jnp.tile(kr, (1, rep))
    k_ref[0] = k.astype(BF16)
    v_ref[0] = jnp.dot(cn, wv_ref[...], preferred_element_type=F32).astype(BF16)


def _rope_tables(l, n_ctx):
    rows = l // GRID_W
    row = jnp.broadcast_to(jnp.arange(rows, dtype=F32)[:, None], (rows, GRID_W)).reshape(l)
    col = jnp.broadcast_to(jnp.arange(GRID_W, dtype=F32)[None, :], (rows, GRID_W)).reshape(l)
    inv = ROPE_BASE ** (-jnp.arange(0, ROPE_AXIS, 2, dtype=F32) / ROPE_AXIS)
    ang = jnp.concatenate([row[:, None] * inv, col[:, None] * inv], axis=-1)
    cos = jnp.concatenate([jnp.ones((n_ctx, ROPE_AXIS), F32), jnp.cos(ang)], axis=0)
    sin = jnp.concatenate([jnp.zeros((n_ctx, ROPE_AXIS), F32), jnp.sin(ang)], axis=0)
    n = n_ctx + l
    ones = jnp.ones((n, QK_NOPE), F32)
    z16 = jnp.zeros((n, ROPE_AXIS), F32)
    z64 = jnp.zeros((n, QK_NOPE), F32)
    zpad = jnp.zeros((n, HEAD_SLOT - QK_HEAD), F32)
    c = jnp.concatenate([ones, cos, cos, zpad], axis=1)
    s1 = jnp.concatenate([z64, z16, sin, zpad], axis=1)
    s2 = jnp.concatenate([z64, -sin, z16, zpad], axis=1)
    scale = 1.0 / math.sqrt(QK_HEAD)
    return jnp.concatenate([c * scale, s1 * scale, s2 * scale, c, s1, s2], axis=1)


def _mla_weights(wq_a, wq_b, wkv_a, wkv_b):
    d, q_lora = wq_a.shape
    kv_lora = wkv_a.shape[1] - QK_ROPE
    ev = np.arange(0, QK_ROPE, 2)
    od = np.arange(1, QK_ROPE, 2)
    kpe = wkv_a[:, kv_lora:]
    kpe_slot = jnp.concatenate([jnp.zeros((d, QK_NOPE), F32), kpe[:, ev], kpe[:, od],
                                jnp.zeros((d, HEAD_SLOT - QK_HEAD), F32)], axis=1)
    wa = jnp.concatenate([wq_a, wkv_a[:, :kv_lora], kpe_slot], axis=1)
    qb = wq_b.reshape(q_lora, N_HEADS, QK_HEAD)
    qb = jnp.concatenate([qb[..., :QK_NOPE], qb[..., QK_NOPE + ev], qb[..., QK_NOPE + od],
                          jnp.zeros((q_lora, N_HEADS, HEAD_SLOT - QK_HEAD), F32)], axis=-1)
    kvb = wkv_b.reshape(kv_lora, N_HEADS, QK_NOPE + V_HEAD)
    wk = jnp.concatenate([kvb[..., :QK_NOPE], jnp.zeros((kv_lora, N_HEADS, HEAD_SLOT - QK_NOPE), F32)], axis=-1)
    wv = kvb[..., QK_NOPE:]
    return (wa.astype(BF16), qb.reshape(q_lora, N_HEADS * HEAD_SLOT).astype(BF16),
            wk.reshape(kv_lora, N_HEADS * HEAD_SLOT).astype(BF16),
            wv.reshape(kv_lora, N_HEADS * V_HEAD).astype(BF16))


def _mla_proj(tok, mod_pair, gnorm, wa, q_norm, kv_norm, wqb, wk, wv, tables, n_ctx):
    bx, n, d = tok.shape
    tm = _pick(n_ctx, 256)
    q_lora = q_norm.shape[0]
    kv_lora = kv_norm.shape[0]
    wq = wqb.shape[1]
    wvn = wv.shape[1]
    ctx_tiles = n_ctx // tm
    const = lambda b, i: (0, 0)
    return pl.pallas_call(
        functools.partial(_mla_proj_kernel, q_lora=q_lora, kv_lora=kv_lora),
        out_shape=(jax.ShapeDtypeStruct((bx, n, wq), BF16),
                   jax.ShapeDtypeStruct((bx, n, wq), BF16),
                   jax.ShapeDtypeStruct((bx, n, wvn), BF16)),
        grid=(bx, n // tm),
        in_specs=[pl.BlockSpec((1, tm, d), lambda b, i: (b, i, 0)),
                  pl.BlockSpec((1, 1, MOD_ROWS, d), lambda b, i: (b, jnp.where(i < ctx_tiles, 0, 1), 0, 0)),
                  pl.BlockSpec((1, d), const),
                  pl.BlockSpec(wa.shape, const),
                  pl.BlockSpec((1, q_lora), const),
                  pl.BlockSpec((1, kv_lora), const),
                  pl.BlockSpec(wqb.shape, const),
                  pl.BlockSpec(wk.shape, const),
                  pl.BlockSpec(wv.shape, const),
                  pl.BlockSpec((tm, 6 * LANES), lambda b, i: (i, 0))],
        out_specs=(pl.BlockSpec((1, tm, wq), lambda b, i: (b, i, 0)),
                   pl.BlockSpec((1, tm, wq), lambda b, i: (b, i, 0)),
                   pl.BlockSpec((1, tm, wvn), lambda b, i: (b, i, 0))),
        compiler_params=_cparams(("arbitrary", "arbitrary")),
        name="mla_proj",
    )(tok, mod_pair, gnorm.reshape(1, d), wa, q_norm.reshape(1, q_lora), kv_norm.reshape(1, kv_lora),
      wqb, wk, wv, tables)


def _attn_kernel(q_ref, k_ref, v_ref, o_ref):
    q2 = q_ref[0]
    k2 = k_ref[0]
    v2 = v_ref[0]
    outs = []
    for h in range(2):
        q = q2[:, h * HEAD_SLOT:(h + 1) * HEAD_SLOT]
        k = k2[:, h * HEAD_SLOT:(h + 1) * HEAD_SLOT]
        s = lax.dot_general(q, k, (((1,), (1,)), ((), ())), preferred_element_type=F32)
        m = jnp.max(s, axis=-1, keepdims=True)
        e = jnp.exp(s - m)
        den = jnp.sum(e, axis=-1, keepdims=True)
        outs.append(jnp.dot(e.astype(BF16), v2, preferred_element_type=F32) / den)
    lane = lax.broadcasted_iota(jnp.int32, outs[0].shape, 1)
    o_ref[0] = jnp.where(lane < V_HEAD, outs[0], outs[1]).astype(BF16)


def _attention(q, k, v, l, n_ctx, tq_pref=256):
    bx, n, _ = q.shape
    tq = _pick(n_ctx, tq_pref)
    off = n_ctx // tq
    hp = N_HEADS // 2
    return pl.pallas_call(
        _attn_kernel,
        out_shape=jax.ShapeDtypeStruct((bx, l, N_HEADS * V_HEAD), BF16),
        grid=(bx, hp, l // tq),
        in_specs=[pl.BlockSpec((1, tq, 2 * HEAD_SLOT), lambda b, h, i: (b, i + off, h)),
                  pl.BlockSpec((1, n, 2 * HEAD_SLOT), lambda b, h, i: (b, 0, h)),
                  pl.BlockSpec((1, n, 2 * V_HEAD), lambda b, h, i: (b, 0, h))],
        out_specs=pl.BlockSpec((1, tq, 2 * V_HEAD), lambda b, h, i: (b, i, h)),
        compiler_params=_cparams(("arbitrary", "arbitrary", "arbitrary")),
        name="mla_attention",
    )(q, k, v)


def _route_kernel(x_ref, mod_ref, g_ref, r_ref, tri_ref, h_ref, meta_ref, cnt_ref, carry_ref, *, n_exp):
    i = pl.program_id(0)

    @pl.when(i == 0)
    def _():
        carry_ref[...] = jnp.zeros_like(carry_ref)

    mod = mod_ref[0]
    h = _rms(x_ref[...], g_ref[...]) * (1.0 + mod[4:5]) + mod[3:4]
    h_ref[...] = h
    logits = jnp.dot(h, r_ref[...], precision=HIGHEST, preferred_element_type=F32)
    lane = lax.broadcasted_iota(jnp.int32, logits.shape, 1)
    lane_f = lane.astype(F32)
    neg = jnp.float32(-jnp.inf)
    lg = jnp.where(lane < n_exp, logits, neg)
    v1 = jnp.max(lg, axis=-1, keepdims=True)
    i1 = jnp.min(jnp.where(lg == v1, lane_f, float(LANES)), axis=-1, keepdims=True)
    oh1 = lane_f == i1
    lg2 = jnp.where(oh1, neg, lg)
    v2 = jnp.max(lg2, axis=-1, keepdims=True)
    i2 = jnp.min(jnp.where(lg2 == v2, lane_f, float(LANES)), axis=-1, keepdims=True)
    oh2 = lane_f == i2
    e = jnp.exp(v2 - v1)
    g1 = 1.0 / (1.0 + e)
    g2 = e / (1.0 + e)
    oh = jnp.where(oh1 | oh2, 1.0, 0.0)
    pref = jnp.dot(tri_ref[...], oh.astype(BF16), preferred_element_type=F32)
    excl = pref - oh + carry_ref[...]
    r1 = jnp.sum(jnp.where(oh1, excl, 0.0), axis=-1, keepdims=True)
    r2 = jnp.sum(jnp.where(oh2, excl, 0.0), axis=-1, keepdims=True)
    carry_ref[...] += jnp.sum(oh, axis=0, keepdims=True)
    cnt_ref[...] = carry_ref[...]
    meta = jnp.where(lane == 0, i1, 0.0)
    meta = jnp.where(lane == 1, i2, meta)
    meta = jnp.where(lane == 2, g1, meta)
    meta = jnp.where(lane == 3, g2, meta)
    meta = jnp.where(lane == 4, r1, meta)
    meta = jnp.where(lane == 5, r2, meta)
    meta_ref[...] = meta


def _route(x_flat, mod, gnorm, router, tokens_per_batch, tr_pref=512):
    n, d = x_flat.shape
    n_exp = router.shape[1]
    tr = _pick(tokens_per_batch, tr_pref)
    per_b = tokens_per_batch // tr
    rpad = jnp.pad(router, ((0, 0), (0, LANES - n_exp)))
    tri = jnp.asarray(np.tril(np.ones((tr, tr), np.float32))).astype(BF16)
    return pl.pallas_call(
        functools.partial(_route_kernel, n_exp=n_exp),
        out_shape=(jax.ShapeDtypeStruct((n, d), F32),
                   jax.ShapeDtypeStruct((n, LANES), F32),
                   jax.ShapeDtypeStruct((1, LANES), F32)),
        grid=(n // tr,),
        in_specs=[pl.BlockSpec((tr, d), lambda i: (i, 0)),
                  pl.BlockSpec((1, MOD_ROWS, d), lambda i: (i // per_b, 0, 0)),
                  pl.BlockSpec((1, d), lambda i: (0, 0)),
                  pl.BlockSpec((d, LANES), lambda i: (0, 0)),
                  pl.BlockSpec((tr, tr), lambda i: (0, 0))],
        out_specs=(pl.BlockSpec((tr, d), lambda i: (i, 0)),
                   pl.BlockSpec((tr, LANES), lambda i: (i, 0)),
                   pl.BlockSpec((1, LANES), lambda i: (0, 0))),
        scratch_shapes=[pltpu.VMEM((1, LANES), F32)],
        compiler_params=_cparams(("arbitrary",)),
        name="moe_route",
    )(x_flat, mod, gnorm.reshape(1, d), rpad, tri)


ROW_DMA_UNROLL = 8


def _wait_rows(any_ref, rows, sem):
    blk = any_ref.at[pl.ds(0, rows)]
    pltpu.make_async_copy(blk, blk, sem).wait()


def _dispatch_kernel(pos_ref, h_ref, xs_in_ref, xs_ref, sem, *, td):
    del xs_in_ref

    def issue(t, c):
        for k in range(TOP_K):
            pltpu.make_async_copy(h_ref.at[pl.ds(t, 1)], xs_ref.at[pl.ds(pos_ref[0, 0, k * td + t], 1)],
                                  sem).start(priority=k)
        return c

    lax.fori_loop(0, td, issue, 0, unroll=ROW_DMA_UNROLL)
    _wait_rows(xs_ref, TOP_K * td, sem)


def _dispatch(h, pos_tiles, np_rows, td):
    n, d = h.shape
    xs0 = jnp.zeros((np_rows, d), F32)
    return pl.pallas_call(
        functools.partial(_dispatch_kernel, td=td),
        out_shape=jax.ShapeDtypeStruct((np_rows, d), F32),
        grid=(n // td,),
        in_specs=[pl.BlockSpec((1, 1, TOP_K * td), lambda i: (i, 0, 0), memory_space=pltpu.SMEM),
                  pl.BlockSpec((td, d), lambda i: (i, 0)),
                  pl.BlockSpec(memory_space=pl.ANY)],
        out_specs=pl.BlockSpec(memory_space=pl.ANY),
        scratch_shapes=[pltpu.SemaphoreType.DMA(())],
        input_output_aliases={2: 0},
        compiler_params=_cparams(("arbitrary",)),
        name="moe_dispatch",
    )(pos_tiles, h, xs0)


def _combine_kernel(pos_ref, posn_ref, ys_ref, meta_ref, x_ref, mod_ref, g_ref, o_ref, buf_ref, sems, *, td):
    i = pl.program_id(0)
    n = pl.num_programs(0)
    slot = lax.rem(i, 2)

    def gather(p_ref, s):
        def issue(t, c):
            for k in range(TOP_K):
                pltpu.make_async_copy(ys_ref.at[pl.ds(p_ref[0, 0, k * td + t], 1)],
                                      buf_ref.at[s, k, pl.ds(t, 1)], sems.at[s]).start(priority=k)
            return c

        lax.fori_loop(0, td, issue, 0, unroll=ROW_DMA_UNROLL)

    @pl.when(i == 0)
    def _():
        gather(pos_ref, slot)

    @pl.when(i + 1 < n)
    def _():
        gather(posn_ref, 1 - slot)

    _wait_rows(ys_ref, TOP_K * td, sems.at[slot])
    meta = meta_ref[...]
    y = meta[:, 2:3] * buf_ref[slot, 0] + meta[:, 3:4] * buf_ref[slot, 1]
    x = x_ref[...] + mod_ref[0][5:6] * y
    o_ref[...] = _rms(x, g_ref[...])


def _combine(ys, pos_tiles, meta, x_flat, mod, norm_final, tokens_per_batch, td):
    n, d = x_flat.shape
    per_b = tokens_per_batch // td
    nt = n // td
    return pl.pallas_call(
        functools.partial(_combine_kernel, td=td),
        out_shape=jax.ShapeDtypeStruct((n, d), F32),
        grid=(nt,),
        in_specs=[pl.BlockSpec((1, 1, TOP_K * td), lambda i: (i, 0, 0), memory_space=pltpu.SMEM),
                  pl.BlockSpec((1, 1, TOP_K * td), lambda i: (jnp.minimum(i + 1, nt - 1), 0, 0),
                               memory_space=pltpu.SMEM),
                  pl.BlockSpec(memory_space=pl.ANY),
                  pl.BlockSpec((td, LANES), lambda i: (i, 0)),
                  pl.BlockSpec((td, d), lambda i: (i, 0)),
                  pl.BlockSpec((1, MOD_ROWS, d), lambda i: (i // per_b, 0, 0)),
                  pl.BlockSpec((1, d), lambda i: (0, 0))],
        out_specs=pl.BlockSpec((td, d), lambda i: (i, 0)),
        scratch_shapes=[pltpu.VMEM((2, TOP_K, td, d), F32), pltpu.SemaphoreType.DMA((2,))],
        compiler_params=_cparams(("arbitrary",)),
        name="moe_combine",
    )(pos_tiles, pos_tiles, ys, meta, x_flat, mod, norm_final.reshape(1, d))


def _moe(x, mod, gnorm, router, w1, w3, w2, norm_final, tm=1024, td=256):
    bx, l, d = x.shape
    n = bx * l
    n_exp = router.shape[1]
    x_flat = x.reshape(n, d)
    h, meta, counts = _route(x_flat, mod, gnorm, router, l)
    idx = meta[:, 0:TOP_K].astype(jnp.int32)
    rank = meta[:, 4:4 + TOP_K].astype(jnp.int32)
    cnt = counts[0, :n_exp].astype(jnp.int32)
    tiles_e = (cnt + tm - 1) // tm
    tile_end = jnp.cumsum(tiles_e)
    start_rows = (tile_end - tiles_e) * tm
    pos = start_rows[idx] + rank
    n_tiles = (n * TOP_K) // tm + n_exp
    np_rows = n_tiles * tm
    tile_ids = jnp.arange(n_tiles, dtype=jnp.int32)
    used = tile_end[-1]
    tile_expert = jnp.sum((jnp.minimum(tile_ids, used - 1)[:, None] >= tile_end[None, :]).astype(jnp.int32), axis=1)
    tile_expert = jnp.minimum(tile_expert, n_exp - 1)
    tile_valid = (tile_ids < used).astype(jnp.int32)
    pos_tiles = pos.reshape(n // td, td, TOP_K).transpose(0, 2, 1).reshape(n // td, 1, TOP_K * td)
    xs = _dispatch(h, pos_tiles, np_rows, td)
    ys = _ffn_experts(xs, tile_expert, tile_valid, w1, w3, w2, tm)
    out = _combine(ys, pos_tiles, meta, x_flat, mod, norm_final, l, td)
    return out.reshape(bx, l, d)


def _mod_rows(m):
    r, n = m.shape
    return jnp.pad(m.reshape(r, N_MOD, n // N_MOD), ((0, 0), (0, MOD_ROWS - N_MOD), (0, 0)))


def _hyena_layer(x, mod, gnorm, in_w, in_b, sc_w, sc_b, k_full, f_bias, out_w, out_b):
    z = _norm_mod_matmul(x, mod, gnorm, in_w, in_b, 0, 1)
    g = _hyena_conv(z, sc_w, sc_b, k_full, f_bias)
    return _matmul_gated_residual(g, out_w, out_b, x, mod, 2)


@jax.jit
def kernel(x, c, ctx, c_ctx, ada_w, ada_b, norm_mix, norm_ffn, hy_in_w, hy_in_b, hy_sc_w, hy_sc_b, hy_f_w0, hy_f_b0, hy_f_wi, hy_f_bi, hy_f_freq, hy_f_wout, hy_f_bias, hy_out_w, hy_out_b, mla_wq_a, mla_q_norm, mla_wq_b, mla_wkv_a, mla_kv_norm, mla_wkv_b, mla_wo, ffn_w1, ffn_w3, ffn_w2, moe_router, moe_w1, moe_w3, moe_w2, norm_final):
    bsz, l, d = x.shape
    n_ctx = ctx.shape[1]
    depth = ada_w.shape[0]
    assert depth == 2, "layer 0 = Hyena + dense SwiGLU, layer 1 = MLA + expert SwiGLU"

    rows = -(-(bsz + 1) // 8) * 8
    cvec = jnp.zeros((rows, d), F32).at[:bsz].set(c).at[bsz].set(c_ctx)
    mods = _ada_mod(cvec, ada_w, ada_b)
    modx = [_mod_rows(mods[i, :bsz]) for i in range(depth)]
    modc = [_mod_rows(mods[i, bsz:bsz + 1]) for i in range(depth)]

    in_w = hy_in_w[0].astype(BF16)
    out_w = hy_out_w[0].astype(BF16)
    fargs = (hy_f_w0[0], hy_f_b0[0], hy_f_wi[0], hy_f_bi[0], hy_f_freq[0], hy_f_wout[0])
    kx = _hyena_filter_full(l, d, *fargs)
    kc = _hyena_filter_full(n_ctx, d, *fargs)
    x = _hyena_layer(x, modx[0], norm_mix[0], in_w, hy_in_b[0], hy_sc_w[0], hy_sc_b[0], kx,
                     hy_f_bias[0], out_w, hy_out_b[0])
    ctx = _hyena_layer(ctx, modc[0], norm_mix[0], in_w, hy_in_b[0], hy_sc_w[0], hy_sc_b[0], kc,
                       hy_f_bias[0], out_w, hy_out_b[0])
    x = _ffn_dense(x, modx[0], norm_ffn[0], ffn_w1[0], ffn_w3[0], ffn_w2[0], 3, 4, 5)
    ctx = _ffn_dense(ctx.reshape(1, bsz * n_ctx, d), modc[0], norm_ffn[0], ffn_w1[0], ffn_w3[0], ffn_w2[0],
                     3, 4, 5).reshape(bsz, n_ctx, d)

    tok = jnp.concatenate([ctx, x], axis=1)
    mod_pair = jnp.stack([jnp.broadcast_to(modc[1], modx[1].shape), modx[1]], axis=1)
    wa, wqb, wk, wv = _mla_weights(mla_wq_a[0], mla_wq_b[0], mla_wkv_a[0], mla_wkv_b[0])
    tables = _rope_tables(l, n_ctx)
    q, k, v = _mla_proj(tok, mod_pair, norm_mix[1], wa, mla_q_norm[0], mla_kv_norm[0], wqb, wk, wv, tables, n_ctx)
    o = _attention(q, k, v, l, n_ctx)
    x = _matmul_gated_residual(o, mla_wo[0].astype(BF16), jnp.zeros((d,), F32), x, modx[1], 2)
    return _moe(x, modx[1], norm_ffn[1], moe_router[0], moe_w1[0], moe_w3[0], moe_w2[0], norm_final)
```

```python
import functools
import math

import jax
import jax.numpy as jnp
import numpy as np
from jax import lax
from jax.experimental import pallas as pl
from jax.experimental.pallas import tpu as pltpu

F32 = jnp.float32
BF16 = jnp.bfloat16
HIGHEST = lax.Precision.HIGHEST

RMS_EPS = 1e-6
N_MOD = 6
MOD_ROWS = 8
GRID_W = 64
SHORT_CONV = 3
FILTER_BANDS = 8
FILTER_EMB = 1 + 2 * FILTER_BANDS
FILTER_EMB_PAD = 32
DECAY_TARGET = 1e-2
FAST_DECAY_PCT = 0.3
SLOW_DECAY_PCT = 1.5
N_HEADS = 16
QK_NOPE = 64
QK_ROPE = 32
QK_HEAD = QK_NOPE + QK_ROPE
V_HEAD = 64
ROPE_AXIS = QK_ROPE // 2
ROPE_BASE = 10000.0
TOP_K = 2
LANES = 128
HEAD_SLOT = 128
VMEM_LIMIT = 56 * 1024 * 1024


def _cparams(sem, vmem=VMEM_LIMIT):
    return pltpu.CompilerParams(dimension_semantics=sem, vmem_limit_bytes=vmem)


def _rms(x, g):
    return x * lax.rsqrt(jnp.mean(x * x, axis=-1, keepdims=True) + RMS_EPS) * g


def _silu(x):
    return x * (1.0 / (1.0 + jnp.exp(-x)))


def _pick(total, pref):
    t = min(total, pref)
    while total % t:
        t //= 2
    return t


def _ada_kernel(c_ref, w_ref, b_ref, o_ref):
    c = c_ref[...]
    o_ref[0] = jnp.dot(_silu(c), w_ref[0], precision=HIGHEST, preferred_element_type=F32) + b_ref[0]


def _ada_mod(cvec, ada_w, ada_b):
    depth, d, n = ada_w.shape
    r = cvec.shape[0]
    tn = _pick(n, 1536)
    return pl.pallas_call(
        _ada_kernel,
        out_shape=jax.ShapeDtypeStruct((depth, r, n), F32),
        grid=(depth, n // tn),
        in_specs=[pl.BlockSpec((r, d), lambda i, j: (0, 0)),
                  pl.BlockSpec((1, d, tn), lambda i, j: (i, 0, j)),
                  pl.BlockSpec((1, 1, tn), lambda i, j: (i, 0, j))],
        out_specs=pl.BlockSpec((1, r, tn), lambda i, j: (i, 0, j)),
        compiler_params=_cparams(("arbitrary", "arbitrary")),
        name="ada_mod",
    )(cvec, ada_w, ada_b.reshape(depth, 1, n))


def _nmm_kernel(x_ref, mod_ref, g_ref, w_ref, b_ref, o_ref, *, sh_row, sc_row):
    x = x_ref[0]
    mod = mod_ref[0]
    h = _rms(x, g_ref[...]) * (1.0 + mod[sc_row:sc_row + 1]) + mod[sh_row:sh_row + 1]
    o_ref[0] = jnp.dot(h.astype(BF16), w_ref[...], preferred_element_type=F32) + b_ref[...]


def _norm_mod_matmul(x, mod, gnorm, w_bf16, bias, sh_row, sc_row, tm_pref=512):
    bx, l, d = x.shape
    n = w_bf16.shape[1]
    tm = _pick(l, tm_pref)
    per_b = mod.shape[0] > 1
    return pl.pallas_call(
        functools.partial(_nmm_kernel, sh_row=sh_row, sc_row=sc_row),
        out_shape=jax.ShapeDtypeStruct((bx, l, n), F32),
        grid=(bx, l // tm),
        in_specs=[pl.BlockSpec((1, tm, d), lambda b, i: (b, i, 0)),
                  pl.BlockSpec((1, MOD_ROWS, d), (lambda b, i: (b, 0, 0)) if per_b else (lambda b, i: (0, 0, 0))),
                  pl.BlockSpec((1, d), lambda b, i: (0, 0)),
                  pl.BlockSpec((d, n), lambda b, i: (0, 0)),
                  pl.BlockSpec((1, n), lambda b, i: (0, 0))],
        out_specs=pl.BlockSpec((1, tm, n), lambda b, i: (b, i, 0)),
        compiler_params=_cparams(("arbitrary", "arbitrary")),
        name="norm_mod_matmul",
    )(x, mod, gnorm.reshape(1, d), w_bf16, bias.reshape(1, n))


def _mm_res_kernel(a_ref, w_ref, b_ref, r_ref, mod_ref, o_ref, *, gate_row):
    y = jnp.dot(a_ref[0], w_ref[...], preferred_element_type=F32) + b_ref[...]
    o_ref[0] = r_ref[0] + mod_ref[0][gate_row:gate_row + 1] * y


def _matmul_gated_residual(a_bf16, w_bf16, bias, resid, mod, gate_row, tm_pref=512):
    bx, l, k = a_bf16.shape
    d = w_bf16.shape[1]
    tm = _pick(l, tm_pref)
    per_b = mod.shape[0] > 1
    return pl.pallas_call(
        functools.partial(_mm_res_kernel, gate_row=gate_row),
        out_shape=jax.ShapeDtypeStruct((bx, l, d), F32),
        grid=(bx, l // tm),
        in_specs=[pl.BlockSpec((1, tm, k), lambda b, i: (b, i, 0)),
                  pl.BlockSpec((k, d), lambda b, i: (0, 0)),
                  pl.BlockSpec((1, d), lambda b, i: (0, 0)),
                  pl.BlockSpec((1, tm, d), lambda b, i: (b, i, 0)),
                  pl.BlockSpec((1, MOD_ROWS, d), (lambda b, i: (b, 0, 0)) if per_b else (lambda b, i: (0, 0, 0)))],
        out_specs=pl.BlockSpec((1, tm, d), lambda b, i: (b, i, 0)),
        compiler_params=_cparams(("arbitrary", "arbitrary")),
        name="matmul_gated_residual",
    )(a_bf16, w_bf16, bias.reshape(1, d), resid, mod)


def _filter_kernel(z_ref, w0_ref, b0_ref, wi_ref, bi_ref, fr_ref, wt_ref, wb_ref, dl_ref, fwd_ref, o_ref, h_ref,
                   *, l, p):
    @pl.when(pl.program_id(0) == 0)
    def _():
        fr = fr_ref[...]
        h = jnp.sin(fr * (jnp.dot(z_ref[...], w0_ref[...], precision=HIGHEST, preferred_element_type=F32)
                          + b0_ref[...]))
        for n in range(wi_ref.shape[0]):
            h = jnp.sin(fr * (jnp.dot(h, wi_ref[n], precision=HIGHEST, preferred_element_type=F32) + bi_ref[n]))
        h_ref[...] = h

    top = jnp.dot(h_ref[:l], wt_ref[...], precision=HIGHEST, preferred_element_type=F32)
    bot = jnp.dot(h_ref[l:], wb_ref[...], precision=HIGHEST, preferred_element_type=F32)
    t = z_ref[:, 0:1]
    rows = lax.broadcasted_iota(jnp.int32, (2 * l, 1), 0)
    decay = jnp.where(rows == l, 0.0, jnp.exp(-t * dl_ref[...]))
    k = jnp.concatenate([top, bot], axis=0) * decay
    k = k / jnp.sum(jnp.abs(k), axis=0, keepdims=True)
    nb = l // p
    for di in range(2 * nb - 1):
        start = (p * (di - nb)) % (2 * l)
        if start + 2 * p <= 2 * l:
            seg = k[start:start + 2 * p]
        else:
            seg = jnp.concatenate([k[start:], k[:start + 2 * p - 2 * l]], axis=0)
        o_ref[di] = jnp.dot(fwd_ref[...], seg.astype(BF16), preferred_element_type=F32)


def _conv_block(l, p_pref=512):
    return min(p_pref, l)


def _hyena_filter_spectra(l, d, w0, b0, wi, bi, freq, wout):
    p = _conv_block(l)
    nd = 2 * (l // p) - 1
    pos = jnp.arange(l, dtype=F32)
    t = (pos / max(l - 1, 1))[:, None]
    w = 2.0 * math.pi * pos / l
    f = jnp.linspace(1e-4, FILTER_BANDS - 1, FILTER_BANDS, dtype=F32)
    ang = w[:, None] * f[None, :]
    z = jnp.concatenate([t, jnp.cos(ang), -jnp.sin(ang)], axis=-1)
    deltas = jnp.abs(jnp.linspace(math.log(DECAY_TARGET) / SLOW_DECAY_PCT,
                                  math.log(DECAY_TARGET) / FAST_DECAY_PCT, d, dtype=F32))
    idx = np.concatenate([np.arange(l), [0], np.arange(l - 1, 0, -1)])
    zc = jnp.pad(z[idx], ((0, 0), (0, FILTER_EMB_PAD - FILTER_EMB)))
    w0p = jnp.pad(w0, ((0, FILTER_EMB_PAD - FILTER_EMB), (0, 0)))
    hid = w0.shape[1]
    n_in = wi.shape[0]
    ct = _pick(d, 256)
    nct = d // ct
    fwd = jnp.asarray(_dft_mats(p)[0]).astype(BF16)
    return pl.pallas_call(
        functools.partial(_filter_kernel, l=l, p=p),
        out_shape=jax.ShapeDtypeStruct((nd, 2 * p, d), F32),
        grid=(nct,),
        in_specs=[pl.BlockSpec((2 * l, FILTER_EMB_PAD), lambda j: (0, 0)),
                  pl.BlockSpec((FILTER_EMB_PAD, hid), lambda j: (0, 0)),
                  pl.BlockSpec((1, hid), lambda j: (0, 0)),
                  pl.BlockSpec((n_in, hid, hid), lambda j: (0, 0, 0)),
                  pl.BlockSpec((n_in, 1, hid), lambda j: (0, 0, 0)),
                  pl.BlockSpec((1, hid), lambda j: (0, 0)),
                  pl.BlockSpec((hid, ct), lambda j: (0, j)),
                  pl.BlockSpec((hid, ct), lambda j: (0, nct + j)),
                  pl.BlockSpec((1, ct), lambda j: (0, j)),
                  pl.BlockSpec((2 * p, 2 * p), lambda j: (0, 0))],
        out_specs=pl.BlockSpec((nd, 2 * p, ct), lambda j: (0, 0, j)),
        scratch_shapes=[pltpu.VMEM((2 * l, hid), F32)],
        compiler_params=_cparams(("arbitrary",)),
        name="hyena_filter",
    )(zc, w0p, b0.reshape(1, hid), wi, bi.reshape(n_in, 1, hid), freq.reshape(1, hid), wout, wout,
      deltas.reshape(1, d), fwd)


@functools.lru_cache(maxsize=None)
def _dft_mats(p):
    n = 2 * p
    f = np.arange(p)[:, None]
    t = np.arange(n)[None, :]
    ang = 2.0 * np.pi * (((2 * f + 1) * t) % (4 * p)) / (4 * p)
    fwd = np.concatenate([np.cos(ang), -np.sin(ang)], axis=0)
    q = np.arange(p)[:, None]
    ff = np.arange(p)[None, :]
    ang2 = 2.0 * np.pi * (((2 * ff + 1) * (q + p)) % (4 * p)) / (4 * p)
    inv = np.concatenate([np.cos(ang2), -np.sin(ang2)], axis=1) / p
    return fwd.astype(np.float32), inv.astype(np.float32)


def _hyena_conv_kernel(z0_ref, z1_ref, z2_ref, w0_ref, w1_ref, w2_ref, b0_ref, b1_ref, b2_ref,
                       ks_ref, fb_ref, fwd_ref, inv_ref, o_ref, vs_ref, ys_ref, *, l, p, rc):
    nb = l // p
    rows = lax.broadcasted_iota(jnp.int32, (l, z0_ref.shape[2]), 0)

    def sconv(z_ref, w_ref, b_ref):
        z = z_ref[0]
        w = w_ref[...]
        zm = jnp.where(rows == 0, 0.0, pltpu.roll(z, 1, 0))
        zp = jnp.where(rows == l - 1, 0.0, pltpu.roll(z, l - 1, 0))
        return zm * w[0:1] + z * w[1:2] + zp * w[2:3] + b_ref[...]

    u = sconv(z2_ref, w2_ref, b2_ref) * sconv(z1_ref, w1_ref, b1_ref)
    ub = u.astype(BF16)
    for j in range(nb):
        vs_ref[j] = jnp.dot(fwd_ref[...], ub[j * p:(j + 1) * p], preferred_element_type=F32)

    def chunk(c, carry):
        r0 = pl.multiple_of(c * rc, rc)
        re = pl.ds(r0, rc)
        im = pl.ds(p + r0, rc)
        for i in range(nb):
            yr = None
            yi = None
            for j in range(nb):
                di = i - j + nb - 1
                kr = ks_ref[di, re, :]
                ki = ks_ref[di, im, :]
                vr = vs_ref[j, re, :]
                vi = vs_ref[j, im, :]
                tr = kr * vr - ki * vi
                ti = kr * vi + ki * vr
                yr = tr if yr is None else yr + tr
                yi = ti if yi is None else yi + ti
            ys_ref[i, re, :] = yr
            ys_ref[i, im, :] = yi
        return carry

    lax.fori_loop(0, p // rc, chunk, 0)

    x0 = sconv(z0_ref, w0_ref, b0_ref)
    fb = fb_ref[...]
    for i in range(nb):
        y = jnp.dot(inv_ref[...], ys_ref[i].astype(BF16), preferred_element_type=F32)
        sl = slice(i * p, (i + 1) * p)
        o_ref[0, sl, :] = (x0[sl] * (y + u[sl] * fb)).astype(BF16)


def _hyena_conv(z, sc_w, sc_b, spectra, f_bias, ct_pref=256):
    bx, l, d3 = z.shape
    d = d3 // 3
    nd, p2, _ = spectra.shape
    p = p2 // 2
    nb = l // p
    ct = _pick(d, ct_pref)
    nct = d // ct
    fwd, inv = _dft_mats(p)
    fwd = jnp.asarray(fwd[:, :p]).astype(BF16)
    inv = jnp.asarray(inv).astype(BF16)
    zspec = lambda part: pl.BlockSpec((1, l, ct), lambda j, b, part=part: (b, 0, part * nct + j))
    wspec = lambda part: pl.BlockSpec((SHORT_CONV, ct), lambda j, b, part=part: (0, part * nct + j))
    bspec = lambda part: pl.BlockSpec((1, ct), lambda j, b, part=part: (0, part * nct + j))
    once = pl.Buffered(1)
    scb = sc_b.reshape(1, d3)
    return pl.pallas_call(
        functools.partial(_hyena_conv_kernel, l=l, p=p, rc=8),
        out_shape=jax.ShapeDtypeStruct((bx, l, d), BF16),
        grid=(nct, bx),
        in_specs=[zspec(0), zspec(1), zspec(2), wspec(0), wspec(1), wspec(2), bspec(0), bspec(1), bspec(2),
                  pl.BlockSpec((nd, 2 * p, ct), lambda j, b: (0, 0, j), pipeline_mode=once),
                  pl.BlockSpec((1, ct), lambda j, b: (0, j)),
                  pl.BlockSpec((2 * p, p), lambda j, b: (0, 0), pipeline_mode=once),
                  pl.BlockSpec((p, 2 * p), lambda j, b: (0, 0), pipeline_mode=once)],
        out_specs=pl.BlockSpec((1, l, ct), lambda j, b: (b, 0, j)),
        scratch_shapes=[pltpu.VMEM((nb, 2 * p, ct), F32),
                        pltpu.VMEM((nb, 2 * p, ct), F32)],
        compiler_params=_cparams(("arbitrary", "arbitrary")),
        name="hyena_conv",
    )(z, z, z, sc_w, sc_w, sc_w, scb, scb, scb, spectra, f_bias.reshape(1, d), fwd, inv)


def _swiglu_step(h_bf16, w1_ref, w3_ref, w2_ref, acc_ref):
    a = jnp.dot(h_bf16, w1_ref[0].astype(BF16), preferred_element_type=F32)
    b = jnp.dot(h_bf16, w3_ref[0].astype(BF16), preferred_element_type=F32)
    g = (_silu(a) * b).astype(BF16)
    acc_ref[...] += jnp.dot(g, w2_ref[0].astype(BF16), preferred_element_type=F32)


def _ffn_dense_kernel(x_ref, mod_ref, g_ref, w1_ref, w3_ref, w2_ref, o_ref, h_ref, acc_ref,
                      *, sh_row, sc_row, gate_row):
    f = pl.program_id(2)

    @pl.when(f == 0)
    def _():
        mod = mod_ref[0]
        h = _rms(x_ref[0], g_ref[...]) * (1.0 + mod[sc_row:sc_row + 1]) + mod[sh_row:sh_row + 1]
        h_ref[...] = h.astype(BF16)
        acc_ref[...] = jnp.zeros_like(acc_ref)

    _swiglu_step(h_ref[...], w1_ref, w3_ref, w2_ref, acc_ref)

    @pl.when(f == pl.num_programs(2) - 1)
    def _():
        o_ref[0] = x_ref[0] + mod_ref[0][gate_row:gate_row + 1] * acc_ref[...]


def _ffn_dense(x, mod, gnorm, w1, w3, w2, sh_row, sc_row, gate_row, tm_pref=1024, tf_pref=512):
    bx, l, d = x.shape
    ff = w1.shape[1]
    tm = _pick(l, tm_pref)
    tf = _pick(ff, tf_pref)
    per_b = mod.shape[0] > 1
    return pl.pallas_call(
        functools.partial(_ffn_dense_kernel, sh_row=sh_row, sc_row=sc_row, gate_row=gate_row),
        out_shape=jax.ShapeDtypeStruct((bx, l, d), F32),
        grid=(bx, l // tm, ff // tf),
        in_specs=[pl.BlockSpec((1, tm, d), lambda b, i, f: (b, i, 0)),
                  pl.BlockSpec((1, MOD_ROWS, d), (lambda b, i, f: (b, 0, 0)) if per_b else (lambda b, i, f: (0, 0, 0))),
                  pl.BlockSpec((1, d), lambda b, i, f: (0, 0)),
                  pl.BlockSpec((1, d, tf), lambda b, i, f: (0, 0, f)),
                  pl.BlockSpec((1, d, tf), lambda b, i, f: (0, 0, f)),
                  pl.BlockSpec((1, tf, d), lambda b, i, f: (0, f, 0))],
        out_specs=pl.BlockSpec((1, tm, d), lambda b, i, f: (b, i, 0)),
        scratch_shapes=[pltpu.VMEM((tm, d), BF16), pltpu.VMEM((tm, d), F32)],
        compiler_params=_cparams(("arbitrary", "arbitrary", "arbitrary")),
        name="ffn_dense",
    )(x, mod, gnorm.reshape(1, d), w1[None], w3[None], w2[None])


def _ffn_expert_kernel(te_ref, tv_ref, x_ref, w1_ref, w3_ref, w2_ref, o_ref, h_ref, acc_ref):
    i = pl.program_id(0)
    f = pl.program_id(1)

    @pl.when(tv_ref[i] > 0)
    def _():
        @pl.when(f == 0)
        def _():
            h_ref[...] = x_ref[...].astype(BF16)
            acc_ref[...] = jnp.zeros_like(acc_ref)

        _swiglu_step(h_ref[...], w1_ref, w3_ref, w2_ref, acc_ref)

        @pl.when(f == pl.num_programs(1) - 1)
        def _():
            o_ref[...] = acc_ref[...]

    @pl.when((tv_ref[i] == 0) & (f == 0))
    def _():
        o_ref[...] = jnp.zeros_like(o_ref)


def _ffn_experts(xs, tile_expert, tile_valid, w1, w3, w2, tm, tf_pref=512):
    np_rows, d = xs.shape
    ff = w1.shape[2]
    tf = _pick(ff, tf_pref)
    grid_spec = pltpu.PrefetchScalarGridSpec(
        num_scalar_prefetch=2,
        grid=(np_rows // tm, ff // tf),
        in_specs=[pl.BlockSpec((tm, d), lambda i, f, te, tv: (i, 0)),
                  pl.BlockSpec((1, d, tf), lambda i, f, te, tv: (te[i], 0, f)),
                  pl.BlockSpec((1, d, tf), lambda i, f, te, tv: (te[i], 0, f)),
                  pl.BlockSpec((1, tf, d), lambda i, f, te, tv: (te[i], f, 0))],
        out_specs=pl.BlockSpec((tm, d), lambda i, f, te, tv: (i, 0)),
        scratch_shapes=[pltpu.VMEM((tm, d), BF16), pltpu.VMEM((tm, d), F32)],
    )
    return pl.pallas_call(
        _ffn_expert_kernel,
        out_shape=jax.ShapeDtypeStruct((np_rows, d), F32),
        grid_spec=grid_spec,
        compiler_params=_cparams(("arbitrary", "arbitrary")),
        name="ffn_experts",
    )(tile_expert, tile_valid, xs, w1, w3, w2)


def _mla_latent_kernel(x_ref, mod_ref, g_ref, wa_ref, qg_ref, kvg_ref, wqb_ref, wk_ref, wv_ref, tab_ref,
                       q_ref, k_ref, v_ref, *, q_lora, kv_lora):
    mod = mod_ref[0]
    h = _rms(x_ref[0], g_ref[...]) * (1.0 + mod[1:2]) + mod[0:1]
    a = jnp.dot(h.astype(BF16), wa_ref[...], preferred_element_type=F32)
    qn = _rms(a[:, :q_lora], qg_ref[...]).astype(BF16)
    cn = _rms(a[:, q_lora:q_lora + kv_lora], kvg_ref[...]).astype(BF16)
    kpe = a[:, q_lora + kv_lora:]
    tab = tab_ref[...]
    cq, s1q, s2q, ck, s1k, s2k = (tab[:, n * LANES:(n + 1) * LANES] for n in range(6))
    q = jnp.dot(qn, wqb_ref[...], preferred_element_type=F32)
    wq = q.shape[1]
    rep = wq // LANES
    q = (q * jnp.tile(cq, (1, rep)) + pltpu.roll(q, ROPE_AXIS, 1) * jnp.tile(s1q, (1, rep))
         + pltpu.roll(q, wq - ROPE_AXIS, 1) * jnp.tile(s2q, (1, rep)))
    q_ref[0] = q.astype(BF16)
    kr = kpe * ck + pltpu.roll(kpe, ROPE_AXIS, 1) * s1k + pltpu.roll(kpe, LANES - ROPE_AXIS, 1) * s2k
    k = jnp.dot(cn, wk_ref[...], preferred_element_type=F32) + jnp.tile(kr, (1, rep))
    k_ref[0] = k.astype(BF16)
    v_ref[0] = jnp.dot(cn, wv_ref[...], preferred_element_type=F32).astype(BF16)


def _mla_context_kernel(x_ref, mod_ref, g_ref, wa_ref, kvg_ref, wk_ref, wv_ref, k_ref, v_ref, *, kv_lora):
    mod = mod_ref[0]
    h = _rms(x_ref[0], g_ref[...]) * (1.0 + mod[1:2]) + mod[0:1]
    a = jnp.dot(h.astype(BF16), wa_ref[...], preferred_element_type=F32)
    cn = _rms(a[:, :kv_lora], kvg_ref[...]).astype(BF16)
    kpe = a[:, kv_lora:]
    rep = wk_ref.shape[1] // LANES
    k = jnp.dot(cn, wk_ref[...], preferred_element_type=F32) + jnp.tile(kpe, (1, rep))
    k_ref[0] = k.astype(BF16)
    v_ref[0] = jnp.dot(cn, wv_ref[...], preferred_element_type=F32).astype(BF16)


def _rope_tables(l):
    rows = l // GRID_W
    row = jnp.broadcast_to(jnp.arange(rows, dtype=F32)[:, None], (rows, GRID_W)).reshape(l)
    col = jnp.broadcast_to(jnp.arange(GRID_W, dtype=F32)[None, :], (rows, GRID_W)).reshape(l)
    inv = ROPE_BASE ** (-jnp.arange(0, ROPE_AXIS, 2, dtype=F32) / ROPE_AXIS)
    ang = jnp.concatenate([row[:, None] * inv, col[:, None] * inv], axis=-1)
    cos = jnp.cos(ang)
    sin = jnp.sin(ang)
    n = l
    ones = jnp.ones((n, QK_NOPE), F32)
    z16 = jnp.zeros((n, ROPE_AXIS), F32)
    z64 = jnp.zeros((n, QK_NOPE), F32)
    zpad = jnp.zeros((n, HEAD_SLOT - QK_HEAD), F32)
    c = jnp.concatenate([ones, cos, cos, zpad], axis=1)
    s1 = jnp.concatenate([z64, z16, sin, zpad], axis=1)
    s2 = jnp.concatenate([z64, -sin, z16, zpad], axis=1)
    scale = math.log2(math.e) / math.sqrt(QK_HEAD)
    return jnp.concatenate([c * scale, s1 * scale, s2 * scale, c, s1, s2], axis=1)


def _mla_weights(wq_a, wq_b, wkv_a, wkv_b):
    d, q_lora = wq_a.shape
    kv_lora = wkv_a.shape[1] - QK_ROPE
    ev = np.arange(0, QK_ROPE, 2)
    od = np.arange(1, QK_ROPE, 2)
    kpe = wkv_a[:, kv_lora:]
    kpe_slot = jnp.concatenate([jnp.zeros((d, QK_NOPE), F32), kpe[:, ev], kpe[:, od],
                                jnp.zeros((d, HEAD_SLOT - QK_HEAD), F32)], axis=1)
    wa = jnp.concatenate([wq_a, wkv_a[:, :kv_lora], kpe_slot], axis=1)
    qb = wq_b.reshape(q_lora, N_HEADS, QK_HEAD)
    qb = jnp.concatenate([qb[..., :QK_NOPE], qb[..., QK_NOPE + ev], qb[..., QK_NOPE + od],
                          jnp.zeros((q_lora, N_HEADS, HEAD_SLOT - QK_HEAD), F32)], axis=-1)
    kvb = wkv_b.reshape(kv_lora, N_HEADS, QK_NOPE + V_HEAD)
    wk = jnp.concatenate([kvb[..., :QK_NOPE], jnp.zeros((kv_lora, N_HEADS, HEAD_SLOT - QK_NOPE), F32)], axis=-1)
    wv = kvb[..., QK_NOPE:]
    return (wa.astype(BF16), qb.reshape(q_lora, N_HEADS * HEAD_SLOT).astype(BF16),
            wk.reshape(kv_lora, N_HEADS * HEAD_SLOT).astype(BF16),
            wv.reshape(kv_lora, N_HEADS * V_HEAD).astype(BF16))


def _mod_spec(mod):
    per_b = mod.shape[0] > 1
    return pl.BlockSpec((1, MOD_ROWS, mod.shape[2]), (lambda b, i: (b, 0, 0)) if per_b else (lambda b, i: (0, 0, 0)))


def _mla_latent(x, mod, gnorm, wa, q_norm, kv_norm, wqb, wk, wv, tables, tm_pref=512):
    bx, n, d = x.shape
    tm = _pick(n, tm_pref)
    q_lora = q_norm.shape[0]
    kv_lora = kv_norm.shape[0]
    wq = wqb.shape[1]
    wvn = wv.shape[1]
    const = lambda b, i: (0, 0)
    row = lambda width: pl.BlockSpec((1, tm, width), lambda b, i: (b, i, 0))
    return pl.pallas_call(
        functools.partial(_mla_latent_kernel, q_lora=q_lora, kv_lora=kv_lora),
        out_shape=(jax.ShapeDtypeStruct((bx, n, wq), BF16),
                   jax.ShapeDtypeStruct((bx, n, wq), BF16),
                   jax.ShapeDtypeStruct((bx, n, wvn), BF16)),
        grid=(bx, n // tm),
        in_specs=[row(d), _mod_spec(mod),
                  pl.BlockSpec((1, d), const),
                  pl.BlockSpec(wa.shape, const),
                  pl.BlockSpec((1, q_lora), const),
                  pl.BlockSpec((1, kv_lora), const),
                  pl.BlockSpec(wqb.shape, const),
                  pl.BlockSpec(wk.shape, const),
                  pl.BlockSpec(wv.shape, const),
                  pl.BlockSpec((tm, 6 * LANES), lambda b, i: (i, 0))],
        out_specs=(row(wq), row(wq), row(wvn)),
        compiler_params=_cparams(("arbitrary", "arbitrary")),
        name="mla_latent_proj",
    )(x, mod, gnorm.reshape(1, d), wa, q_norm.reshape(1, q_lora), kv_norm.reshape(1, kv_lora),
      wqb, wk, wv, tables)


def _mla_context(ctx, mod, gnorm, wa_kv, kv_norm, wk, wv, tm_pref=512):
    bx, n, d = ctx.shape
    tm = _pick(n, tm_pref)
    kv_lora = kv_norm.shape[0]
    wkn = wk.shape[1]
    wvn = wv.shape[1]
    const = lambda b, i: (0, 0)
    row = lambda width: pl.BlockSpec((1, tm, width), lambda b, i: (b, i, 0))
    return pl.pallas_call(
        functools.partial(_mla_context_kernel, kv_lora=kv_lora),
        out_shape=(jax.ShapeDtypeStruct((bx, n, wkn), BF16),
                   jax.ShapeDtypeStruct((bx, n, wvn), BF16)),
        grid=(bx, n // tm),
        in_specs=[row(d), _mod_spec(mod),
                  pl.BlockSpec((1, d), const),
                  pl.BlockSpec(wa_kv.shape, const),
                  pl.BlockSpec((1, kv_lora), const),
                  pl.BlockSpec(wk.shape, const),
                  pl.BlockSpec(wv.shape, const)],
        out_specs=(row(wkn), row(wvn)),
        compiler_params=_cparams(("arbitrary", "arbitrary")),
        name="mla_context_proj",
    )(ctx, mod, gnorm.reshape(1, d), wa_kv, kv_norm.reshape(1, kv_lora), wk, wv)


ATTN_HEADS_PER_STEP = 8
ATTN_KEY_CHUNK = 256
ATTN_SOFTMAX_LAG = 2
ATTN_PV_LAG = 6
SUM_ROWS = 16


def _attn_kernel(qt_ref, kc_ref, kx_ref, vct_ref, vxt_ref, o_ref, *, heads, ck):
    chunks = ([(kc_ref, vct_ref, c) for c in range(kc_ref.shape[1] // ck)]
              + [(kx_ref, vxt_ref, c) for c in range(kx_ref.shape[1] // ck)])
    items = [(h, ci) for ci in range(len(chunks)) for h in range(heads)]
    state = {h: None for h in range(heads)}
    scores = {}
    probs = {}
    ones = jnp.ones((SUM_ROWS, ck), BF16)

    def qk(t):
        h, ci = items[t]
        kref, _, c = chunks[ci]
        hsl = slice(h * HEAD_SLOT, (h + 1) * HEAD_SLOT)
        scores[t] = jnp.dot(kref[0, c * ck:(c + 1) * ck, hsl], qt_ref[0, hsl, :], preferred_element_type=F32)

    def softmax(t):
        h, _ = items[t]
        s = scores.pop(t)
        mc = jnp.max(s, axis=0, keepdims=True)
        if state[h] is None:
            probs[t] = (jnp.exp2(s - mc).astype(BF16), None)
            state[h] = (mc, None)
        else:
            m, acc = state[h]
            m_new = jnp.maximum(m, mc)
            probs[t] = (jnp.exp2(s - m_new).astype(BF16), jnp.exp2(m - m_new))
            state[h] = (m_new, acc)

    def pv(t):
        h, ci = items[t]
        _, vref, c = chunks[ci]
        p, alpha = probs.pop(t)
        lhs = jnp.concatenate([vref[0, h * V_HEAD:(h + 1) * V_HEAD, c * ck:(c + 1) * ck], ones], axis=0)
        o = jnp.dot(lhs, p, preferred_element_type=F32)
        m, acc = state[h]
        state[h] = (m, o if acc is None else alpha * acc + o)

    n_items = len(items)
    for t in range(n_items + ATTN_PV_LAG):
        if t < n_items:
            qk(t)
        if ATTN_SOFTMAX_LAG <= t < n_items + ATTN_SOFTMAX_LAG:
            softmax(t - ATTN_SOFTMAX_LAG)
        if t >= ATTN_PV_LAG:
            pv(t - ATTN_PV_LAG)

    for hp in range(heads // 2):
        outs = [state[h][1][:V_HEAD] / state[h][1][V_HEAD:V_HEAD + 1] for h in (2 * hp, 2 * hp + 1)]
        pair = jnp.concatenate(outs, axis=0)
        o_ref[0, :, hp * 2 * V_HEAD:(hp + 1) * 2 * V_HEAD] = pair.T.astype(BF16)


def _attention(qt, kc, kx, vct, vxt, tq_pref=256, heads=ATTN_HEADS_PER_STEP):
    bx, _, l = qt.shape
    n_ctx = kc.shape[1]
    tq = _pick(l, tq_pref)
    ck = _pick(n_ctx, ATTN_KEY_CHUNK)
    qk_w = heads * HEAD_SLOT
    v_w = heads * V_HEAD
    return pl.pallas_call(
        functools.partial(_attn_kernel, heads=heads, ck=ck),
        out_shape=jax.ShapeDtypeStruct((bx, l, N_HEADS * V_HEAD), BF16),
        grid=(bx, N_HEADS // heads, l // tq),
        in_specs=[pl.BlockSpec((1, qk_w, tq), lambda b, g, i: (b, g, i)),
                  pl.BlockSpec((1, n_ctx, qk_w), lambda b, g, i: (b, 0, g)),
                  pl.BlockSpec((1, l, qk_w), lambda b, g, i: (b, 0, g)),
                  pl.BlockSpec((1, v_w, n_ctx), lambda b, g, i: (b, g, 0)),
                  pl.BlockSpec((1, v_w, l), lambda b, g, i: (b, g, 0))],
        out_specs=pl.BlockSpec((1, tq, v_w), lambda b, g, i: (b, i, g)),
        compiler_params=_cparams(("arbitrary", "arbitrary", "arbitrary")),
        name="mla_attention",
    )(qt, kc, kx, vct, vxt)


def _route_kernel(x_ref, mod_ref, g_ref, r_ref, tri_ref, h_ref, meta_ref, cnt_ref, carry_ref, *, n_exp):
    i = pl.program_id(0)

    @pl.when(i == 0)
    def _():
        carry_ref[...] = jnp.zeros_like(carry_ref)

    mod = mod_ref[0]
    h = _rms(x_ref[...], g_ref[...]) * (1.0 + mod[4:5]) + mod[3:4]
    h_ref[...] = h
    logits = jnp.dot(h, r_ref[...], precision=HIGHEST, preferred_element_type=F32)
    lane = lax.broadcasted_iota(jnp.int32, logits.shape, 1)
    lane_f = lane.astype(F32)
    neg = jnp.float32(-jnp.inf)
    lg = jnp.where(lane < n_exp, logits, neg)
    v1 = jnp.max(lg, axis=-1, keepdims=True)
    i1 = jnp.min(jnp.where(lg == v1, lane_f, float(LANES)), axis=-1, keepdims=True)
    oh1 = lane_f == i1
    lg2 = jnp.where(oh1, neg, lg)
    v2 = jnp.max(lg2, axis=-1, keepdims=True)
    i2 = jnp.min(jnp.where(lg2 == v2, lane_f, float(LANES)), axis=-1, keepdims=True)
    oh2 = lane_f == i2
    e = jnp.exp(v2 - v1)
    g1 = 1.0 / (1.0 + e)
    g2 = e / (1.0 + e)
    oh = jnp.where(oh1 | oh2, 1.0, 0.0)
    pref = jnp.dot(tri_ref[...], oh.astype(BF16), preferred_element_type=F32)
    excl = pref - oh + carry_ref[...]
    r1 = jnp.sum(jnp.where(oh1, excl, 0.0), axis=-1, keepdims=True)
    r2 = jnp.sum(jnp.where(oh2, excl, 0.0), axis=-1, keepdims=True)
    carry_ref[...] += jnp.sum(oh, axis=0, keepdims=True)
    cnt_ref[...] = carry_ref[...]
    meta = jnp.where(lane == 0, i1, 0.0)
    meta = jnp.where(lane == 1, i2, meta)
    meta = jnp.where(lane == 2, g1, meta)
    meta = jnp.where(lane == 3, g2, meta)
    meta = jnp.where(lane == 4, r1, meta)
    meta = jnp.where(lane == 5, r2, meta)
    meta_ref[...] = meta


def _route(x_flat, mod, gnorm, router, tokens_per_batch, tr_pref=512):
    n, d = x_flat.shape
    n_exp = router.shape[1]
    tr = _pick(tokens_per_batch, tr_pref)
    per_b = tokens_per_batch // tr
    rpad = jnp.pad(router, ((0, 0), (0, LANES - n_exp)))
    tri = jnp.asarray(np.tril(np.ones((tr, tr), np.float32))).astype(BF16)
    return pl.pallas_call(
        functools.partial(_route_kernel, n_exp=n_exp),
        out_shape=(jax.ShapeDtypeStruct((n, d), F32),
                   jax.ShapeDtypeStruct((n, LANES), F32),
                   jax.ShapeDtypeStruct((1, LANES), F32)),
        grid=(n // tr,),
        in_specs=[pl.BlockSpec((tr, d), lambda i: (i, 0)),
                  pl.BlockSpec((1, MOD_ROWS, d), lambda i: (i // per_b, 0, 0)),
                  pl.BlockSpec((1, d), lambda i: (0, 0)),
                  pl.BlockSpec((d, LANES), lambda i: (0, 0)),
                  pl.BlockSpec((tr, tr), lambda i: (0, 0))],
        out_specs=(pl.BlockSpec((tr, d), lambda i: (i, 0)),
                   pl.BlockSpec((tr, LANES), lambda i: (i, 0)),
                   pl.BlockSpec((1, LANES), lambda i: (0, 0))),
        scratch_shapes=[pltpu.VMEM((1, LANES), F32)],
        compiler_params=_cparams(("arbitrary",)),
        name="moe_route",
    )(x_flat, mod, gnorm.reshape(1, d), rpad, tri)


ROW_DMA_UNROLL = 8


def _wait_rows(any_ref, rows, sem):
    blk = any_ref.at[pl.ds(0, rows)]
    pltpu.make_async_copy(blk, blk, sem).wait()


def _dispatch_kernel(pos_ref, h_ref, xs_in_ref, xs_ref, sem, *, td):
    del xs_in_ref

    def issue(t, c):
        for k in range(TOP_K):
            pltpu.make_async_copy(h_ref.at[pl.ds(t, 1)], xs_ref.at[pl.ds(pos_ref[0, 0, k * td + t], 1)],
                                  sem).start(priority=k)
        return c

    lax.fori_loop(0, td, issue, 0, unroll=ROW_DMA_UNROLL)
    _wait_rows(xs_ref, TOP_K * td, sem)


def _dispatch(h, pos_tiles, np_rows, td):
    n, d = h.shape
    xs0 = jnp.zeros((np_rows, d), F32)
    return pl.pallas_call(
        functools.partial(_dispatch_kernel, td=td),
        out_shape=jax.ShapeDtypeStruct((np_rows, d), F32),
        grid=(n // td,),
        in_specs=[pl.BlockSpec((1, 1, TOP_K * td), lambda i: (i, 0, 0), memory_space=pltpu.SMEM),
                  pl.BlockSpec((td, d), lambda i: (i, 0)),
                  pl.BlockSpec(memory_space=pl.ANY)],
        out_specs=pl.BlockSpec(memory_space=pl.ANY),
        scratch_shapes=[pltpu.SemaphoreType.DMA(())],
        input_output_aliases={2: 0},
        compiler_params=_cparams(("arbitrary",)),
        name="moe_dispatch",
    )(pos_tiles, h, xs0)


def _combine_kernel(pos_ref, posn_ref, ys_ref, meta_ref, x_ref, mod_ref, g_ref, o_ref, buf_ref, sems, *, td):
    i = pl.program_id(0)
    n = pl.num_programs(0)
    slot = lax.rem(i, 2)

    def gather(p_ref, s):
        def issue(t, c):
            for k in range(TOP_K):
                pltpu.make_async_copy(ys_ref.at[pl.ds(p_ref[0, 0, k * td + t], 1)],
                                      buf_ref.at[s, k, pl.ds(t, 1)], sems.at[s]).start(priority=k)
            return c

        lax.fori_loop(0, td, issue, 0, unroll=ROW_DMA_UNROLL)

    @pl.when(i == 0)
    def _():
        gather(pos_ref, slot)

    @pl.when(i + 1 < n)
    def _():
        gather(posn_ref, 1 - slot)

    _wait_rows(ys_ref, TOP_K * td, sems.at[slot])
    meta = meta_ref[...]
    y = meta[:, 2:3] * buf_ref[slot, 0] + meta[:, 3:4] * buf_ref[slot, 1]
    x = x_ref[...] + mod_ref[0][5:6] * y
    o_ref[...] = _rms(x, g_ref[...])


def _combine(ys, pos_tiles, meta, x_flat, mod, norm_final, tokens_per_batch, td):
    n, d = x_flat.shape
    per_b = tokens_per_batch // td
    nt = n // td
    return pl.pallas_call(
        functools.partial(_combine_kernel, td=td),
        out_shape=jax.ShapeDtypeStruct((n, d), F32),
        grid=(nt,),
        in_specs=[pl.BlockSpec((1, 1, TOP_K * td), lambda i: (i, 0, 0), memory_space=pltpu.SMEM),
                  pl.BlockSpec((1, 1, TOP_K * td), lambda i: (jnp.minimum(i + 1, nt - 1), 0, 0),
                               memory_space=pltpu.SMEM),
                  pl.BlockSpec(memory_space=pl.ANY),
                  pl.BlockSpec((td, LANES), lambda i: (i, 0)),
                  pl.BlockSpec((td, d), lambda i: (i, 0)),
                  pl.BlockSpec((1, MOD_ROWS, d), lambda i: (i // per_b, 0, 0)),
                  pl.BlockSpec((1, d), lambda i: (0, 0))],
        out_specs=pl.BlockSpec((td, d), lambda i: (i, 0)),
        scratch_shapes=[pltpu.VMEM((2, TOP_K, td, d), F32), pltpu.SemaphoreType.DMA((2,))],
        compiler_params=_cparams(("arbitrary",)),
        name="moe_combine",
    )(pos_tiles, pos_tiles, ys, meta, x_flat, mod, norm_final.reshape(1, d))


def _moe(x, mod, gnorm, router, w1, w3, w2, norm_final, tm=1024, td=256):
    bx, l, d = x.shape
    n = bx * l
    n_exp = router.shape[1]
    x_flat = x.reshape(n, d)
    h, meta, counts = _route(x_flat, mod, gnorm, router, l)
    idx = meta[:, 0:TOP_K].astype(jnp.int32)
    rank = meta[:, 4:4 + TOP_K].astype(jnp.int32)
    cnt = counts[0, :n_exp].astype(jnp.int32)
    tiles_e = (cnt + tm - 1) // tm
    tile_end = jnp.cumsum(tiles_e)
    start_rows = (tile_end - tiles_e) * tm
    pos = start_rows[idx] + rank
    n_tiles = (n * TOP_K) // tm + n_exp
    np_rows = n_tiles * tm
    tile_ids = jnp.arange(n_tiles, dtype=jnp.int32)
    used = tile_end[-1]
    tile_expert = jnp.sum((jnp.minimum(tile_ids, used - 1)[:, None] >= tile_end[None, :]).astype(jnp.int32), axis=1)
    tile_expert = jnp.minimum(tile_expert, n_exp - 1)
    tile_valid = (tile_ids < used).astype(jnp.int32)
    pos_tiles = pos.reshape(n // td, td, TOP_K).transpose(0, 2, 1).reshape(n // td, 1, TOP_K * td)
    xs = _dispatch(h, pos_tiles, np_rows, td)
    ys = _ffn_experts(xs, tile_expert, tile_valid, w1, w3, w2, tm)
    out = _combine(ys, pos_tiles, meta, x_flat, mod, norm_final, l, td)
    return out.reshape(bx, l, d)


def _mod_rows(m):
    r, n = m.shape
    return jnp.pad(m.reshape(r, N_MOD, n // N_MOD), ((0, 0), (0, MOD_ROWS - N_MOD), (0, 0)))


def _hyena_layer(x, mod, gnorm, in_w, in_b, sc_w, sc_b, spectra, f_bias, out_w, out_b, ct_pref):
    z = _norm_mod_matmul(x, mod, gnorm, in_w, in_b, 0, 1)
    g = _hyena_conv(z, sc_w, sc_b, spectra, f_bias, ct_pref)
    return _matmul_gated_residual(g, out_w, out_b, x, mod, 2)


@jax.jit
def kernel(x, c, ctx, c_ctx, ada_w, ada_b, norm_mix, norm_ffn, hy_in_w, hy_in_b, hy_sc_w, hy_sc_b, hy_f_w0, hy_f_b0, hy_f_wi, hy_f_bi, hy_f_freq, hy_f_wout, hy_f_bias, hy_out_w, hy_out_b, mla_wq_a, mla_q_norm, mla_wq_b, mla_wkv_a, mla_kv_norm, mla_wkv_b, mla_wo, ffn_w1, ffn_w3, ffn_w2, moe_router, moe_w1, moe_w3, moe_w2, norm_final):
    bsz, l, d = x.shape
    n_ctx = ctx.shape[1]
    depth = ada_w.shape[0]
    assert depth == 2, "layer 0 = Hyena + dense SwiGLU, layer 1 = MLA + expert SwiGLU"

    rows = -(-(bsz + 1) // 8) * 8
    cvec = jnp.zeros((rows, d), F32).at[:bsz].set(c).at[bsz].set(c_ctx)
    mods = _ada_mod(cvec, ada_w, ada_b)
    modx = [_mod_rows(mods[i, :bsz]) for i in range(depth)]
    modc = [_mod_rows(mods[i, bsz:bsz + 1]) for i in range(depth)]

    in_w = hy_in_w[0].astype(BF16)
    out_w = hy_out_w[0].astype(BF16)
    fargs = (hy_f_w0[0], hy_f_b0[0], hy_f_wi[0], hy_f_bi[0], hy_f_freq[0], hy_f_wout[0])
    kx = _hyena_filter_spectra(l, d, *fargs)
    kc = _hyena_filter_spectra(n_ctx, d, *fargs)
    x = _hyena_layer(x, modx[0], norm_mix[0], in_w, hy_in_b[0], hy_sc_w[0], hy_sc_b[0], kx,
                     hy_f_bias[0], out_w, hy_out_b[0], 256)
    ctx = _hyena_layer(ctx, modc[0], norm_mix[0], in_w, hy_in_b[0], hy_sc_w[0], hy_sc_b[0], kc,
                       hy_f_bias[0], out_w, hy_out_b[0], 1024)
    x = _ffn_dense(x, modx[0], norm_ffn[0], ffn_w1[0], ffn_w3[0], ffn_w2[0], 3, 4, 5)
    ctx = _ffn_dense(ctx.reshape(1, bsz * n_ctx, d), modc[0], norm_ffn[0], ffn_w1[0], ffn_w3[0], ffn_w2[0],
                     3, 4, 5).reshape(bsz, n_ctx, d)

    wa, wqb, wk, wv = _mla_weights(mla_wq_a[0], mla_wq_b[0], mla_wkv_a[0], mla_wkv_b[0])
    q_lora = mla_q_norm.shape[1]
    q, kx, vx = _mla_latent(x, modx[1], norm_mix[1], wa, mla_q_norm[0], mla_kv_norm[0], wqb, wk, wv,
                            _rope_tables(l))
    kc, vc = _mla_context(ctx, modc[1], norm_mix[1], wa[:, q_lora:], mla_kv_norm[0], wk, wv)
    o = _attention(q.transpose(0, 2, 1), kc, kx, vc.transpose(0, 2, 1), vx.transpose(0, 2, 1))
    x = _matmul_gated_residual(o, mla_wo[0].astype(BF16), jnp.zeros((d,), F32), x, modx[1], 2)
    return _moe(x, modx[1], norm_ffn[1], moe_router[0], moe_w1[0], moe_w3[0], moe_w2[0], norm_final)
```

```python
import functools
import math

import jax
import jax.numpy as jnp
import numpy as np
from jax import lax
from jax.experimental import pallas as pl
from jax.experimental.pallas import tpu as pltpu

F32 = jnp.float32
BF16 = jnp.bfloat16
HIGHEST = lax.Precision.HIGHEST

RMS_EPS = 1e-6
N_MOD = 6
MOD_ROWS = 8
GRID_W = 64
SHORT_CONV = 3
FILTER_BANDS = 8
FILTER_EMB = 1 + 2 * FILTER_BANDS
FILTER_EMB_PAD = 32
DECAY_TARGET = 1e-2
FAST_DECAY_PCT = 0.3
SLOW_DECAY_PCT = 1.5
N_HEADS = 16
QK_NOPE = 64
QK_ROPE = 32
QK_HEAD = QK_NOPE + QK_ROPE
V_HEAD = 64
ROPE_AXIS = QK_ROPE // 2
ROPE_BASE = 10000.0
TOP_K = 2
LANES = 128
HEAD_SLOT = 128
VMEM_LIMIT = 56 * 1024 * 1024


def _cparams(sem, vmem=VMEM_LIMIT):
    return pltpu.CompilerParams(dimension_semantics=sem, vmem_limit_bytes=vmem)


def _rms(x, g):
    return x * lax.rsqrt(jnp.mean(x * x, axis=-1, keepdims=True) + RMS_EPS) * g


def _silu(x):
    return x * (1.0 / (1.0 + jnp.exp(-x)))


def _pick(total, pref):
    t = min(total, pref)
    while total % t:
        t //= 2
    return t


def _ada_kernel(c_ref, w_ref, b_ref, o_ref):
    c = c_ref[...]
    o_ref[0] = jnp.dot(_silu(c), w_ref[0], precision=HIGHEST, preferred_element_type=F32) + b_ref[0]


def _ada_mod(cvec, ada_w, ada_b):
    depth, d, n = ada_w.shape
    r = cvec.shape[0]
    tn = _pick(n, 1536)
    return pl.pallas_call(
        _ada_kernel,
        out_shape=jax.ShapeDtypeStruct((depth, r, n), F32),
        grid=(depth, n // tn),
        in_specs=[pl.BlockSpec((r, d), lambda i, j: (0, 0)),
                  pl.BlockSpec((1, d, tn), lambda i, j: (i, 0, j)),
                  pl.BlockSpec((1, 1, tn), lambda i, j: (i, 0, j))],
        out_specs=pl.BlockSpec((1, r, tn), lambda i, j: (i, 0, j)),
        compiler_params=_cparams(("arbitrary", "arbitrary")),
        name="ada_mod",
    )(cvec, ada_w, ada_b.reshape(depth, 1, n))


def _nmm_kernel(x_ref, mod_ref, g_ref, w_ref, b_ref, o_ref, *, sh_row, sc_row, ct):
    x = x_ref[0]
    mod = mod_ref[0]
    h = _rms(x, g_ref[...]) * (1.0 + mod[sc_row:sc_row + 1]) + mod[sh_row:sh_row + 1]
    y = jnp.dot(h.astype(BF16), w_ref[...], preferred_element_type=F32) + b_ref[...]
    for j in range(o_ref.shape[0]):
        o_ref[j, 0] = y[:, j * ct:(j + 1) * ct]


def _norm_mod_matmul(x, mod, gnorm, w_bf16, bias, sh_row, sc_row, ct, tm_pref=512):
    bx, l, d = x.shape
    n = w_bf16.shape[1]
    tm = _pick(l, tm_pref)
    per_b = mod.shape[0] > 1
    return pl.pallas_call(
        functools.partial(_nmm_kernel, sh_row=sh_row, sc_row=sc_row, ct=ct),
        out_shape=jax.ShapeDtypeStruct((n // ct, bx, l, ct), F32),
        grid=(bx, l // tm),
        in_specs=[pl.BlockSpec((1, tm, d), lambda b, i: (b, i, 0)),
                  pl.BlockSpec((1, MOD_ROWS, d), (lambda b, i: (b, 0, 0)) if per_b else (lambda b, i: (0, 0, 0))),
                  pl.BlockSpec((1, d), lambda b, i: (0, 0)),
                  pl.BlockSpec((d, n), lambda b, i: (0, 0)),
                  pl.BlockSpec((1, n), lambda b, i: (0, 0))],
        out_specs=pl.BlockSpec((n // ct, 1, tm, ct), lambda b, i: (0, b, i, 0)),
        compiler_params=_cparams(("arbitrary", "arbitrary")),
        name="norm_mod_matmul",
    )(x, mod, gnorm.reshape(1, d), w_bf16, bias.reshape(1, n))


def _mm_res_kernel(a_ref, w_ref, b_ref, r_ref, mod_ref, o_ref, *, gate_row):
    y = jnp.dot(a_ref[0], w_ref[...], preferred_element_type=F32) + b_ref[...]
    o_ref[0] = r_ref[0] + mod_ref[0][gate_row:gate_row + 1] * y


def _matmul_gated_residual(a_bf16, w_bf16, bias, resid, mod, gate_row, tm_pref=512):
    bx, l, k = a_bf16.shape
    d = w_bf16.shape[1]
    tm = _pick(l, tm_pref)
    per_b = mod.shape[0] > 1
    return pl.pallas_call(
        functools.partial(_mm_res_kernel, gate_row=gate_row),
        out_shape=jax.ShapeDtypeStruct((bx, l, d), F32),
        grid=(bx, l // tm),
        in_specs=[pl.BlockSpec((1, tm, k), lambda b, i: (b, i, 0)),
                  pl.BlockSpec((k, d), lambda b, i: (0, 0)),
                  pl.BlockSpec((1, d), lambda b, i: (0, 0)),
                  pl.BlockSpec((1, tm, d), lambda b, i: (b, i, 0)),
                  pl.BlockSpec((1, MOD_ROWS, d), (lambda b, i: (b, 0, 0)) if per_b else (lambda b, i: (0, 0, 0)))],
        out_specs=pl.BlockSpec((1, tm, d), lambda b, i: (b, i, 0)),
        compiler_params=_cparams(("arbitrary", "arbitrary")),
        name="matmul_gated_residual",
    )(a_bf16, w_bf16, bias.reshape(1, d), resid, mod)


def _filter_kernel(z_ref, w0_ref, b0_ref, wi_ref, bi_ref, fr_ref, wt_ref, wb_ref, dl_ref, fwd_ref, o_ref, h_ref,
                   *, l, p):
    @pl.when(pl.program_id(0) == 0)
    def _():
        fr = fr_ref[...]
        h = jnp.sin(fr * (jnp.dot(z_ref[...], w0_ref[...], precision=HIGHEST, preferred_element_type=F32)
                          + b0_ref[...]))
        for n in range(wi_ref.shape[0]):
            h = jnp.sin(fr * (jnp.dot(h, wi_ref[n], precision=HIGHEST, preferred_element_type=F32) + bi_ref[n]))
        h_ref[...] = h

    top = jnp.dot(h_ref[:l], wt_ref[...], precision=HIGHEST, preferred_element_type=F32)
    bot = jnp.dot(h_ref[l:], wb_ref[...], precision=HIGHEST, preferred_element_type=F32)
    t = z_ref[:, 0:1]
    rows = lax.broadcasted_iota(jnp.int32, (2 * l, 1), 0)
    decay = jnp.where(rows == l, 0.0, jnp.exp(-t * dl_ref[...]))
    k = jnp.concatenate([top, bot], axis=0) * decay
    k = k / jnp.sum(jnp.abs(k), axis=0, keepdims=True)
    nb = l // p
    for di in range(2 * nb - 1):
        start = (p * (di - nb)) % (2 * l)
        if start + 2 * p <= 2 * l:
            seg = k[start:start + 2 * p]
        else:
            seg = jnp.concatenate([k[start:], k[:start + 2 * p - 2 * l]], axis=0)
        o_ref[di] = jnp.dot(fwd_ref[...], seg.astype(BF16), preferred_element_type=F32)


def _conv_block(l, p_pref=512):
    return min(p_pref, l)


def _hyena_filter_spectra(l, d, w0, b0, wi, bi, freq, wout):
    p = _conv_block(l)
    nd = 2 * (l // p) - 1
    pos = jnp.arange(l, dtype=F32)
    t = (pos / max(l - 1, 1))[:, None]
    w = 2.0 * math.pi * pos / l
    f = jnp.linspace(1e-4, FILTER_BANDS - 1, FILTER_BANDS, dtype=F32)
    ang = w[:, None] * f[None, :]
    z = jnp.concatenate([t, jnp.cos(ang), -jnp.sin(ang)], axis=-1)
    deltas = jnp.abs(jnp.linspace(math.log(DECAY_TARGET) / SLOW_DECAY_PCT,
                                  math.log(DECAY_TARGET) / FAST_DECAY_PCT, d, dtype=F32))
    idx = np.concatenate([np.arange(l), [0], np.arange(l - 1, 0, -1)])
    zc = jnp.pad(z[idx], ((0, 0), (0, FILTER_EMB_PAD - FILTER_EMB)))
    w0p = jnp.pad(w0, ((0, FILTER_EMB_PAD - FILTER_EMB), (0, 0)))
    hid = w0.shape[1]
    n_in = wi.shape[0]
    ct = _pick(d, 256)
    nct = d // ct
    fwd = jnp.asarray(_dft_mats(p)[0]).astype(BF16)
    return pl.pallas_call(
        functools.partial(_filter_kernel, l=l, p=p),
        out_shape=jax.ShapeDtypeStruct((nd, 2 * p, d), F32),
        grid=(nct,),
        in_specs=[pl.BlockSpec((2 * l, FILTER_EMB_PAD), lambda j: (0, 0)),
                  pl.BlockSpec((FILTER_EMB_PAD, hid), lambda j: (0, 0)),
                  pl.BlockSpec((1, hid), lambda j: (0, 0)),
                  pl.BlockSpec((n_in, hid, hid), lambda j: (0, 0, 0)),
                  pl.BlockSpec((n_in, 1, hid), lambda j: (0, 0, 0)),
                  pl.BlockSpec((1, hid), lambda j: (0, 0)),
                  pl.BlockSpec((hid, ct), lambda j: (0, j)),
                  pl.BlockSpec((hid, ct), lambda j: (0, nct + j)),
                  pl.BlockSpec((1, ct), lambda j: (0, j)),
                  pl.BlockSpec((2 * p, 2 * p), lambda j: (0, 0))],
        out_specs=pl.BlockSpec((nd, 2 * p, ct), lambda j: (0, 0, j)),
        scratch_shapes=[pltpu.VMEM((2 * l, hid), F32)],
        compiler_params=_cparams(("arbitrary",)),
        name="hyena_filter",
    )(zc, w0p, b0.reshape(1, hid), wi, bi.reshape(n_in, 1, hid), freq.reshape(1, hid), wout, wout,
      deltas.reshape(1, d), fwd)


@functools.lru_cache(maxsize=None)
def _dft_mats(p):
    n = 2 * p
    f = np.arange(p)[:, None]
    t = np.arange(n)[None, :]
    ang = 2.0 * np.pi * (((2 * f + 1) * t) % (4 * p)) / (4 * p)
    fwd = np.concatenate([np.cos(ang), -np.sin(ang)], axis=0)
    q = np.arange(p)[:, None]
    ff = np.arange(p)[None, :]
    ang2 = 2.0 * np.pi * (((2 * ff + 1) * (q + p)) % (4 * p)) / (4 * p)
    inv = np.concatenate([np.cos(ang2), -np.sin(ang2)], axis=1) / p
    return fwd.astype(np.float32), inv.astype(np.float32)


def _hyena_conv_kernel(z0_ref, z1_ref, z2_ref, w0_ref, w1_ref, w2_ref, b0_ref, b1_ref, b2_ref,
                       ks_ref, fb_ref, fwd_ref, inv_ref, o_ref, vs_ref, ys_ref, *, l, p, rc):
    nb = l // p
    rows = lax.broadcasted_iota(jnp.int32, (l, z0_ref.shape[3]), 0)

    def sconv(z_ref, w_ref, b_ref):
        z = z_ref[0, 0]
        w = w_ref[...]
        zm = jnp.where(rows == 0, 0.0, pltpu.roll(z, 1, 0))
        zp = jnp.where(rows == l - 1, 0.0, pltpu.roll(z, l - 1, 0))
        return zm * w[0:1] + z * w[1:2] + zp * w[2:3] + b_ref[...]

    u = sconv(z2_ref, w2_ref, b2_ref) * sconv(z1_ref, w1_ref, b1_ref)
    ub = u.astype(BF16)
    for j in range(nb):
        vs_ref[j] = jnp.dot(fwd_ref[...], ub[j * p:(j + 1) * p], preferred_element_type=F32)

    def chunk(c, carry):
        r0 = pl.multiple_of(c * rc, rc)
        re = pl.ds(r0, rc)
        im = pl.ds(p + r0, rc)
        for i in range(nb):
            yr = None
            yi = None
            for j in range(nb):
                di = i - j + nb - 1
                kr = ks_ref[di, re, :]
                ki = ks_ref[di, im, :]
                vr = vs_ref[j, re, :]
                vi = vs_ref[j, im, :]
                tr = kr * vr - ki * vi
                ti = kr * vi + ki * vr
                yr = tr if yr is None else yr + tr
                yi = ti if yi is None else yi + ti
            ys_ref[i, re, :] = yr
            ys_ref[i, im, :] = yi
        return carry

    lax.fori_loop(0, p // rc, chunk, 0)

    x0 = sconv(z0_ref, w0_ref, b0_ref)
    fb = fb_ref[...]
    for i in range(nb):
        y = jnp.dot(inv_ref[...], ys_ref[i].astype(BF16), preferred_element_type=F32)
        sl = slice(i * p, (i + 1) * p)
        o_ref[0, sl, :] = (x0[sl] * (y + u[sl] * fb)).astype(BF16)


def _hyena_conv(z, sc_w, sc_b, spectra, f_bias):
    nz, bx, l, ct = z.shape
    nct = nz // 3
    d = nct * ct
    d3 = 3 * d
    nd, p2, _ = spectra.shape
    p = p2 // 2
    nb = l // p
    fwd, inv = _dft_mats(p)
    fwd = jnp.asarray(fwd[:, :p]).astype(BF16)
    inv = jnp.asarray(inv).astype(BF16)
    zspec = lambda part: pl.BlockSpec((1, 1, l, ct), lambda j, b, part=part: (part * nct + j, b, 0, 0))
    wspec = lambda part: pl.BlockSpec((SHORT_CONV, ct), lambda j, b, part=part: (0, part * nct + j))
    bspec = lambda part: pl.BlockSpec((1, ct), lambda j, b, part=part: (0, part * nct + j))
    once = pl.Buffered(1)
    scb = sc_b.reshape(1, d3)
    return pl.pallas_call(
        functools.partial(_hyena_conv_kernel, l=l, p=p, rc=8),
        out_shape=jax.ShapeDtypeStruct((bx, l, d), BF16),
        grid=(nct, bx),
        in_specs=[zspec(0), zspec(1), zspec(2), wspec(0), wspec(1), wspec(2), bspec(0), bspec(1), bspec(2),
                  pl.BlockSpec((nd, 2 * p, ct), lambda j, b: (0, 0, j), pipeline_mode=once),
                  pl.BlockSpec((1, ct), lambda j, b: (0, j)),
                  pl.BlockSpec((2 * p, p), lambda j, b: (0, 0), pipeline_mode=once),
                  pl.BlockSpec((p, 2 * p), lambda j, b: (0, 0), pipeline_mode=once)],
        out_specs=pl.BlockSpec((1, l, ct), lambda j, b: (b, 0, j)),
        scratch_shapes=[pltpu.VMEM((nb, 2 * p, ct), F32),
                        pltpu.VMEM((nb, 2 * p, ct), F32)],
        compiler_params=_cparams(("arbitrary", "arbitrary")),
        name="hyena_conv",
    )(z, z, z, sc_w, sc_w, sc_w, scb, scb, scb, spectra, f_bias.reshape(1, d), fwd, inv)


def _swiglu_step(h_bf16, w1_ref, w3_ref, w2_ref, acc_ref):
    a = jnp.dot(h_bf16, w1_ref[0].astype(BF16), preferred_element_type=F32)
    b = jnp.dot(h_bf16, w3_ref[0].astype(BF16), preferred_element_type=F32)
    g = (_silu(a) * b).astype(BF16)
    acc_ref[...] += jnp.dot(g, w2_ref[0].astype(BF16), preferred_element_type=F32)


def _ffn_dense_kernel(x_ref, mod_ref, g_ref, w1_ref, w3_ref, w2_ref, o_ref, h_ref, acc_ref,
                      *, sh_row, sc_row, gate_row):
    f = pl.program_id(2)

    @pl.when(f == 0)
    def _():
        mod = mod_ref[0]
        h = _rms(x_ref[0], g_ref[...]) * (1.0 + mod[sc_row:sc_row + 1]) + mod[sh_row:sh_row + 1]
        h_ref[...] = h.astype(BF16)
        acc_ref[...] = jnp.zeros_like(acc_ref)

    _swiglu_step(h_ref[...], w1_ref, w3_ref, w2_ref, acc_ref)

    @pl.when(f == pl.num_programs(2) - 1)
    def _():
        o_ref[0] = x_ref[0] + mod_ref[0][gate_row:gate_row + 1] * acc_ref[...]


def _ffn_dense(x, mod, gnorm, w1, w3, w2, sh_row, sc_row, gate_row, tm_pref=1024, tf_pref=512):
    bx, l, d = x.shape
    ff = w1.shape[1]
    tm = _pick(l, tm_pref)
    tf = _pick(ff, tf_pref)
    per_b = mod.shape[0] > 1
    return pl.pallas_call(
        functools.partial(_ffn_dense_kernel, sh_row=sh_row, sc_row=sc_row, gate_row=gate_row),
        out_shape=jax.ShapeDtypeStruct((bx, l, d), F32),
        grid=(bx, l // tm, ff // tf),
        in_specs=[pl.BlockSpec((1, tm, d), lambda b, i, f: (b, i, 0)),
                  pl.BlockSpec((1, MOD_ROWS, d), (lambda b, i, f: (b, 0, 0)) if per_b else (lambda b, i, f: (0, 0, 0))),
                  pl.BlockSpec((1, d), lambda b, i, f: (0, 0)),
                  pl.BlockSpec((1, d, tf), lambda b, i, f: (0, 0, f)),
                  pl.BlockSpec((1, d, tf), lambda b, i, f: (0, 0, f)),
                  pl.BlockSpec((1, tf, d), lambda b, i, f: (0, f, 0))],
        out_specs=pl.BlockSpec((1, tm, d), lambda b, i, f: (b, i, 0)),
        scratch_shapes=[pltpu.VMEM((tm, d), BF16), pltpu.VMEM((tm, d), F32)],
        compiler_params=_cparams(("arbitrary", "arbitrary", "arbitrary")),
        name="ffn_dense",
    )(x, mod, gnorm.reshape(1, d), w1[None], w3[None], w2[None])


def _ffn_expert_kernel(te_ref, tv_ref, x_ref, w1_ref, w3_ref, w2_ref, o_ref, h_ref, acc_ref):
    i = pl.program_id(0)
    f = pl.program_id(1)

    @pl.when(tv_ref[i] > 0)
    def _():
        @pl.when(f == 0)
        def _():
            h_ref[...] = x_ref[...].astype(BF16)
            acc_ref[...] = jnp.zeros_like(acc_ref)

        _swiglu_step(h_ref[...], w1_ref, w3_ref, w2_ref, acc_ref)

        @pl.when(f == pl.num_programs(1) - 1)
        def _():
            o_ref[...] = acc_ref[...]

    @pl.when((tv_ref[i] == 0) & (f == 0))
    def _():
        o_ref[...] = jnp.zeros_like(o_ref)


def _ffn_experts(xs, tile_expert, tile_valid, w1, w3, w2, tm, tf_pref=512):
    np_rows, d = xs.shape
    ff = w1.shape[2]
    tf = _pick(ff, tf_pref)
    nf = ff // tf
    fsel = lambda i, f, tv: jnp.where(tv[i] > 0, f, nf - 1)
    grid_spec = pltpu.PrefetchScalarGridSpec(
        num_scalar_prefetch=2,
        grid=(np_rows // tm, nf),
        in_specs=[pl.BlockSpec((tm, d), lambda i, f, te, tv: (i, 0)),
                  pl.BlockSpec((1, d, tf), lambda i, f, te, tv: (te[i], 0, fsel(i, f, tv))),
                  pl.BlockSpec((1, d, tf), lambda i, f, te, tv: (te[i], 0, fsel(i, f, tv))),
                  pl.BlockSpec((1, tf, d), lambda i, f, te, tv: (te[i], fsel(i, f, tv), 0))],
        out_specs=pl.BlockSpec((tm, d), lambda i, f, te, tv: (i, 0)),
        scratch_shapes=[pltpu.VMEM((tm, d), BF16), pltpu.VMEM((tm, d), F32)],
    )
    return pl.pallas_call(
        _ffn_expert_kernel,
        out_shape=jax.ShapeDtypeStruct((np_rows, d), F32),
        grid_spec=grid_spec,
        compiler_params=_cparams(("arbitrary", "arbitrary")),
        name="ffn_experts",
    )(tile_expert, tile_valid, xs, w1, w3, w2)


def _mla_latent_kernel(x_ref, mod_ref, g_ref, wa_ref, qg_ref, kvg_ref, wqb_ref, wk_ref, wv_ref, tab_ref,
                       q_ref, k_ref, v_ref, *, q_lora, kv_lora):
    mod = mod_ref[0]
    h = _rms(x_ref[0], g_ref[...]) * (1.0 + mod[1:2]) + mod[0:1]
    a = jnp.dot(h.astype(BF16), wa_ref[...], preferred_element_type=F32)
    qn = _rms(a[:, :q_lora], qg_ref[...]).astype(BF16)
    cn = _rms(a[:, q_lora:q_lora + kv_lora], kvg_ref[...]).astype(BF16)
    kpe = a[:, q_lora + kv_lora:]
    tab = tab_ref[...]
    cq, s1q, s2q, ck, s1k, s2k = (tab[:, n * LANES:(n + 1) * LANES] for n in range(6))
    q = jnp.dot(qn, wqb_ref[...], preferred_element_type=F32)
    wq = q.shape[1]
    rep = wq // LANES
    q = (q * jnp.tile(cq, (1, rep)) + pltpu.roll(q, ROPE_AXIS, 1) * jnp.tile(s1q, (1, rep))
         + pltpu.roll(q, wq - ROPE_AXIS, 1) * jnp.tile(s2q, (1, rep)))
    q_ref[0] = q.astype(BF16)
    kr = kpe * ck + pltpu.roll(kpe, ROPE_AXIS, 1) * s1k + pltpu.roll(kpe, LANES - ROPE_AXIS, 1) * s2k
    k = jnp.dot(cn, wk_ref[...], preferred_element_type=F32) + jnp.tile(kr, (1, rep))
    k_ref[0] = k.astype(BF16)
    v_ref[0] = jnp.dot(cn, wv_ref[...], preferred_element_type=F32).astype(BF16)


def _mla_context_kernel(x_ref, mod_ref, g_ref, wa_ref, kvg_ref, wk_ref, wv_ref, k_ref, v_ref, *, kv_lora):
    mod = mod_ref[0]
    h = _rms(x_ref[0], g_ref[...]) * (1.0 + mod[1:2]) + mod[0:1]
    a = jnp.dot(h.astype(BF16), wa_ref[...], preferred_element_type=F32)
    cn = _rms(a[:, :kv_lora], kvg_ref[...]).astype(BF16)
    kpe = a[:, kv_lora:]
    rep = wk_ref.shape[1] // LANES
    k = jnp.dot(cn, wk_ref[...], preferred_element_type=F32) + jnp.tile(kpe, (1, rep))
    k_ref[0] = k.astype(BF16)
    v_ref[0] = jnp.dot(cn, wv_ref[...], preferred_element_type=F32).astype(BF16)


def _rope_tables(l):
    rows = l // GRID_W
    row = jnp.broadcast_to(jnp.arange(rows, dtype=F32)[:, None], (rows, GRID_W)).reshape(l)
    col = jnp.broadcast_to(jnp.arange(GRID_W, dtype=F32)[None, :], (rows, GRID_W)).reshape(l)
    inv = ROPE_BASE ** (-jnp.arange(0, ROPE_AXIS, 2, dtype=F32) / ROPE_AXIS)
    ang = jnp.concatenate([row[:, None] * inv, col[:, None] * inv], axis=-1)
    cos = jnp.cos(ang)
    sin = jnp.sin(ang)
    n = l
    ones = jnp.ones((n, QK_NOPE), F32)
    z16 = jnp.zeros((n, ROPE_AXIS), F32)
    z64 = jnp.zeros((n, QK_NOPE), F32)
    zpad = jnp.zeros((n, HEAD_SLOT - QK_HEAD), F32)
    c = jnp.concatenate([ones, cos, cos, zpad], axis=1)
    s1 = jnp.concatenate([z64, z16, sin, zpad], axis=1)
    s2 = jnp.concatenate([z64, -sin, z16, zpad], axis=1)
    scale = math.log2(math.e) / math.sqrt(QK_HEAD)
    return jnp.concatenate([c * scale, s1 * scale, s2 * scale, c, s1, s2], axis=1)


def _mla_weights(wq_a, wq_b, wkv_a, wkv_b):
    d, q_lora = wq_a.shape
    kv_lora = wkv_a.shape[1] - QK_ROPE
    ev = np.arange(0, QK_ROPE, 2)
    od = np.arange(1, QK_ROPE, 2)
    kpe = wkv_a[:, kv_lora:]
    kpe_slot = jnp.concatenate([jnp.zeros((d, QK_NOPE), F32), kpe[:, ev], kpe[:, od],
                                jnp.zeros((d, HEAD_SLOT - QK_HEAD), F32)], axis=1)
    wa = jnp.concatenate([wq_a, wkv_a[:, :kv_lora], kpe_slot], axis=1)
    qb = wq_b.reshape(q_lora, N_HEADS, QK_HEAD)
    qb = jnp.concatenate([qb[..., :QK_NOPE], qb[..., QK_NOPE + ev], qb[..., QK_NOPE + od],
                          jnp.zeros((q_lora, N_HEADS, HEAD_SLOT - QK_HEAD), F32)], axis=-1)
    kvb = wkv_b.reshape(kv_lora, N_HEADS, QK_NOPE + V_HEAD)
    wk = jnp.concatenate([kvb[..., :QK_NOPE], jnp.zeros((kv_lora, N_HEADS, HEAD_SLOT - QK_NOPE), F32)], axis=-1)
    wv = kvb[..., QK_NOPE:]
    return (wa.astype(BF16), qb.reshape(q_lora, N_HEADS * HEAD_SLOT).astype(BF16),
            wk.reshape(kv_lora, N_HEADS * HEAD_SLOT).astype(BF16),
            wv.reshape(kv_lora, N_HEADS * V_HEAD).astype(BF16))


def _mod_spec(mod):
    per_b = mod.shape[0] > 1
    return pl.BlockSpec((1, MOD_ROWS, mod.shape[2]), (lambda b, i: (b, 0, 0)) if per_b else (lambda b, i: (0, 0, 0)))


def _mla_latent(x, mod, gnorm, wa, q_norm, kv_norm, wqb, wk, wv, tables, tm_pref=512):
    bx, n, d = x.shape
    tm = _pick(n, tm_pref)
    q_lora = q_norm.shape[0]
    kv_lora = kv_norm.shape[0]
    wq = wqb.shape[1]
    wvn = wv.shape[1]
    const = lambda b, i: (0, 0)
    row = lambda width: pl.BlockSpec((1, tm, width), lambda b, i: (b, i, 0))
    return pl.pallas_call(
        functools.partial(_mla_latent_kernel, q_lora=q_lora, kv_lora=kv_lora),
        out_shape=(jax.ShapeDtypeStruct((bx, n, wq), BF16),
                   jax.ShapeDtypeStruct((bx, n, wq), BF16),
                   jax.ShapeDtypeStruct((bx, n, wvn), BF16)),
        grid=(bx, n // tm),
        in_specs=[row(d), _mod_spec(mod),
                  pl.BlockSpec((1, d), const),
                  pl.BlockSpec(wa.shape, const),
                  pl.BlockSpec((1, q_lora), const),
                  pl.BlockSpec((1, kv_lora), const),
                  pl.BlockSpec(wqb.shape, const),
                  pl.BlockSpec(wk.shape, const),
                  pl.BlockSpec(wv.shape, const),
                  pl.BlockSpec((tm, 6 * LANES), lambda b, i: (i, 0))],
        out_specs=(row(wq), row(wq), row(wvn)),
        compiler_params=_cparams(("arbitrary", "arbitrary")),
        name="mla_latent_proj",
    )(x, mod, gnorm.reshape(1, d), wa, q_norm.reshape(1, q_lora), kv_norm.reshape(1, kv_lora),
      wqb, wk, wv, tables)


def _mla_context(ctx, mod, gnorm, wa_kv, kv_norm, wk, wv, tm_pref=512):
    bx, n, d = ctx.shape
    tm = _pick(n, tm_pref)
    kv_lora = kv_norm.shape[0]
    wkn = wk.shape[1]
    wvn = wv.shape[1]
    const = lambda b, i: (0, 0)
    row = lambda width: pl.BlockSpec((1, tm, width), lambda b, i: (b, i, 0))
    return pl.pallas_call(
        functools.partial(_mla_context_kernel, kv_lora=kv_lora),
        out_shape=(jax.ShapeDtypeStruct((bx, n, wkn), BF16),
                   jax.ShapeDtypeStruct((bx, n, wvn), BF16)),
        grid=(bx, n // tm),
        in_specs=[row(d), _mod_spec(mod),
                  pl.BlockSpec((1, d), const),
                  pl.BlockSpec(wa_kv.shape, const),
                  pl.BlockSpec((1, kv_lora), const),
                  pl.BlockSpec(wk.shape, const),
                  pl.BlockSpec(wv.shape, const)],
        out_specs=(row(wkn), row(wvn)),
        compiler_params=_cparams(("arbitrary", "arbitrary")),
        name="mla_context_proj",
    )(ctx, mod, gnorm.reshape(1, d), wa_kv, kv_norm.reshape(1, kv_lora), wk, wv)


ATTN_HEADS_PER_STEP = 8
ATTN_KEY_CHUNK = 256
ATTN_SOFTMAX_LAG = 2
ATTN_PV_LAG = 6
SUM_ROWS = 16


def _attn_kernel(qt_ref, kc_ref, kx_ref, vct_ref, vxt_ref, o_ref, *, heads, ck):
    chunks = ([(kc_ref, vct_ref, c) for c in range(kc_ref.shape[1] // ck)]
              + [(kx_ref, vxt_ref, c) for c in range(kx_ref.shape[1] // ck)])
    items = [(h, ci) for ci in range(len(chunks)) for h in range(heads)]
    state = {h: None for h in range(heads)}
    scores = {}
    probs = {}
    ones = jnp.ones((SUM_ROWS, ck), BF16)

    def qk(t):
        h, ci = items[t]
        kref, _, c = chunks[ci]
        hsl = slice(h * HEAD_SLOT, (h + 1) * HEAD_SLOT)
        scores[t] = jnp.dot(kref[0, c * ck:(c + 1) * ck, hsl], qt_ref[0, hsl, :], preferred_element_type=F32)

    def softmax(t):
        h, _ = items[t]
        s = scores.pop(t)
        mc = jnp.max(s, axis=0, keepdims=True)
        if state[h] is None:
            probs[t] = (jnp.exp2(s - mc).astype(BF16), None)
            state[h] = (mc, None)
        else:
            m, acc = state[h]
            m_new = jnp.maximum(m, mc)
            probs[t] = (jnp.exp2(s - m_new).astype(BF16), jnp.exp2(m - m_new))
            state[h] = (m_new, acc)

    def pv(t):
        h, ci = items[t]
        _, vref, c = chunks[ci]
        p, alpha = probs.pop(t)
        lhs = jnp.concatenate([vref[0, h * V_HEAD:(h + 1) * V_HEAD, c * ck:(c + 1) * ck], ones], axis=0)
        o = jnp.dot(lhs, p, preferred_element_type=F32)
        m, acc = state[h]
        state[h] = (m, o if acc is None else alpha * acc + o)

    n_items = len(items)
    for t in range(n_items + ATTN_PV_LAG):
        if t < n_items:
            qk(t)
        if ATTN_SOFTMAX_LAG <= t < n_items + ATTN_SOFTMAX_LAG:
            softmax(t - ATTN_SOFTMAX_LAG)
        if t >= ATTN_PV_LAG:
            pv(t - ATTN_PV_LAG)

    for hp in range(heads // 2):
        outs = [state[h][1][:V_HEAD] / state[h][1][V_HEAD:V_HEAD + 1] for h in (2 * hp, 2 * hp + 1)]
        pair = jnp.concatenate(outs, axis=0)
        o_ref[0, :, hp * 2 * V_HEAD:(hp + 1) * 2 * V_HEAD] = pair.T.astype(BF16)


def _attention(qt, kc, kx, vct, vxt, tq_pref=256, heads=ATTN_HEADS_PER_STEP):
    bx, _, l = qt.shape
    n_ctx = kc.shape[1]
    tq = _pick(l, tq_pref)
    ck = _pick(n_ctx, ATTN_KEY_CHUNK)
    qk_w = heads * HEAD_SLOT
    v_w = heads * V_HEAD
    return pl.pallas_call(
        functools.partial(_attn_kernel, heads=heads, ck=ck),
        out_shape=jax.ShapeDtypeStruct((bx, l, N_HEADS * V_HEAD), BF16),
        grid=(bx, N_HEADS // heads, l // tq),
        in_specs=[pl.BlockSpec((1, qk_w, tq), lambda b, g, i: (b, g, i)),
                  pl.BlockSpec((1, n_ctx, qk_w), lambda b, g, i: (b, 0, g)),
                  pl.BlockSpec((1, l, qk_w), lambda b, g, i: (b, 0, g)),
                  pl.BlockSpec((1, v_w, n_ctx), lambda b, g, i: (b, g, 0)),
                  pl.BlockSpec((1, v_w, l), lambda b, g, i: (b, g, 0))],
        out_specs=pl.BlockSpec((1, tq, v_w), lambda b, g, i: (b, i, g)),
        compiler_params=_cparams(("arbitrary", "arbitrary", "arbitrary")),
        name="mla_attention",
    )(qt, kc, kx, vct, vxt)


def _route_kernel(x_ref, mod_ref, g_ref, r_ref, tri_ref, h_ref, meta_ref, cnt_ref, carry_ref, *, n_exp):
    i = pl.program_id(0)

    @pl.when(i == 0)
    def _():
        carry_ref[...] = jnp.zeros_like(carry_ref)

    mod = mod_ref[0]
    h = _rms(x_ref[...], g_ref[...]) * (1.0 + mod[4:5]) + mod[3:4]
    h_ref[...] = h
    logits = jnp.dot(h, r_ref[...], precision=HIGHEST, preferred_element_type=F32)
    lane = lax.broadcasted_iota(jnp.int32, logits.shape, 1)
    lane_f = lane.astype(F32)
    neg = jnp.float32(-jnp.inf)
    lg = jnp.where(lane < n_exp, logits, neg)
    v1 = jnp.max(lg, axis=-1, keepdims=True)
    i1 = jnp.min(jnp.where(lg == v1, lane_f, float(LANES)), axis=-1, keepdims=True)
    oh1 = lane_f == i1
    lg2 = jnp.where(oh1, neg, lg)
    v2 = jnp.max(lg2, axis=-1, keepdims=True)
    i2 = jnp.min(jnp.where(lg2 == v2, lane_f, float(LANES)), axis=-1, keepdims=True)
    oh2 = lane_f == i2
    e = jnp.exp(v2 - v1)
    g1 = 1.0 / (1.0 + e)
    g2 = e / (1.0 + e)
    oh = jnp.where(oh1 | oh2, 1.0, 0.0)
    pref = jnp.dot(tri_ref[...], oh.astype(BF16), preferred_element_type=F32)
    excl = pref - oh + carry_ref[...]
    r1 = jnp.sum(jnp.where(oh1, excl, 0.0), axis=-1, keepdims=True)
    r2 = jnp.sum(jnp.where(oh2, excl, 0.0), axis=-1, keepdims=True)
    carry_ref[...] += jnp.sum(oh, axis=0, keepdims=True)
    cnt_ref[...] = carry_ref[...]
    meta = jnp.where(lane == 0, i1, 0.0)
    meta = jnp.where(lane == 1, i2, meta)
    meta = jnp.where(lane == 2, g1, meta)
    meta = jnp.where(lane == 3, g2, meta)
    meta = jnp.where(lane == 4, r1, meta)
    meta = jnp.where(lane == 5, r2, meta)
    meta_ref[...] = meta


def _route(x_flat, mod, gnorm, router, tokens_per_batch, tr_pref=512):
    n, d = x_flat.shape
    n_exp = router.shape[1]
    tr = _pick(tokens_per_batch, tr_pref)
    per_b = tokens_per_batch // tr
    rpad = jnp.pad(router, ((0, 0), (0, LANES - n_exp)))
    tri = jnp.asarray(np.tril(np.ones((tr, tr), np.float32))).astype(BF16)
    return pl.pallas_call(
        functools.partial(_route_kernel, n_exp=n_exp),
        out_shape=(jax.ShapeDtypeStruct((n, d), F32),
                   jax.ShapeDtypeStruct((n, LANES), F32),
                   jax.ShapeDtypeStruct((1, LANES), F32)),
        grid=(n // tr,),
        in_specs=[pl.BlockSpec((tr, d), lambda i: (i, 0)),
                  pl.BlockSpec((1, MOD_ROWS, d), lambda i: (i // per_b, 0, 0)),
                  pl.BlockSpec((1, d), lambda i: (0, 0)),
                  pl.BlockSpec((d, LANES), lambda i: (0, 0)),
                  pl.BlockSpec((tr, tr), lambda i: (0, 0))],
        out_specs=(pl.BlockSpec((tr, d), lambda i: (i, 0)),
                   pl.BlockSpec((tr, LANES), lambda i: (i, 0)),
                   pl.BlockSpec((1, LANES), lambda i: (0, 0))),
        scratch_shapes=[pltpu.VMEM((1, LANES), F32)],
        compiler_params=_cparams(("arbitrary",)),
        name="moe_route",
    )(x_flat, mod, gnorm.reshape(1, d), rpad, tri)


ROW_DMA_UNROLL = 8


def _wait_rows(any_ref, rows, sem):
    blk = any_ref.at[pl.ds(0, rows)]
    pltpu.make_async_copy(blk, blk, sem).wait()


def _dispatch_kernel(pos_ref, h_ref, xs_in_ref, xs_ref, sem, *, td):
    del xs_in_ref

    def issue(t, c):
        for k in range(TOP_K):
            pltpu.make_async_copy(h_ref.at[pl.ds(t, 1)], xs_ref.at[pl.ds(pos_ref[0, 0, k * td + t], 1)],
                                  sem).start(priority=k)
        return c

    lax.fori_loop(0, td, issue, 0, unroll=ROW_DMA_UNROLL)
    _wait_rows(xs_ref, TOP_K * td, sem)


def _dispatch(h, pos_tiles, np_rows, td):
    n, d = h.shape
    xs0 = jnp.zeros((np_rows, d), F32)
    return pl.pallas_call(
        functools.partial(_dispatch_kernel, td=td),
        out_shape=jax.ShapeDtypeStruct((np_rows, d), F32),
        grid=(n // td,),
        in_specs=[pl.BlockSpec((1, 1, TOP_K * td), lambda i: (i, 0, 0), memory_space=pltpu.SMEM),
                  pl.BlockSpec((td, d), lambda i: (i, 0)),
                  pl.BlockSpec(memory_space=pl.ANY)],
        out_specs=pl.BlockSpec(memory_space=pl.ANY),
        scratch_shapes=[pltpu.SemaphoreType.DMA(())],
        input_output_aliases={2: 0},
        compiler_params=_cparams(("arbitrary",)),
        name="moe_dispatch",
    )(pos_tiles, h, xs0)


def _combine_kernel(pos_ref, posn_ref, ys_ref, meta_ref, x_ref, mod_ref, g_ref, o_ref, buf_ref, sems, *, td):
    i = pl.program_id(0)
    n = pl.num_programs(0)
    slot = lax.rem(i, 2)

    def gather(p_ref, s):
        def issue(t, c):
            for k in range(TOP_K):
                pltpu.make_async_copy(ys_ref.at[pl.ds(p_ref[0, 0, k * td + t], 1)],
                                      buf_ref.at[s, k, pl.ds(t, 1)], sems.at[s]).start(priority=k)
            return c

        lax.fori_loop(0, td, issue, 0, unroll=ROW_DMA_UNROLL)

    @pl.when(i == 0)
    def _():
        gather(pos_ref, slot)

    @pl.when(i + 1 < n)
    def _():
        gather(posn_ref, 1 - slot)

    _wait_rows(ys_ref, TOP_K * td, sems.at[slot])
    meta = meta_ref[...]
    y = meta[:, 2:3] * buf_ref[slot, 0] + meta[:, 3:4] * buf_ref[slot, 1]
    x = x_ref[...] + mod_ref[0][5:6] * y
    o_ref[...] = _rms(x, g_ref[...])


def _combine(ys, pos_tiles, meta, x_flat, mod, norm_final, tokens_per_batch, td):
    n, d = x_flat.shape
    per_b = tokens_per_batch // td
    nt = n // td
    return pl.pallas_call(
        functools.partial(_combine_kernel, td=td),
        out_shape=jax.ShapeDtypeStruct((n, d), F32),
        grid=(nt,),
        in_specs=[pl.BlockSpec((1, 1, TOP_K * td), lambda i: (i, 0, 0), memory_space=pltpu.SMEM),
                  pl.BlockSpec((1, 1, TOP_K * td), lambda i: (jnp.minimum(i + 1, nt - 1), 0, 0),
                               memory_space=pltpu.SMEM),
                  pl.BlockSpec(memory_space=pl.ANY),
                  pl.BlockSpec((td, LANES), lambda i: (i, 0)),
                  pl.BlockSpec((td, d), lambda i: (i, 0)),
                  pl.BlockSpec((1, MOD_ROWS, d), lambda i: (i // per_b, 0, 0)),
                  pl.BlockSpec((1, d), lambda i: (0, 0))],
        out_specs=pl.BlockSpec((td, d), lambda i: (i, 0)),
        scratch_shapes=[pltpu.VMEM((2, TOP_K, td, d), F32), pltpu.SemaphoreType.DMA((2,))],
        compiler_params=_cparams(("arbitrary",)),
        name="moe_combine",
    )(pos_tiles, pos_tiles, ys, meta, x_flat, mod, norm_final.reshape(1, d))


EXPERT_TILE_ROWS = 1024
EXPERT_TILE_SIGMAS = 3.0
BF16_ROWS = 16


def _expert_tile_rows(n, n_exp):
    share = TOP_K / n_exp
    mean = n * share
    sigma = math.sqrt(n * share * (1.0 - share))
    tiles = max(1, math.ceil(mean / EXPERT_TILE_ROWS))
    rows = math.ceil((mean + EXPERT_TILE_SIGMAS * sigma) / tiles)
    return -(-rows // BF16_ROWS) * BF16_ROWS


def _moe(x, mod, gnorm, router, w1, w3, w2, norm_final, td=256):
    bx, l, d = x.shape
    n = bx * l
    n_exp = router.shape[1]
    tm = _expert_tile_rows(n, n_exp)
    x_flat = x.reshape(n, d)
    h, meta, counts = _route(x_flat, mod, gnorm, router, l)
    idx = meta[:, 0:TOP_K].astype(jnp.int32)
    rank = meta[:, 4:4 + TOP_K].astype(jnp.int32)
    cnt = counts[0, :n_exp].astype(jnp.int32)
    tiles_e = (cnt + tm - 1) // tm
    tile_end = jnp.cumsum(tiles_e)
    start_rows = (tile_end - tiles_e) * tm
    sel = idx[..., None] == jnp.arange(n_exp, dtype=jnp.int32)
    pos = jnp.sum(jnp.where(sel, start_rows, 0), axis=-1) + rank
    n_tiles = -(-(n * TOP_K) // tm) + n_exp
    np_rows = n_tiles * tm
    tile_ids = jnp.arange(n_tiles, dtype=jnp.int32)
    used = tile_end[-1]
    tile_expert = jnp.sum((jnp.minimum(tile_ids, used - 1)[:, None] >= tile_end[None, :]).astype(jnp.int32), axis=1)
    tile_expert = jnp.minimum(tile_expert, n_exp - 1)
    tile_valid = (tile_ids < used).astype(jnp.int32)
    pos_tiles = pos.reshape(n // td, td, TOP_K).transpose(0, 2, 1).reshape(n // td, 1, TOP_K * td)
    xs = _dispatch(h, pos_tiles, np_rows, td)
    ys = _ffn_experts(xs, tile_expert, tile_valid, w1, w3, w2, tm)
    out = _combine(ys, pos_tiles, meta, x_flat, mod, norm_final, l, td)
    return out.reshape(bx, l, d)


def _mod_rows(m):
    r, n = m.shape
    return jnp.pad(m.reshape(r, N_MOD, n // N_MOD), ((0, 0), (0, MOD_ROWS - N_MOD), (0, 0)))


def _hyena_layer(x, mod, gnorm, in_w, in_b, sc_w, sc_b, spectra, f_bias, out_w, out_b, ct_pref):
    z = _norm_mod_matmul(x, mod, gnorm, in_w, in_b, 0, 1, _pick(x.shape[2], ct_pref))
    g = _hyena_conv(z, sc_w, sc_b, spectra, f_bias)
    return _matmul_gated_residual(g, out_w, out_b, x, mod, 2)


@jax.jit
def kernel(x, c, ctx, c_ctx, ada_w, ada_b, norm_mix, norm_ffn, hy_in_w, hy_in_b, hy_sc_w, hy_sc_b, hy_f_w0, hy_f_b0, hy_f_wi, hy_f_bi, hy_f_freq, hy_f_wout, hy_f_bias, hy_out_w, hy_out_b, mla_wq_a, mla_q_norm, mla_wq_b, mla_wkv_a, mla_kv_norm, mla_wkv_b, mla_wo, ffn_w1, ffn_w3, ffn_w2, moe_router, moe_w1, moe_w3, moe_w2, norm_final):
    bsz, l, d = x.shape
    n_ctx = ctx.shape[1]
    depth = ada_w.shape[0]
    assert depth == 2, "layer 0 = Hyena + dense SwiGLU, layer 1 = MLA + expert SwiGLU"

    rows = -(-(bsz + 1) // 8) * 8
    cvec = jnp.zeros((rows, d), F32).at[:bsz].set(c).at[bsz].set(c_ctx)
    mods = _ada_mod(cvec, ada_w, ada_b)
    modx = [_mod_rows(mods[i, :bsz]) for i in range(depth)]
    modc = [_mod_rows(mods[i, bsz:bsz + 1]) for i in range(depth)]

    in_w = hy_in_w[0].astype(BF16)
    out_w = hy_out_w[0].astype(BF16)
    fargs = (hy_f_w0[0], hy_f_b0[0], hy_f_wi[0], hy_f_bi[0], hy_f_freq[0], hy_f_wout[0])
    kx = _hyena_filter_spectra(l, d, *fargs)
    kc = _hyena_filter_spectra(n_ctx, d, *fargs)
    x = _hyena_layer(x, modx[0], norm_mix[0], in_w, hy_in_b[0], hy_sc_w[0], hy_sc_b[0], kx,
                     hy_f_bias[0], out_w, hy_out_b[0], 256)
    ctx = _hyena_layer(ctx, modc[0], norm_mix[0], in_w, hy_in_b[0], hy_sc_w[0], hy_sc_b[0], kc,
                       hy_f_bias[0], out_w, hy_out_b[0], 1024)
    x = _ffn_dense(x, modx[0], norm_ffn[0], ffn_w1[0], ffn_w3[0], ffn_w2[0], 3, 4, 5)
    ctx = _ffn_dense(ctx.reshape(1, bsz * n_ctx, d), modc[0], norm_ffn[0], ffn_w1[0], ffn_w3[0], ffn_w2[0],
                     3, 4, 5).reshape(bsz, n_ctx, d)

    wa, wqb, wk, wv = _mla_weights(mla_wq_a[0], mla_wq_b[0], mla_wkv_a[0], mla_wkv_b[0])
    q_lora = mla_q_norm.shape[1]
    q, kx, vx = _mla_latent(x, modx[1], norm_mix[1], wa, mla_q_norm[0], mla_kv_norm[0], wqb, wk, wv,
                            _rope_tables(l))
    kc, vc = _mla_context(ctx, modc[1], norm_mix[1], wa[:, q_lora:], mla_kv_norm[0], wk, wv)
    o = _attention(q.transpose(0, 2, 1), kc, kx, vc.transpose(0, 2, 1), vx.transpose(0, 2, 1))
    x = _matmul_gated_residual(o, mla_wo[0].astype(BF16), jnp.zeros((d,), F32), x, modx[1], 2)
    return _moe(x, modx[1], norm_ffn[1], moe_router[0], moe_w1[0], moe_w3[0], moe_w2[0], norm_final)
```

```python
import functools
import math

import jax
import jax.numpy as jnp
import numpy as np
from jax import lax
from jax.experimental import pallas as pl
from jax.experimental.pallas import tpu as pltpu

F32 = jnp.float32
BF16 = jnp.bfloat16
HIGHEST = lax.Precision.HIGHEST

RMS_EPS = 1e-6
N_MOD = 6
MOD_ROWS = 8
GRID_W = 64
SHORT_CONV = 3
FILTER_BANDS = 8
FILTER_EMB = 1 + 2 * FILTER_BANDS
FILTER_EMB_PAD = 32
DECAY_TARGET = 1e-2
FAST_DECAY_PCT = 0.3
SLOW_DECAY_PCT = 1.5
N_HEADS = 16
QK_NOPE = 64
QK_ROPE = 32
QK_HEAD = QK_NOPE + QK_ROPE
V_HEAD = 64
ROPE_AXIS = QK_ROPE // 2
ROPE_BASE = 10000.0
TOP_K = 2
LANES = 128
HEAD_SLOT = 128
VMEM_LIMIT = 56 * 1024 * 1024


def _cparams(sem, vmem=VMEM_LIMIT):
    return pltpu.CompilerParams(dimension_semantics=sem, vmem_limit_bytes=vmem)


def _rms(x, g):
    return x * lax.rsqrt(jnp.mean(x * x, axis=-1, keepdims=True) + RMS_EPS) * g


def _silu(x):
    return x * (1.0 / (1.0 + jnp.exp(-x)))


def _store_token_slabs(ref, value):
    rows, width = value.shape
    s = width // LANES
    for j in range(s):
        ref[pl.ds(j, rows, stride=s), :] = value[:, j * LANES:(j + 1) * LANES]


def _load_token_slab(ref, j, rows, s):
    return ref[pl.ds(j, rows, stride=s), :]


def _pick(total, pref):
    t = min(total, pref)
    while total % t:
        t //= 2
    return t


def _ada_kernel(c_ref, w_ref, b_ref, o_ref):
    c = c_ref[...]
    o_ref[0] = jnp.dot(_silu(c), w_ref[0], precision=HIGHEST, preferred_element_type=F32) + b_ref[0]


def _ada_mod(cvec, ada_w, ada_b):
    depth, d, n = ada_w.shape
    r = cvec.shape[0]
    tn = _pick(n, 1536)
    return pl.pallas_call(
        _ada_kernel,
        out_shape=jax.ShapeDtypeStruct((depth, r, n), F32),
        grid=(depth, n // tn),
        in_specs=[pl.BlockSpec((r, d), lambda i, j: (0, 0)),
                  pl.BlockSpec((1, d, tn), lambda i, j: (i, 0, j)),
                  pl.BlockSpec((1, 1, tn), lambda i, j: (i, 0, j))],
        out_specs=pl.BlockSpec((1, r, tn), lambda i, j: (i, 0, j)),
        compiler_params=_cparams(("arbitrary", "arbitrary")),
        name="ada_mod",
    )(cvec, ada_w, ada_b.reshape(depth, 1, n))


def _nmm_kernel(x_ref, mod_ref, g_ref, w_ref, b_ref, o_ref, *, sh_row, sc_row, ct):
    x = x_ref[0]
    mod = mod_ref[0]
    h = _rms(x, g_ref[...]) * (1.0 + mod[sc_row:sc_row + 1]) + mod[sh_row:sh_row + 1]
    y = jnp.dot(h.astype(BF16), w_ref[...], preferred_element_type=F32) + b_ref[...]
    for j in range(o_ref.shape[0]):
        o_ref[j, 0] = y[:, j * ct:(j + 1) * ct]


def _norm_mod_matmul(x, mod, gnorm, w_bf16, bias, sh_row, sc_row, ct, tm_pref=512):
    bx, l, d = x.shape
    n = w_bf16.shape[1]
    tm = _pick(l, tm_pref)
    per_b = mod.shape[0] > 1
    return pl.pallas_call(
        functools.partial(_nmm_kernel, sh_row=sh_row, sc_row=sc_row, ct=ct),
        out_shape=jax.ShapeDtypeStruct((n // ct, bx, l, ct), F32),
        grid=(bx, l // tm),
        in_specs=[pl.BlockSpec((1, tm, d), lambda b, i: (b, i, 0)),
                  pl.BlockSpec((1, MOD_ROWS, d), (lambda b, i: (b, 0, 0)) if per_b else (lambda b, i: (0, 0, 0))),
                  pl.BlockSpec((1, d), lambda b, i: (0, 0)),
                  pl.BlockSpec((d, n), lambda b, i: (0, 0)),
                  pl.BlockSpec((1, n), lambda b, i: (0, 0))],
        out_specs=pl.BlockSpec((n // ct, 1, tm, ct), lambda b, i: (0, b, i, 0)),
        compiler_params=_cparams(("arbitrary", "arbitrary")),
        name="norm_mod_matmul",
    )(x, mod, gnorm.reshape(1, d), w_bf16, bias.reshape(1, n))


def _mm_res_kernel(a_ref, w_ref, b_ref, r_ref, mod_ref, o_ref, *, gate_row):
    y = jnp.dot(a_ref[0], w_ref[...], preferred_element_type=F32) + b_ref[...]
    o_ref[0] = r_ref[0] + mod_ref[0][gate_row:gate_row + 1] * y


def _matmul_gated_residual(a_bf16, w_bf16, bias, resid, mod, gate_row, tm_pref=512):
    bx, l, k = a_bf16.shape
    d = w_bf16.shape[1]
    tm = _pick(l, tm_pref)
    per_b = mod.shape[0] > 1
    return pl.pallas_call(
        functools.partial(_mm_res_kernel, gate_row=gate_row),
        out_shape=jax.ShapeDtypeStruct((bx, l, d), F32),
        grid=(bx, l // tm),
        in_specs=[pl.BlockSpec((1, tm, k), lambda b, i: (b, i, 0)),
                  pl.BlockSpec((k, d), lambda b, i: (0, 0)),
                  pl.BlockSpec((1, d), lambda b, i: (0, 0)),
                  pl.BlockSpec((1, tm, d), lambda b, i: (b, i, 0)),
                  pl.BlockSpec((1, MOD_ROWS, d), (lambda b, i: (b, 0, 0)) if per_b else (lambda b, i: (0, 0, 0)))],
        out_specs=pl.BlockSpec((1, tm, d), lambda b, i: (b, i, 0)),
        compiler_params=_cparams(("arbitrary", "arbitrary")),
        name="matmul_gated_residual",
    )(a_bf16, w_bf16, bias.reshape(1, d), resid, mod)


def _filter_kernel(z_ref, w0_ref, b0_ref, wi_ref, bi_ref, fr_ref, wt_ref, wb_ref, dl_ref, fwd_ref, o_ref, h_ref,
                   *, l, p):
    @pl.when(pl.program_id(0) == 0)
    def _():
        fr = fr_ref[...]
        h = jnp.sin(fr * (jnp.dot(z_ref[...], w0_ref[...], precision=HIGHEST, preferred_element_type=F32)
                          + b0_ref[...]))
        for n in range(wi_ref.shape[0]):
            h = jnp.sin(fr * (jnp.dot(h, wi_ref[n], precision=HIGHEST, preferred_element_type=F32) + bi_ref[n]))
        h_ref[...] = h

    top = jnp.dot(h_ref[:l], wt_ref[...], precision=HIGHEST, preferred_element_type=F32)
    bot = jnp.dot(h_ref[l:], wb_ref[...], precision=HIGHEST, preferred_element_type=F32)
    t = z_ref[:, 0:1]
    rows = lax.broadcasted_iota(jnp.int32, (2 * l, 1), 0)
    decay = jnp.where(rows == l, 0.0, jnp.exp(-t * dl_ref[...]))
    k = jnp.concatenate([top, bot], axis=0) * decay
    k = k / jnp.sum(jnp.abs(k), axis=0, keepdims=True)
    nb = l // p
    for di in range(2 * nb - 1):
        start = (p * (di - nb)) % (2 * l)
        if start + 2 * p <= 2 * l:
            seg = k[start:start + 2 * p]
        else:
            seg = jnp.concatenate([k[start:], k[:start + 2 * p - 2 * l]], axis=0)
        o_ref[di] = jnp.dot(fwd_ref[...], seg.astype(BF16), preferred_element_type=F32)


def _conv_block(l, p_pref=512):
    return min(p_pref, l)


def _hyena_filter_spectra(l, d, w0, b0, wi, bi, freq, wout):
    p = _conv_block(l)
    nd = 2 * (l // p) - 1
    pos = jnp.arange(l, dtype=F32)
    t = (pos / max(l - 1, 1))[:, None]
    w = 2.0 * math.pi * pos / l
    f = jnp.linspace(1e-4, FILTER_BANDS - 1, FILTER_BANDS, dtype=F32)
    ang = w[:, None] * f[None, :]
    z = jnp.concatenate([t, jnp.cos(ang), -jnp.sin(ang)], axis=-1)
    deltas = jnp.abs(jnp.linspace(math.log(DECAY_TARGET) / SLOW_DECAY_PCT,
                                  math.log(DECAY_TARGET) / FAST_DECAY_PCT, d, dtype=F32))
    idx = np.concatenate([np.arange(l), [0], np.arange(l - 1, 0, -1)])
    zc = jnp.pad(z[idx], ((0, 0), (0, FILTER_EMB_PAD - FILTER_EMB)))
    w0p = jnp.pad(w0, ((0, FILTER_EMB_PAD - FILTER_EMB), (0, 0)))
    hid = w0.shape[1]
    n_in = wi.shape[0]
    ct = _pick(d, 256)
    nct = d // ct
    fwd = jnp.asarray(_dft_mats(p)[0]).astype(BF16)
    return pl.pallas_call(
        functools.partial(_filter_kernel, l=l, p=p),
        out_shape=jax.ShapeDtypeStruct((nd, 2 * p, d), F32),
        grid=(nct,),
        in_specs=[pl.BlockSpec((2 * l, FILTER_EMB_PAD), lambda j: (0, 0)),
                  pl.BlockSpec((FILTER_EMB_PAD, hid), lambda j: (0, 0)),
                  pl.BlockSpec((1, hid), lambda j: (0, 0)),
                  pl.BlockSpec((n_in, hid, hid), lambda j: (0, 0, 0)),
                  pl.BlockSpec((n_in, 1, hid), lambda j: (0, 0, 0)),
                  pl.BlockSpec((1, hid), lambda j: (0, 0)),
                  pl.BlockSpec((hid, ct), lambda j: (0, j)),
                  pl.BlockSpec((hid, ct), lambda j: (0, nct + j)),
                  pl.BlockSpec((1, ct), lambda j: (0, j)),
                  pl.BlockSpec((2 * p, 2 * p), lambda j: (0, 0))],
        out_specs=pl.BlockSpec((nd, 2 * p, ct), lambda j: (0, 0, j)),
        scratch_shapes=[pltpu.VMEM((2 * l, hid), F32)],
        compiler_params=_cparams(("arbitrary",)),
        name="hyena_filter",
    )(zc, w0p, b0.reshape(1, hid), wi, bi.reshape(n_in, 1, hid), freq.reshape(1, hid), wout, wout,
      deltas.reshape(1, d), fwd)


@functools.lru_cache(maxsize=None)
def _dft_mats(p):
    n = 2 * p
    f = np.arange(p)[:, None]
    t = np.arange(n)[None, :]
    ang = 2.0 * np.pi * (((2 * f + 1) * t) % (4 * p)) / (4 * p)
    fwd = np.concatenate([np.cos(ang), -np.sin(ang)], axis=0)
    q = np.arange(p)[:, None]
    ff = np.arange(p)[None, :]
    ang2 = 2.0 * np.pi * (((2 * ff + 1) * (q + p)) % (4 * p)) / (4 * p)
    inv = np.concatenate([np.cos(ang2), -np.sin(ang2)], axis=1) / p
    return fwd.astype(np.float32), inv.astype(np.float32)


def _hyena_conv_kernel(z0_ref, z1_ref, z2_ref, w0_ref, w1_ref, w2_ref, b0_ref, b1_ref, b2_ref,
                       ks_ref, fb_ref, fwd_ref, inv_ref, o_ref, vs_ref, ys_ref, *, l, p, rc):
    nb = l // p
    rows = lax.broadcasted_iota(jnp.int32, (l, z0_ref.shape[3]), 0)

    def sconv(z_ref, w_ref, b_ref):
        z = z_ref[0, 0]
        w = w_ref[...]
        zm = jnp.where(rows == 0, 0.0, pltpu.roll(z, 1, 0))
        zp = jnp.where(rows == l - 1, 0.0, pltpu.roll(z, l - 1, 0))
        return zm * w[0:1] + z * w[1:2] + zp * w[2:3] + b_ref[...]

    u = sconv(z2_ref, w2_ref, b2_ref) * sconv(z1_ref, w1_ref, b1_ref)
    ub = u.astype(BF16)
    for j in range(nb):
        vs_ref[j] = jnp.dot(fwd_ref[...], ub[j * p:(j + 1) * p], preferred_element_type=F32)

    def chunk(c, carry):
        r0 = pl.multiple_of(c * rc, rc)
        re = pl.ds(r0, rc)
        im = pl.ds(p + r0, rc)
        for i in range(nb):
            yr = None
            yi = None
            for j in range(nb):
                di = i - j + nb - 1
                kr = ks_ref[di, re, :]
                ki = ks_ref[di, im, :]
                vr = vs_ref[j, re, :]
                vi = vs_ref[j, im, :]
                tr = kr * vr - ki * vi
                ti = kr * vi + ki * vr
                yr = tr if yr is None else yr + tr
                yi = ti if yi is None else yi + ti
            ys_ref[i, re, :] = yr
            ys_ref[i, im, :] = yi
        return carry

    lax.fori_loop(0, p // rc, chunk, 0)

    x0 = sconv(z0_ref, w0_ref, b0_ref)
    fb = fb_ref[...]
    for i in range(nb):
        y = jnp.dot(inv_ref[...], ys_ref[i].astype(BF16), preferred_element_type=F32)
        sl = slice(i * p, (i + 1) * p)
        o_ref[0, sl, :] = (x0[sl] * (y + u[sl] * fb)).astype(BF16)


def _hyena_conv(z, sc_w, sc_b, spectra, f_bias):
    nz, bx, l, ct = z.shape
    nct = nz // 3
    d = nct * ct
    d3 = 3 * d
    nd, p2, _ = spectra.shape
    p = p2 // 2
    nb = l // p
    fwd, inv = _dft_mats(p)
    fwd = jnp.asarray(fwd[:, :p]).astype(BF16)
    inv = jnp.asarray(inv).astype(BF16)
    zspec = lambda part: pl.BlockSpec((1, 1, l, ct), lambda j, b, part=part: (part * nct + j, b, 0, 0))
    wspec = lambda part: pl.BlockSpec((SHORT_CONV, ct), lambda j, b, part=part: (0, part * nct + j))
    bspec = lambda part: pl.BlockSpec((1, ct), lambda j, b, part=part: (0, part * nct + j))
    once = pl.Buffered(1)
    scb = sc_b.reshape(1, d3)
    return pl.pallas_call(
        functools.partial(_hyena_conv_kernel, l=l, p=p, rc=8),
        out_shape=jax.ShapeDtypeStruct((bx, l, d), BF16),
        grid=(nct, bx),
        in_specs=[zspec(0), zspec(1), zspec(2), wspec(0), wspec(1), wspec(2), bspec(0), bspec(1), bspec(2),
                  pl.BlockSpec((nd, 2 * p, ct), lambda j, b: (0, 0, j), pipeline_mode=once),
                  pl.BlockSpec((1, ct), lambda j, b: (0, j)),
                  pl.BlockSpec((2 * p, p), lambda j, b: (0, 0), pipeline_mode=once),
                  pl.BlockSpec((p, 2 * p), lambda j, b: (0, 0), pipeline_mode=once)],
        out_specs=pl.BlockSpec((1, l, ct), lambda j, b: (b, 0, j)),
        scratch_shapes=[pltpu.VMEM((nb, 2 * p, ct), F32),
                        pltpu.VMEM((nb, 2 * p, ct), F32)],
        compiler_params=_cparams(("arbitrary", "arbitrary")),
        name="hyena_conv",
    )(z, z, z, sc_w, sc_w, sc_w, scb, scb, scb, spectra, f_bias.reshape(1, d), fwd, inv)


def _swiglu_step(h_bf16, w1_ref, w3_ref, w2_ref, acc_ref):
    a = jnp.dot(h_bf16, w1_ref[0].astype(BF16), preferred_element_type=F32)
    b = jnp.dot(h_bf16, w3_ref[0].astype(BF16), preferred_element_type=F32)
    g = (_silu(a) * b).astype(BF16)
    acc_ref[...] += jnp.dot(g, w2_ref[0].astype(BF16), preferred_element_type=F32)


def _ffn_dense_kernel(x_ref, mod_ref, g_ref, w1_ref, w3_ref, w2_ref, o_ref, h_ref, acc_ref,
                      *, sh_row, sc_row, gate_row):
    f = pl.program_id(2)

    @pl.when(f == 0)
    def _():
        mod = mod_ref[0]
        h = _rms(x_ref[0], g_ref[...]) * (1.0 + mod[sc_row:sc_row + 1]) + mod[sh_row:sh_row + 1]
        h_ref[...] = h.astype(BF16)
        acc_ref[...] = jnp.zeros_like(acc_ref)

    _swiglu_step(h_ref[...], w1_ref, w3_ref, w2_ref, acc_ref)

    @pl.when(f == pl.num_programs(2) - 1)
    def _():
        o_ref[0] = x_ref[0] + mod_ref[0][gate_row:gate_row + 1] * acc_ref[...]


def _ffn_dense(x, mod, gnorm, w1, w3, w2, sh_row, sc_row, gate_row, tm_pref=1024, tf_pref=512):
    bx, l, d = x.shape
    ff = w1.shape[1]
    tm = _pick(l, tm_pref)
    tf = _pick(ff, tf_pref)
    per_b = mod.shape[0] > 1
    return pl.pallas_call(
        functools.partial(_ffn_dense_kernel, sh_row=sh_row, sc_row=sc_row, gate_row=gate_row),
        out_shape=jax.ShapeDtypeStruct((bx, l, d), F32),
        grid=(bx, l // tm, ff // tf),
        in_specs=[pl.BlockSpec((1, tm, d), lambda b, i, f: (b, i, 0)),
                  pl.BlockSpec((1, MOD_ROWS, d), (lambda b, i, f: (b, 0, 0)) if per_b else (lambda b, i, f: (0, 0, 0))),
                  pl.BlockSpec((1, d), lambda b, i, f: (0, 0)),
                  pl.BlockSpec((1, d, tf), lambda b, i, f: (0, 0, f)),
                  pl.BlockSpec((1, d, tf), lambda b, i, f: (0, 0, f)),
                  pl.BlockSpec((1, tf, d), lambda b, i, f: (0, f, 0))],
        out_specs=pl.BlockSpec((1, tm, d), lambda b, i, f: (b, i, 0)),
        scratch_shapes=[pltpu.VMEM((tm, d), BF16), pltpu.VMEM((tm, d), F32)],
        compiler_params=_cparams(("arbitrary", "arbitrary", "arbitrary")),
        name="ffn_dense",
    )(x, mod, gnorm.reshape(1, d), w1[None], w3[None], w2[None])


def _ffn_expert_kernel(te_ref, tv_ref, x_ref, w1_ref, w3_ref, w2_ref, o_ref, h_ref, acc_ref):
    i = pl.program_id(0)
    f = pl.program_id(1)

    @pl.when(tv_ref[i] > 0)
    def _():
        @pl.when(f == 0)
        def _():
            tm, d = h_ref.shape
            s = d // LANES
            for j in range(s):
                h_ref[:, j * LANES:(j + 1) * LANES] = _load_token_slab(x_ref, j, tm, s).astype(BF16)
            acc_ref[...] = jnp.zeros_like(acc_ref)

        _swiglu_step(h_ref[...], w1_ref, w3_ref, w2_ref, acc_ref)

        @pl.when(f == pl.num_programs(1) - 1)
        def _():
            _store_token_slabs(o_ref, acc_ref[...])

    @pl.when((tv_ref[i] == 0) & (f == 0))
    def _():
        o_ref[...] = jnp.zeros_like(o_ref)


def _ffn_experts(xs, tile_expert, tile_valid, w1, w3, w2, tm, tf_pref=512):
    d = w1.shape[1]
    s = d // LANES
    np_rows = xs.shape[0] // s
    ff = w1.shape[2]
    tf = _pick(ff, tf_pref)
    nf = ff // tf
    fsel = lambda i, f, tv: jnp.where(tv[i] > 0, f, nf - 1)
    grid_spec = pltpu.PrefetchScalarGridSpec(
        num_scalar_prefetch=2,
        grid=(np_rows // tm, nf),
        in_specs=[pl.BlockSpec((tm * s, LANES), lambda i, f, te, tv: (i, 0)),
                  pl.BlockSpec((1, d, tf), lambda i, f, te, tv: (te[i], 0, fsel(i, f, tv))),
                  pl.BlockSpec((1, d, tf), lambda i, f, te, tv: (te[i], 0, fsel(i, f, tv))),
                  pl.BlockSpec((1, tf, d), lambda i, f, te, tv: (te[i], fsel(i, f, tv), 0))],
        out_specs=pl.BlockSpec((tm * s, LANES), lambda i, f, te, tv: (i, 0)),
        scratch_shapes=[pltpu.VMEM((tm, d), BF16), pltpu.VMEM((tm, d), F32)],
    )
    return pl.pallas_call(
        _ffn_expert_kernel,
        out_shape=jax.ShapeDtypeStruct((np_rows * s, LANES), F32),
        grid_spec=grid_spec,
        compiler_params=_cparams(("arbitrary", "arbitrary")),
        name="ffn_experts",
    )(tile_expert, tile_valid, xs, w1, w3, w2)


def _mla_latent_kernel(x_ref, mod_ref, g_ref, wa_ref, qg_ref, kvg_ref, wqb_ref, wk_ref, wv_ref, tab_ref,
                       q_ref, k_ref, v_ref, *, q_lora, kv_lora):
    mod = mod_ref[0]
    h = _rms(x_ref[0], g_ref[...]) * (1.0 + mod[1:2]) + mod[0:1]
    a = jnp.dot(h.astype(BF16), wa_ref[...], preferred_element_type=F32)
    qn = _rms(a[:, :q_lora], qg_ref[...]).astype(BF16)
    cn = _rms(a[:, q_lora:q_lora + kv_lora], kvg_ref[...]).astype(BF16)
    kpe = a[:, q_lora + kv_lora:]
    tab = tab_ref[...]
    cq, s1q, s2q, ck, s1k, s2k = (tab[:, n * LANES:(n + 1) * LANES] for n in range(6))
    q = jnp.dot(qn, wqb_ref[...], preferred_element_type=F32)
    wq = q.shape[1]
    rep = wq // LANES
    q = (q * jnp.tile(cq, (1, rep)) + pltpu.roll(q, ROPE_AXIS, 1) * jnp.tile(s1q, (1, rep))
         + pltpu.roll(q, wq - ROPE_AXIS, 1) * jnp.tile(s2q, (1, rep)))
    q_ref[0] = q.astype(BF16)
    kr = kpe * ck + pltpu.roll(kpe, ROPE_AXIS, 1) * s1k + pltpu.roll(kpe, LANES - ROPE_AXIS, 1) * s2k
    k = jnp.dot(cn, wk_ref[...], preferred_element_type=F32) + jnp.tile(kr, (1, rep))
    k_ref[0] = k.astype(BF16)
    v_ref[0] = jnp.dot(cn, wv_ref[...], preferred_element_type=F32).astype(BF16)


def _mla_context_kernel(x_ref, mod_ref, g_ref, wa_ref, kvg_ref, wk_ref, wv_ref, k_ref, v_ref, *, kv_lora):
    mod = mod_ref[0]
    h = _rms(x_ref[0], g_ref[...]) * (1.0 + mod[1:2]) + mod[0:1]
    a = jnp.dot(h.astype(BF16), wa_ref[...], preferred_element_type=F32)
    cn = _rms(a[:, :kv_lora], kvg_ref[...]).astype(BF16)
    kpe = a[:, kv_lora:]
    rep = wk_ref.shape[1] // LANES
    k = jnp.dot(cn, wk_ref[...], preferred_element_type=F32) + jnp.tile(kpe, (1, rep))
    k_ref[0] = k.astype(BF16)
    v_ref[0] = jnp.dot(cn, wv_ref[...], preferred_element_type=F32).astype(BF16)


def _rope_tables(l):
    rows = l // GRID_W
    row = jnp.broadcast_to(jnp.arange(rows, dtype=F32)[:, None], (rows, GRID_W)).reshape(l)
    col = jnp.broadcast_to(jnp.arange(GRID_W, dtype=F32)[None, :], (rows, GRID_W)).reshape(l)
    inv = ROPE_BASE ** (-jnp.arange(0, ROPE_AXIS, 2, dtype=F32) / ROPE_AXIS)
    ang = jnp.concatenate([row[:, None] * inv, col[:, None] * inv], axis=-1)
    cos = jnp.cos(ang)
    sin = jnp.sin(ang)
    n = l
    ones = jnp.ones((n, QK_NOPE), F32)
    z16 = jnp.zeros((n, ROPE_AXIS), F32)
    z64 = jnp.zeros((n, QK_NOPE), F32)
    zpad = jnp.zeros((n, HEAD_SLOT - QK_HEAD), F32)
    c = jnp.concatenate([ones, cos, cos, zpad], axis=1)
    s1 = jnp.concatenate([z64, z16, sin, zpad], axis=1)
    s2 = jnp.concatenate([z64, -sin, z16, zpad], axis=1)
    scale = math.log2(math.e) / math.sqrt(QK_HEAD)
    return jnp.concatenate([c * scale, s1 * scale, s2 * scale, c, s1, s2], axis=1)


def _mla_weights(wq_a, wq_b, wkv_a, wkv_b):
    d, q_lora = wq_a.shape
    kv_lora = wkv_a.shape[1] - QK_ROPE
    ev = np.arange(0, QK_ROPE, 2)
    od = np.arange(1, QK_ROPE, 2)
    kpe = wkv_a[:, kv_lora:]
    kpe_slot = jnp.concatenate([jnp.zeros((d, QK_NOPE), F32), kpe[:, ev], kpe[:, od],
                                jnp.zeros((d, HEAD_SLOT - QK_HEAD), F32)], axis=1)
    wa = jnp.concatenate([wq_a, wkv_a[:, :kv_lora], kpe_slot], axis=1)
    qb = wq_b.reshape(q_lora, N_HEADS, QK_HEAD)
    qb = jnp.concatenate([qb[..., :QK_NOPE], qb[..., QK_NOPE + ev], qb[..., QK_NOPE + od],
                          jnp.zeros((q_lora, N_HEADS, HEAD_SLOT - QK_HEAD), F32)], axis=-1)
    kvb = wkv_b.reshape(kv_lora, N_HEADS, QK_NOPE + V_HEAD)
    wk = jnp.concatenate([kvb[..., :QK_NOPE], jnp.zeros((kv_lora, N_HEADS, HEAD_SLOT - QK_NOPE), F32)], axis=-1)
    wv = kvb[..., QK_NOPE:]
    return (wa.astype(BF16), qb.reshape(q_lora, N_HEADS * HEAD_SLOT).astype(BF16),
            wk.reshape(kv_lora, N_HEADS * HEAD_SLOT).astype(BF16),
            wv.reshape(kv_lora, N_HEADS * V_HEAD).astype(BF16))


def _mod_spec(mod):
    per_b = mod.shape[0] > 1
    return pl.BlockSpec((1, MOD_ROWS, mod.shape[2]), (lambda b, i: (b, 0, 0)) if per_b else (lambda b, i: (0, 0, 0)))


def _mla_latent(x, mod, gnorm, wa, q_norm, kv_norm, wqb, wk, wv, tables, tm_pref=512):
    bx, n, d = x.shape
    tm = _pick(n, tm_pref)
    q_lora = q_norm.shape[0]
    kv_lora = kv_norm.shape[0]
    wq = wqb.shape[1]
    wvn = wv.shape[1]
    const = lambda b, i: (0, 0)
    row = lambda width: pl.BlockSpec((1, tm, width), lambda b, i: (b, i, 0))
    return pl.pallas_call(
        functools.partial(_mla_latent_kernel, q_lora=q_lora, kv_lora=kv_lora),
        out_shape=(jax.ShapeDtypeStruct((bx, n, wq), BF16),
                   jax.ShapeDtypeStruct((bx, n, wq), BF16),
                   jax.ShapeDtypeStruct((bx, n, wvn), BF16)),
        grid=(bx, n // tm),
        in_specs=[row(d), _mod_spec(mod),
                  pl.BlockSpec((1, d), const),
                  pl.BlockSpec(wa.shape, const),
                  pl.BlockSpec((1, q_lora), const),
                  pl.BlockSpec((1, kv_lora), const),
                  pl.BlockSpec(wqb.shape, const),
                  pl.BlockSpec(wk.shape, const),
                  pl.BlockSpec(wv.shape, const),
                  pl.BlockSpec((tm, 6 * LANES), lambda b, i: (i, 0))],
        out_specs=(row(wq), row(wq), row(wvn)),
        compiler_params=_cparams(("arbitrary", "arbitrary")),
        name="mla_latent_proj",
    )(x, mod, gnorm.reshape(1, d), wa, q_norm.reshape(1, q_lora), kv_norm.reshape(1, kv_lora),
      wqb, wk, wv, tables)


def _mla_context(ctx, mod, gnorm, wa_kv, kv_norm, wk, wv, tm_pref=512):
    bx, n, d = ctx.shape
    tm = _pick(n, tm_pref)
    kv_lora = kv_norm.shape[0]
    wkn = wk.shape[1]
    wvn = wv.shape[1]
    const = lambda b, i: (0, 0)
    row = lambda width: pl.BlockSpec((1, tm, width), lambda b, i: (b, i, 0))
    return pl.pallas_call(
        functools.partial(_mla_context_kernel, kv_lora=kv_lora),
        out_shape=(jax.ShapeDtypeStruct((bx, n, wkn), BF16),
                   jax.ShapeDtypeStruct((bx, n, wvn), BF16)),
        grid=(bx, n // tm),
        in_specs=[row(d), _mod_spec(mod),
                  pl.BlockSpec((1, d), const),
                  pl.BlockSpec(wa_kv.shape, const),
                  pl.BlockSpec((1, kv_lora), const),
                  pl.BlockSpec(wk.shape, const),
                  pl.BlockSpec(wv.shape, const)],
        out_specs=(row(wkn), row(wvn)),
        compiler_params=_cparams(("arbitrary", "arbitrary")),
        name="mla_context_proj",
    )(ctx, mod, gnorm.reshape(1, d), wa_kv, kv_norm.reshape(1, kv_lora), wk, wv)


ATTN_HEADS_PER_STEP = 8
ATTN_KEY_CHUNK = 256
ATTN_SOFTMAX_LAG = 2
ATTN_PV_LAG = 6
SUM_ROWS = 16


def _attn_kernel(qt_ref, kc_ref, kx_ref, vct_ref, vxt_ref, o_ref, *, heads, ck):
    chunks = ([(kc_ref, vct_ref, c) for c in range(kc_ref.shape[1] // ck)]
              + [(kx_ref, vxt_ref, c) for c in range(kx_ref.shape[1] // ck)])
    items = [(h, ci) for ci in range(len(chunks)) for h in range(heads)]
    state = {h: None for h in range(heads)}
    scores = {}
    probs = {}
    ones = jnp.ones((SUM_ROWS, ck), BF16)

    def qk(t):
        h, ci = items[t]
        kref, _, c = chunks[ci]
        hsl = slice(h * HEAD_SLOT, (h + 1) * HEAD_SLOT)
        scores[t] = jnp.dot(kref[0, c * ck:(c + 1) * ck, hsl], qt_ref[0, hsl, :], preferred_element_type=F32)

    def softmax(t):
        h, _ = items[t]
        s = scores.pop(t)
        mc = jnp.max(s, axis=0, keepdims=True)
        if state[h] is None:
            probs[t] = (jnp.exp2(s - mc).astype(BF16), None)
            state[h] = (mc, None)
        else:
            m, acc = state[h]
            m_new = jnp.maximum(m, mc)
            probs[t] = (jnp.exp2(s - m_new).astype(BF16), jnp.exp2(m - m_new))
            state[h] = (m_new, acc)

    def pv(t):
        h, ci = items[t]
        _, vref, c = chunks[ci]
        p, alpha = probs.pop(t)
        lhs = jnp.concatenate([vref[0, h * V_HEAD:(h + 1) * V_HEAD, c * ck:(c + 1) * ck], ones], axis=0)
        o = jnp.dot(lhs, p, preferred_element_type=F32)
        m, acc = state[h]
        state[h] = (m, o if acc is None else alpha * acc + o)

    n_items = len(items)
    for t in range(n_items + ATTN_PV_LAG):
        if t < n_items:
            qk(t)
        if ATTN_SOFTMAX_LAG <= t < n_items + ATTN_SOFTMAX_LAG:
            softmax(t - ATTN_SOFTMAX_LAG)
        if t >= ATTN_PV_LAG:
            pv(t - ATTN_PV_LAG)

    for hp in range(heads // 2):
        outs = [state[h][1][:V_HEAD] / state[h][1][V_HEAD:V_HEAD + 1] for h in (2 * hp, 2 * hp + 1)]
        pair = jnp.concatenate(outs, axis=0)
        o_ref[0, :, hp * 2 * V_HEAD:(hp + 1) * 2 * V_HEAD] = pair.T.astype(BF16)


def _attention(qt, kc, kx, vct, vxt, tq_pref=256, heads=ATTN_HEADS_PER_STEP):
    bx, _, l = qt.shape
    n_ctx = kc.shape[1]
    tq = _pick(l, tq_pref)
    ck = _pick(n_ctx, ATTN_KEY_CHUNK)
    qk_w = heads * HEAD_SLOT
    v_w = heads * V_HEAD
    return pl.pallas_call(
        functools.partial(_attn_kernel, heads=heads, ck=ck),
        out_shape=jax.ShapeDtypeStruct((bx, l, N_HEADS * V_HEAD), BF16),
        grid=(bx, N_HEADS // heads, l // tq),
        in_specs=[pl.BlockSpec((1, qk_w, tq), lambda b, g, i: (b, g, i)),
                  pl.BlockSpec((1, n_ctx, qk_w), lambda b, g, i: (b, 0, g)),
                  pl.BlockSpec((1, l, qk_w), lambda b, g, i: (b, 0, g)),
                  pl.BlockSpec((1, v_w, n_ctx), lambda b, g, i: (b, g, 0)),
                  pl.BlockSpec((1, v_w, l), lambda b, g, i: (b, g, 0))],
        out_specs=pl.BlockSpec((1, tq, v_w), lambda b, g, i: (b, i, g)),
        compiler_params=_cparams(("arbitrary", "arbitrary", "arbitrary")),
        name="mla_attention",
    )(qt, kc, kx, vct, vxt)


def _route_kernel(x_ref, mod_ref, g_ref, r_ref, tri_ref, h_ref, meta_ref, cnt_ref, carry_ref, *, n_exp):
    i = pl.program_id(0)

    @pl.when(i == 0)
    def _():
        carry_ref[...] = jnp.zeros_like(carry_ref)

    mod = mod_ref[0]
    h = _rms(x_ref[...], g_ref[...]) * (1.0 + mod[4:5]) + mod[3:4]
    _store_token_slabs(h_ref, h)
    logits = jnp.dot(h, r_ref[...], precision=HIGHEST, preferred_element_type=F32)
    lane = lax.broadcasted_iota(jnp.int32, logits.shape, 1)
    lane_f = lane.astype(F32)
    neg = jnp.float32(-jnp.inf)
    lg = jnp.where(lane < n_exp, logits, neg)
    v1 = jnp.max(lg, axis=-1, keepdims=True)
    i1 = jnp.min(jnp.where(lg == v1, lane_f, float(LANES)), axis=-1, keepdims=True)
    oh1 = lane_f == i1
    lg2 = jnp.where(oh1, neg, lg)
    v2 = jnp.max(lg2, axis=-1, keepdims=True)
    i2 = jnp.min(jnp.where(lg2 == v2, lane_f, float(LANES)), axis=-1, keepdims=True)
    oh2 = lane_f == i2
    e = jnp.exp(v2 - v1)
    g1 = 1.0 / (1.0 + e)
    g2 = e / (1.0 + e)
    oh = jnp.where(oh1 | oh2, 1.0, 0.0)
    pref = jnp.dot(tri_ref[...], oh.astype(BF16), preferred_element_type=F32)
    excl = pref - oh + carry_ref[...]
    r1 = jnp.sum(jnp.where(oh1, excl, 0.0), axis=-1, keepdims=True)
    r2 = jnp.sum(jnp.where(oh2, excl, 0.0), axis=-1, keepdims=True)
    carry_ref[...] += jnp.sum(oh, axis=0, keepdims=True)
    cnt_ref[...] = carry_ref[...]
    meta = jnp.where(lane == 0, i1, 0.0)
    meta = jnp.where(lane == 1, i2, meta)
    meta = jnp.where(lane == 2, g1, meta)
    meta = jnp.where(lane == 3, g2, meta)
    meta = jnp.where(lane == 4, r1, meta)
    meta = jnp.where(lane == 5, r2, meta)
    meta_ref[...] = meta


def _route(x_flat, mod, gnorm, router, tokens_per_batch, tr_pref=512):
    n, d = x_flat.shape
    n_exp = router.shape[1]
    tr = _pick(tokens_per_batch, tr_pref)
    per_b = tokens_per_batch // tr
    rpad = jnp.pad(router, ((0, 0), (0, LANES - n_exp)))
    tri = jnp.asarray(np.tril(np.ones((tr, tr), np.float32))).astype(BF16)
    return pl.pallas_call(
        functools.partial(_route_kernel, n_exp=n_exp),
        out_shape=(jax.ShapeDtypeStruct((n * (d // LANES), LANES), F32),
                   jax.ShapeDtypeStruct((n, LANES), F32),
                   jax.ShapeDtypeStruct((1, LANES), F32)),
        grid=(n // tr,),
        in_specs=[pl.BlockSpec((tr, d), lambda i: (i, 0)),
                  pl.BlockSpec((1, MOD_ROWS, d), lambda i: (i // per_b, 0, 0)),
                  pl.BlockSpec((1, d), lambda i: (0, 0)),
                  pl.BlockSpec((d, LANES), lambda i: (0, 0)),
                  pl.BlockSpec((tr, tr), lambda i: (0, 0))],
        out_specs=(pl.BlockSpec((tr * (d // LANES), LANES), lambda i: (i, 0)),
                   pl.BlockSpec((tr, LANES), lambda i: (i, 0)),
                   pl.BlockSpec((1, LANES), lambda i: (0, 0))),
        scratch_shapes=[pltpu.VMEM((1, LANES), F32)],
        compiler_params=_cparams(("arbitrary",)),
        name="moe_route",
    )(x_flat, mod, gnorm.reshape(1, d), rpad, tri)


ROW_DMA_UNROLL = 8


def _wait_rows(any_ref, rows, sem):
    blk = any_ref.at[pl.ds(0, rows)]
    pltpu.make_async_copy(blk, blk, sem).wait()


def _dispatch_kernel(pos_ref, h_ref, xs_in_ref, xs_ref, sem, *, td, s):
    del xs_in_ref

    def issue(t, c):
        src = h_ref.at[pl.ds(pl.multiple_of(t * s, s), s)]
        for k in range(TOP_K):
            dst = xs_ref.at[pl.ds(pl.multiple_of(pos_ref[0, 0, k * td + t], s), s)]
            pltpu.make_async_copy(src, dst, sem).start(priority=k)
        return c

    lax.fori_loop(0, td, issue, 0, unroll=ROW_DMA_UNROLL)
    _wait_rows(xs_ref, TOP_K * td * s, sem)


def _dispatch(h, pos_tiles, np_rows, td, s):
    n = h.shape[0] // s
    xs0 = jnp.zeros((np_rows * s, LANES), F32)
    return pl.pallas_call(
        functools.partial(_dispatch_kernel, td=td, s=s),
        out_shape=jax.ShapeDtypeStruct((np_rows * s, LANES), F32),
        grid=(n // td,),
        in_specs=[pl.BlockSpec((1, 1, TOP_K * td), lambda i: (i, 0, 0), memory_space=pltpu.SMEM),
                  pl.BlockSpec((td * s, LANES), lambda i: (i, 0)),
                  pl.BlockSpec(memory_space=pl.ANY)],
        out_specs=pl.BlockSpec(memory_space=pl.ANY),
        scratch_shapes=[pltpu.SemaphoreType.DMA(())],
        input_output_aliases={2: 0},
        compiler_params=_cparams(("arbitrary",)),
        name="moe_dispatch",
    )(pos_tiles, h, xs0)


def _combine_kernel(pos_ref, posn_ref, ys_ref, meta_ref, x_ref, mod_ref, g_ref, o_ref, buf_ref, sems, *, td, s):
    i = pl.program_id(0)
    n = pl.num_programs(0)
    slot = lax.rem(i, 2)

    def gather(p_ref, sl):
        def issue(t, c):
            for k in range(TOP_K):
                src = ys_ref.at[pl.ds(pl.multiple_of(p_ref[0, 0, k * td + t], s), s)]
                dst = buf_ref.at[sl, k, pl.ds(pl.multiple_of(t * s, s), s)]
                pltpu.make_async_copy(src, dst, sems.at[sl]).start(priority=k)
            return c

        lax.fori_loop(0, td, issue, 0, unroll=ROW_DMA_UNROLL)

    @pl.when(i == 0)
    def _():
        gather(pos_ref, slot)

    @pl.when(i + 1 < n)
    def _():
        gather(posn_ref, 1 - slot)

    _wait_rows(ys_ref, TOP_K * td * s, sems.at[slot])
    meta = meta_ref[...]
    g0 = meta[:, 2:3]
    g1 = meta[:, 3:4]
    gate = mod_ref[0][5:6]
    xs = []
    ssq = jnp.zeros((td, 1), F32)
    for j in range(s):
        lanes = slice(j * LANES, (j + 1) * LANES)
        y = (g0 * _load_token_slab(buf_ref.at[slot, 0], j, td, s)
             + g1 * _load_token_slab(buf_ref.at[slot, 1], j, td, s))
        xj = x_ref[:, lanes] + gate[:, lanes] * y
        ssq = ssq + jnp.sum(xj * xj, axis=-1, keepdims=True)
        xs.append(xj)
    inv = lax.rsqrt(ssq / (s * LANES) + RMS_EPS)
    for j in range(s):
        lanes = slice(j * LANES, (j + 1) * LANES)
        o_ref[:, lanes] = xs[j] * inv * g_ref[:, lanes]


def _combine(ys, pos_tiles, meta, x_flat, mod, norm_final, tokens_per_batch, td):
    n, d = x_flat.shape
    s = d // LANES
    per_b = tokens_per_batch // td
    nt = n // td
    return pl.pallas_call(
        functools.partial(_combine_kernel, td=td, s=s),
        out_shape=jax.ShapeDtypeStruct((n, d), F32),
        grid=(nt,),
        in_specs=[pl.BlockSpec((1, 1, TOP_K * td), lambda i: (i, 0, 0), memory_space=pltpu.SMEM),
                  pl.BlockSpec((1, 1, TOP_K * td), lambda i: (jnp.minimum(i + 1, nt - 1), 0, 0),
                               memory_space=pltpu.SMEM),
                  pl.BlockSpec(memory_space=pl.ANY),
                  pl.BlockSpec((td, LANES), lambda i: (i, 0)),
                  pl.BlockSpec((td, d), lambda i: (i, 0)),
                  pl.BlockSpec((1, MOD_ROWS, d), lambda i: (i // per_b, 0, 0)),
                  pl.BlockSpec((1, d), lambda i: (0, 0))],
        out_specs=pl.BlockSpec((td, d), lambda i: (i, 0)),
        scratch_shapes=[pltpu.VMEM((2, TOP_K, td * s, LANES), F32), pltpu.SemaphoreType.DMA((2,))],
        compiler_params=_cparams(("arbitrary",)),
        name="moe_combine",
    )(pos_tiles, pos_tiles, ys, meta, x_flat, mod, norm_final.reshape(1, d))


EXPERT_TILE_ROWS = 1024


def _moe(x, mod, gnorm, router, w1, w3, w2, norm_final, td=256):
    bx, l, d = x.shape
    n = bx * l
    n_exp = router.shape[1]
    s = d // LANES
    tm = min(EXPERT_TILE_ROWS, n * TOP_K)
    x_flat = x.reshape(n, d)
    h, meta, counts = _route(x_flat, mod, gnorm, router, l)
    idx = meta[:, 0:TOP_K].astype(jnp.int32)
    rank = meta[:, 4:4 + TOP_K].astype(jnp.int32)
    cnt = counts[0, :n_exp].astype(jnp.int32)
    tiles_e = (cnt + tm - 1) // tm
    tile_end = jnp.cumsum(tiles_e)
    start_rows = (tile_end - tiles_e) * tm
    sel = idx[..., None] == jnp.arange(n_exp, dtype=jnp.int32)
    pos = jnp.sum(jnp.where(sel, start_rows, 0), axis=-1) + rank
    n_tiles = -(-(n * TOP_K) // tm) + n_exp
    np_rows = n_tiles * tm
    tile_ids = jnp.arange(n_tiles, dtype=jnp.int32)
    used = tile_end[-1]
    tile_expert = jnp.sum((jnp.minimum(tile_ids, used - 1)[:, None] >= tile_end[None, :]).astype(jnp.int32), axis=1)
    tile_expert = jnp.minimum(tile_expert, n_exp - 1)
    tile_valid = (tile_ids < used).astype(jnp.int32)
    pos_tiles = (pos * s).reshape(n // td, td, TOP_K).transpose(0, 2, 1).reshape(n // td, 1, TOP_K * td)
    xs = _dispatch(h, pos_tiles, np_rows, td, s)
    ys = _ffn_experts(xs, tile_expert, tile_valid, w1, w3, w2, tm)
    out = _combine(ys, pos_tiles, meta, x_flat, mod, norm_final, l, td)
    return out.reshape(bx, l, d)


def _mod_rows(m):
    r, n = m.shape
    return jnp.pad(m.reshape(r, N_MOD, n // N_MOD), ((0, 0), (0, MOD_ROWS - N_MOD), (0, 0)))


def _hyena_layer(x, mod, gnorm, in_w, in_b, sc_w, sc_b, spectra, f_bias, out_w, out_b, ct_pref):
    z = _norm_mod_matmul(x, mod, gnorm, in_w, in_b, 0, 1, _pick(x.shape[2], ct_pref))
    g = _hyena_conv(z, sc_w, sc_b, spectra, f_bias)
    return _matmul_gated_residual(g, out_w, out_b, x, mod, 2)


@jax.jit
def kernel(x, c, ctx, c_ctx, ada_w, ada_b, norm_mix, norm_ffn, hy_in_w, hy_in_b, hy_sc_w, hy_sc_b, hy_f_w0, hy_f_b0, hy_f_wi, hy_f_bi, hy_f_freq, hy_f_wout, hy_f_bias, hy_out_w, hy_out_b, mla_wq_a, mla_q_norm, mla_wq_b, mla_wkv_a, mla_kv_norm, mla_wkv_b, mla_wo, ffn_w1, ffn_w3, ffn_w2, moe_router, moe_w1, moe_w3, moe_w2, norm_final):
    bsz, l, d = x.shape
    n_ctx = ctx.shape[1]
    depth = ada_w.shape[0]
    assert depth == 2, "layer 0 = Hyena + dense SwiGLU, layer 1 = MLA + expert SwiGLU"

    rows = -(-(bsz + 1) // 8) * 8
    cvec = jnp.zeros((rows, d), F32).at[:bsz].set(c).at[bsz].set(c_ctx)
    mods = _ada_mod(cvec, ada_w, ada_b)
    modx = [_mod_rows(mods[i, :bsz]) for i in range(depth)]
    modc = [_mod_rows(mods[i, bsz:bsz + 1]) for i in range(depth)]

    in_w = hy_in_w[0].astype(BF16)
    out_w = hy_out_w[0].astype(BF16)
    fargs = (hy_f_w0[0], hy_f_b0[0], hy_f_wi[0], hy_f_bi[0], hy_f_freq[0], hy_f_wout[0])
    kx = _hyena_filter_spectra(l, d, *fargs)
    kc = _hyena_filter_spectra(n_ctx, d, *fargs)
    x = _hyena_layer(x, modx[0], norm_mix[0], in_w, hy_in_b[0], hy_sc_w[0], hy_sc_b[0], kx,
                     hy_f_bias[0], out_w, hy_out_b[0], 256)
    ctx = _hyena_layer(ctx, modc[0], norm_mix[0], in_w, hy_in_b[0], hy_sc_w[0], hy_sc_b[0], kc,
                       hy_f_bias[0], out_w, hy_out_b[0], 1024)
    x = _ffn_dense(x, modx[0], norm_ffn[0], ffn_w1[0], ffn_w3[0], ffn_w2[0], 3, 4, 5)
    ctx = _ffn_dense(ctx.reshape(1, bsz * n_ctx, d), modc[0], norm_ffn[0], ffn_w1[0], ffn_w3[0], ffn_w2[0],
                     3, 4, 5).reshape(bsz, n_ctx, d)

    wa, wqb, wk, wv = _mla_weights(mla_wq_a[0], mla_wq_b[0], mla_wkv_a[0], mla_wkv_b[0])
    q_lora = mla_q_norm.shape[1]
    q, kx, vx = _mla_latent(x, modx[1], norm_mix[1], wa, mla_q_norm[0], mla_kv_norm[0], wqb, wk, wv,
                            _rope_tables(l))
    kc, vc = _mla_context(ctx, modc[1], norm_mix[1], wa[:, q_lora:], mla_kv_norm[0], wk, wv)
    o = _attention(q.transpose(0, 2, 1), kc, kx, vc.transpose(0, 2, 1), vx.transpose(0, 2, 1))
    x = _matmul_gated_residual(o, mla_wo[0].astype(BF16), jnp.zeros((d,), F32), x, modx[1], 2)
    return _moe(x, modx[1], norm_ffn[1], moe_router[0], moe_w1[0], moe_w3[0], moe_w2[0], norm_final)
```

```python
import functools
import math

import jax
import jax.numpy as jnp
import numpy as np
from jax import lax
from jax.experimental import pallas as pl
from jax.experimental.pallas import tpu as pltpu

F32 = jnp.float32
BF16 = jnp.bfloat16
HIGHEST = lax.Precision.HIGHEST

RMS_EPS = 1e-6
N_MOD = 6
MOD_ROWS = 8
GRID_W = 64
SHORT_CONV = 3
FILTER_BANDS = 8
FILTER_EMB = 1 + 2 * FILTER_BANDS
FILTER_EMB_PAD = 32
DECAY_TARGET = 1e-2
FAST_DECAY_PCT = 0.3
SLOW_DECAY_PCT = 1.5
N_HEADS = 16
QK_NOPE = 64
QK_ROPE = 32
QK_HEAD = QK_NOPE + QK_ROPE
V_HEAD = 64
ROPE_AXIS = QK_ROPE // 2
ROPE_BASE = 10000.0
TOP_K = 2
LANES = 128
HEAD_SLOT = 128
VMEM_LIMIT = 56 * 1024 * 1024


def _cparams(sem, vmem=VMEM_LIMIT):
    return pltpu.CompilerParams(dimension_semantics=sem, vmem_limit_bytes=vmem)


def _rms(x, g):
    return x * lax.rsqrt(jnp.mean(x * x, axis=-1, keepdims=True) + RMS_EPS) * g


def _silu(x):
    return x * (1.0 / (1.0 + jnp.exp(-x)))


def _store_token_slabs(ref, value):
    rows, width = value.shape
    s = width // LANES
    for j in range(s):
        ref[pl.ds(j, rows, stride=s), :] = value[:, j * LANES:(j + 1) * LANES]


def _load_token_slab(ref, j, rows, s):
    return ref[pl.ds(j, rows, stride=s), :]


def _pick(total, pref):
    t = min(total, pref)
    while total % t:
        t //= 2
    return t


def _ada_kernel(c_ref, w_ref, b_ref, o_ref):
    c = c_ref[...]
    o_ref[0] = jnp.dot(_silu(c), w_ref[0], precision=HIGHEST, preferred_element_type=F32) + b_ref[0]


def _ada_mod(cvec, ada_w, ada_b):
    depth, d, n = ada_w.shape
    r = cvec.shape[0]
    tn = _pick(n, 1536)
    return pl.pallas_call(
        _ada_kernel,
        out_shape=jax.ShapeDtypeStruct((depth, r, n), F32),
        grid=(depth, n // tn),
        in_specs=[pl.BlockSpec((r, d), lambda i, j: (0, 0)),
                  pl.BlockSpec((1, d, tn), lambda i, j: (i, 0, j)),
                  pl.BlockSpec((1, 1, tn), lambda i, j: (i, 0, j))],
        out_specs=pl.BlockSpec((1, r, tn), lambda i, j: (i, 0, j)),
        compiler_params=_cparams(("arbitrary", "arbitrary")),
        name="ada_mod",
    )(cvec, ada_w, ada_b.reshape(depth, 1, n))


HALO = 8


def _hyena_in_kernel(x_ref, xp_ref, xn_ref, mod_ref, g_ref, w_ref, b_ref, cw_ref, cb_ref, o_ref, *, ct):
    i = pl.program_id(1)
    tm = x_ref.shape[1]
    d = x_ref.shape[2]
    mod = mod_ref[0]
    xe = jnp.concatenate([xp_ref[0], x_ref[0], xn_ref[0]], axis=0)
    h = _rms(xe, g_ref[...]) * (1.0 + mod[1:2]) + mod[0:1]
    z = jnp.dot(h.astype(BF16), w_ref[...], preferred_element_type=F32) + b_ref[...]
    row = lax.broadcasted_iota(jnp.int32, (tm + 2 * HALO, 1), 0)
    outside = ((row < HALO) & (i == 0)) | ((row >= tm + HALO) & (i == pl.num_programs(1) - 1))
    z = jnp.where(outside, 0.0, z)
    cw = cw_ref[...]
    conv = (z[HALO - 1:HALO - 1 + tm] * cw[0:1] + z[HALO:HALO + tm] * cw[1:2]
            + z[HALO + 1:HALO + 1 + tm] * cw[2:3] + cb_ref[...])
    x0 = conv[:, :d]
    u = conv[:, 2 * d:] * conv[:, d:2 * d]
    nct = d // ct
    for j in range(nct):
        o_ref[j, 0] = x0[:, j * ct:(j + 1) * ct]
        o_ref[nct + j, 0] = u[:, j * ct:(j + 1) * ct]


def _hyena_in(x, mod, gnorm, w_bf16, bias, sc_w, sc_b, ct, tm_pref=512):
    bx, l, d = x.shape
    n = w_bf16.shape[1]
    tm = _pick(l, tm_pref)
    tpb = tm // HALO
    last = l // HALO - 1
    per_b = mod.shape[0] > 1
    return pl.pallas_call(
        functools.partial(_hyena_in_kernel, ct=ct),
        out_shape=jax.ShapeDtypeStruct((2 * d // ct, bx, l, ct), F32),
        grid=(bx, l // tm),
        in_specs=[pl.BlockSpec((1, tm, d), lambda b, i: (b, i, 0)),
                  pl.BlockSpec((1, HALO, d), lambda b, i: (b, jnp.maximum(i * tpb - 1, 0), 0)),
                  pl.BlockSpec((1, HALO, d), lambda b, i: (b, jnp.minimum((i + 1) * tpb, last), 0)),
                  pl.BlockSpec((1, MOD_ROWS, d), (lambda b, i: (b, 0, 0)) if per_b else (lambda b, i: (0, 0, 0))),
                  pl.BlockSpec((1, d), lambda b, i: (0, 0)),
                  pl.BlockSpec((d, n), lambda b, i: (0, 0)),
                  pl.BlockSpec((1, n), lambda b, i: (0, 0)),
                  pl.BlockSpec((SHORT_CONV, n), lambda b, i: (0, 0)),
                  pl.BlockSpec((1, n), lambda b, i: (0, 0))],
        out_specs=pl.BlockSpec((2 * d // ct, 1, tm, ct), lambda b, i: (0, b, i, 0)),
        compiler_params=_cparams(("arbitrary", "arbitrary")),
        name="hyena_in_proj",
    )(x, x, x, mod, gnorm.reshape(1, d), w_bf16, bias.reshape(1, n), sc_w, sc_b.reshape(1, n))


def _mm_res_kernel(a_ref, w_ref, b_ref, r_ref, mod_ref, o_ref, *, gate_row):
    y = jnp.dot(a_ref[0], w_ref[...], preferred_element_type=F32) + b_ref[...]
    o_ref[0] = r_ref[0] + mod_ref[0][gate_row:gate_row + 1] * y


def _matmul_gated_residual(a_bf16, w_bf16, bias, resid, mod, gate_row, tm_pref=512):
    bx, l, k = a_bf16.shape
    d = w_bf16.shape[1]
    tm = _pick(l, tm_pref)
    per_b = mod.shape[0] > 1
    return pl.pallas_call(
        functools.partial(_mm_res_kernel, gate_row=gate_row),
        out_shape=jax.ShapeDtypeStruct((bx, l, d), F32),
        grid=(bx, l // tm),
        in_specs=[pl.BlockSpec((1, tm, k), lambda b, i: (b, i, 0)),
                  pl.BlockSpec((k, d), lambda b, i: (0, 0)),
                  pl.BlockSpec((1, d), lambda b, i: (0, 0)),
                  pl.BlockSpec((1, tm, d), lambda b, i: (b, i, 0)),
                  pl.BlockSpec((1, MOD_ROWS, d), (lambda b, i: (b, 0, 0)) if per_b else (lambda b, i: (0, 0, 0)))],
        out_specs=pl.BlockSpec((1, tm, d), lambda b, i: (b, i, 0)),
        compiler_params=_cparams(("arbitrary", "arbitrary")),
        name="matmul_gated_residual",
    )(a_bf16, w_bf16, bias.reshape(1, d), resid, mod)


def _filter_kernel(z_ref, w0_ref, b0_ref, wi_ref, bi_ref, fr_ref, wt_ref, wb_ref, dl_ref, fwd_ref, o_ref, h_ref,
                   *, l, p):
    @pl.when(pl.program_id(0) == 0)
    def _():
        fr = fr_ref[...]
        h = jnp.sin(fr * (jnp.dot(z_ref[...], w0_ref[...], precision=HIGHEST, preferred_element_type=F32)
                          + b0_ref[...]))
        for n in range(wi_ref.shape[0]):
            h = jnp.sin(fr * (jnp.dot(h, wi_ref[n], precision=HIGHEST, preferred_element_type=F32) + bi_ref[n]))
        h_ref[...] = h

    top = jnp.dot(h_ref[:l], wt_ref[...], precision=HIGHEST, preferred_element_type=F32)
    bot = jnp.dot(h_ref[l:], wb_ref[...], precision=HIGHEST, preferred_element_type=F32)
    t = z_ref[:, 0:1]
    rows = lax.broadcasted_iota(jnp.int32, (2 * l, 1), 0)
    decay = jnp.where(rows == l, 0.0, jnp.exp(-t * dl_ref[...]))
    k = jnp.concatenate([top, bot], axis=0) * decay
    k = k / jnp.sum(jnp.abs(k), axis=0, keepdims=True)
    nb = l // p
    for di in range(2 * nb - 1):
        start = (p * (di - nb)) % (2 * l)
        if start + 2 * p <= 2 * l:
            seg = k[start:start + 2 * p]
        else:
            seg = jnp.concatenate([k[start:], k[:start + 2 * p - 2 * l]], axis=0)
        o_ref[di] = jnp.dot(fwd_ref[...], seg.astype(BF16), preferred_element_type=F32)


def _conv_block(l, p_pref=512):
    return min(p_pref, l)


def _hyena_filter_spectra(l, d, w0, b0, wi, bi, freq, wout):
    p = _conv_block(l)
    nd = 2 * (l // p) - 1
    pos = jnp.arange(l, dtype=F32)
    t = (pos / max(l - 1, 1))[:, None]
    w = 2.0 * math.pi * pos / l
    f = jnp.linspace(1e-4, FILTER_BANDS - 1, FILTER_BANDS, dtype=F32)
    ang = w[:, None] * f[None, :]
    z = jnp.concatenate([t, jnp.cos(ang), -jnp.sin(ang)], axis=-1)
    deltas = jnp.abs(jnp.linspace(math.log(DECAY_TARGET) / SLOW_DECAY_PCT,
                                  math.log(DECAY_TARGET) / FAST_DECAY_PCT, d, dtype=F32))
    idx = np.concatenate([np.arange(l), [0], np.arange(l - 1, 0, -1)])
    zc = jnp.pad(z[idx], ((0, 0), (0, FILTER_EMB_PAD - FILTER_EMB)))
    w0p = jnp.pad(w0, ((0, FILTER_EMB_PAD - FILTER_EMB), (0, 0)))
    hid = w0.shape[1]
    n_in = wi.shape[0]
    ct = _pick(d, 256)
    nct = d // ct
    fwd = jnp.asarray(_dft_mats(p)[0]).astype(BF16)
    return pl.pallas_call(
        functools.partial(_filter_kernel, l=l, p=p),
        out_shape=jax.ShapeDtypeStruct((nd, 2 * p, d), F32),
        grid=(nct,),
        in_specs=[pl.BlockSpec((2 * l, FILTER_EMB_PAD), lambda j: (0, 0)),
                  pl.BlockSpec((FILTER_EMB_PAD, hid), lambda j: (0, 0)),
                  pl.BlockSpec((1, hid), lambda j: (0, 0)),
                  pl.BlockSpec((n_in, hid, hid), lambda j: (0, 0, 0)),
                  pl.BlockSpec((n_in, 1, hid), lambda j: (0, 0, 0)),
                  pl.BlockSpec((1, hid), lambda j: (0, 0)),
                  pl.BlockSpec((hid, ct), lambda j: (0, j)),
                  pl.BlockSpec((hid, ct), lambda j: (0, nct + j)),
                  pl.BlockSpec((1, ct), lambda j: (0, j)),
                  pl.BlockSpec((2 * p, 2 * p), lambda j: (0, 0))],
        out_specs=pl.BlockSpec((nd, 2 * p, ct), lambda j: (0, 0, j)),
        scratch_shapes=[pltpu.VMEM((2 * l, hid), F32)],
        compiler_params=_cparams(("arbitrary",)),
        name="hyena_filter",
    )(zc, w0p, b0.reshape(1, hid), wi, bi.reshape(n_in, 1, hid), freq.reshape(1, hid), wout, wout,
      deltas.reshape(1, d), fwd)


@functools.lru_cache(maxsize=None)
def _dft_mats(p):
    n = 2 * p
    f = np.arange(p)[:, None]
    t = np.arange(n)[None, :]
    ang = 2.0 * np.pi * (((2 * f + 1) * t) % (4 * p)) / (4 * p)
    fwd = np.concatenate([np.cos(ang), -np.sin(ang)], axis=0)
    q = np.arange(p)[:, None]
    ff = np.arange(p)[None, :]
    ang2 = 2.0 * np.pi * (((2 * ff + 1) * (q + p)) % (4 * p)) / (4 * p)
    inv = np.concatenate([np.cos(ang2), -np.sin(ang2)], axis=1) / p
    return fwd.astype(np.float32), inv.astype(np.float32)


CONV_INVERSE_PIECES = 4


def _hyena_conv_kernel(x0_ref, u_ref, ks_ref, fb_ref, fwd_ref, inv_ref, o_ref, vs_ref, ys_ref, *, l, p, rc):
    nb = l // p
    for j in range(nb):
        vs_ref[j] = jnp.dot(fwd_ref[...], u_ref[0, 0, j * p:(j + 1) * p, :].astype(BF16),
                            preferred_element_type=F32)

    def spectrum_rows(i, c):
        re = slice(c * rc, (c + 1) * rc)
        im = slice(p + c * rc, p + (c + 1) * rc)
        yr = None
        yi = None
        for j in range(nb):
            di = i - j + nb - 1
            kr = ks_ref[di, re, :]
            ki = ks_ref[di, im, :]
            vr = vs_ref[j, re, :]
            vi = vs_ref[j, im, :]
            tr = kr * vr - ki * vi
            ti = kr * vi + ki * vr
            yr = tr if yr is None else yr + tr
            yi = ti if yi is None else yi + ti
        ys_ref[i, re, :] = yr
        ys_ref[i, im, :] = yi

    fb = fb_ref[...]
    pieces = CONV_INVERSE_PIECES
    pr = p // pieces

    spectra_bf16 = {}

    def inverse_piece(i, q):
        if i not in spectra_bf16:
            spectra_bf16[i] = ys_ref[i].astype(BF16)
        y = jnp.dot(inv_ref[q * pr:(q + 1) * pr, :], spectra_bf16[i], preferred_element_type=F32)
        sl = slice(i * p + q * pr, i * p + (q + 1) * pr)
        o_ref[0, sl, :] = (x0_ref[0, 0, sl, :] * (y + u_ref[0, 0, sl, :] * fb)).astype(BF16)

    chunks = p // rc
    per_piece = chunks // pieces
    for i in range(nb + 1):
        for q in range(pieces):
            if i < nb:
                for c in range(q * per_piece, (q + 1) * per_piece):
                    spectrum_rows(i, c)
            if i >= 1:
                inverse_piece(i - 1, q)


def _hyena_conv(zu, spectra, f_bias):
    nz, bx, l, ct = zu.shape
    nct = nz // 2
    d = nct * ct
    nd, p2, _ = spectra.shape
    p = p2 // 2
    nb = l // p
    fwd, inv = _dft_mats(p)
    fwd = jnp.asarray(fwd[:, :p]).astype(BF16)
    inv = jnp.asarray(inv).astype(BF16)
    zspec = lambda part: pl.BlockSpec((1, 1, l, ct), lambda j, b, part=part: (part * nct + j, b, 0, 0))
    once = pl.Buffered(1)
    return pl.pallas_call(
        functools.partial(_hyena_conv_kernel, l=l, p=p, rc=8),
        out_shape=jax.ShapeDtypeStruct((bx, l, d), BF16),
        grid=(nct, bx),
        in_specs=[zspec(0), zspec(1),
                  pl.BlockSpec((nd, 2 * p, ct), lambda j, b: (0, 0, j), pipeline_mode=once),
                  pl.BlockSpec((1, ct), lambda j, b: (0, j)),
                  pl.BlockSpec((2 * p, p), lambda j, b: (0, 0), pipeline_mode=once),
                  pl.BlockSpec((p, 2 * p), lambda j, b: (0, 0), pipeline_mode=once)],
        out_specs=pl.BlockSpec((1, l, ct), lambda j, b: (b, 0, j)),
        scratch_shapes=[pltpu.VMEM((nb, 2 * p, ct), F32),
                        pltpu.VMEM((nb, 2 * p, ct), F32)],
        compiler_params=_cparams(("arbitrary", "arbitrary")),
        name="hyena_conv",
    )(zu, zu, spectra, f_bias.reshape(1, d), fwd, inv)


def _swiglu_step(h_ref, w1_ref, w3_ref, w2_ref, acc_ref, rows):
    h = h_ref[:rows]
    a = jnp.dot(h, w1_ref[0].astype(BF16), preferred_element_type=F32)
    b = jnp.dot(h, w3_ref[0].astype(BF16), preferred_element_type=F32)
    g = (_silu(a) * b).astype(BF16)
    acc_ref[:rows] += jnp.dot(g, w2_ref[0].astype(BF16), preferred_element_type=F32)


def _ffn_dense_kernel(x_ref, mod_ref, g_ref, w1_ref, w3_ref, w2_ref, o_ref, h_ref, acc_ref,
                      *, sh_row, sc_row, gate_row):
    f = pl.program_id(2)

    @pl.when(f == 0)
    def _():
        mod = mod_ref[0]
        h = _rms(x_ref[0], g_ref[...]) * (1.0 + mod[sc_row:sc_row + 1]) + mod[sh_row:sh_row + 1]
        h_ref[...] = h.astype(BF16)
        acc_ref[...] = jnp.zeros_like(acc_ref)

    _swiglu_step(h_ref, w1_ref, w3_ref, w2_ref, acc_ref, h_ref.shape[0])

    @pl.when(f == pl.num_programs(2) - 1)
    def _():
        o_ref[0] = x_ref[0] + mod_ref[0][gate_row:gate_row + 1] * acc_ref[...]


def _ffn_dense(x, mod, gnorm, w1, w3, w2, sh_row, sc_row, gate_row, tm_pref=1024, tf_pref=512):
    bx, l, d = x.shape
    ff = w1.shape[1]
    tm = _pick(l, tm_pref)
    tf = _pick(ff, tf_pref)
    per_b = mod.shape[0] > 1
    return pl.pallas_call(
        functools.partial(_ffn_dense_kernel, sh_row=sh_row, sc_row=sc_row, gate_row=gate_row),
        out_shape=jax.ShapeDtypeStruct((bx, l, d), F32),
        grid=(bx, l // tm, ff // tf),
        in_specs=[pl.BlockSpec((1, tm, d), lambda b, i, f: (b, i, 0)),
                  pl.BlockSpec((1, MOD_ROWS, d), (lambda b, i, f: (b, 0, 0)) if per_b else (lambda b, i, f: (0, 0, 0))),
                  pl.BlockSpec((1, d), lambda b, i, f: (0, 0)),
                  pl.BlockSpec((1, d, tf), lambda b, i, f: (0, 0, f)),
                  pl.BlockSpec((1, d, tf), lambda b, i, f: (0, 0, f)),
                  pl.BlockSpec((1, tf, d), lambda b, i, f: (0, f, 0))],
        out_specs=pl.BlockSpec((1, tm, d), lambda b, i, f: (b, i, 0)),
        scratch_shapes=[pltpu.VMEM((tm, d), BF16), pltpu.VMEM((tm, d), F32)],
        compiler_params=_cparams(("arbitrary", "arbitrary", "arbitrary")),
        name="ffn_dense",
    )(x, mod, gnorm.reshape(1, d), w1[None], w3[None], w2[None])


EXPERT_TILE_PARTS = 4


def _ffn_expert_kernel(te_ref, tp_ref, x_ref, w1_ref, w3_ref, w2_ref, o_ref, h_ref, acc_ref):
    i = pl.program_id(0)
    f = pl.program_id(1)
    tm, d = h_ref.shape
    parts = tp_ref[i]

    @pl.when(parts > 0)
    def _():
        @pl.when(f == 0)
        def _():
            s = d // LANES
            for j in range(s):
                h_ref[:, j * LANES:(j + 1) * LANES] = _load_token_slab(x_ref, j, tm, s).astype(BF16)
            acc_ref[...] = jnp.zeros_like(acc_ref)

        for q in range(1, EXPERT_TILE_PARTS + 1):
            @pl.when(parts == q)
            def _(q=q):
                _swiglu_step(h_ref, w1_ref, w3_ref, w2_ref, acc_ref, q * tm // EXPERT_TILE_PARTS)

        @pl.when(f == pl.num_programs(1) - 1)
        def _():
            _store_token_slabs(o_ref, acc_ref[...])

    @pl.when((parts == 0) & (f == 0))
    def _():
        o_ref[...] = jnp.zeros_like(o_ref)


def _ffn_experts(xs, tile_expert, tile_valid, w1, w3, w2, tm, tf_pref=512):
    d = w1.shape[1]
    s = d // LANES
    np_rows = xs.shape[0] // s
    ff = w1.shape[2]
    tf = _pick(ff, tf_pref)
    nf = ff // tf
    fsel = lambda i, f, tv: jnp.where(tv[i] > 0, f, nf - 1)
    grid_spec = pltpu.PrefetchScalarGridSpec(
        num_scalar_prefetch=2,
        grid=(np_rows // tm, nf),
        in_specs=[pl.BlockSpec((tm * s, LANES), lambda i, f, te, tv: (i, 0)),
                  pl.BlockSpec((1, d, tf), lambda i, f, te, tv: (te[i], 0, fsel(i, f, tv))),
                  pl.BlockSpec((1, d, tf), lambda i, f, te, tv: (te[i], 0, fsel(i, f, tv))),
                  pl.BlockSpec((1, tf, d), lambda i, f, te, tv: (te[i], fsel(i, f, tv), 0))],
        out_specs=pl.BlockSpec((tm * s, LANES), lambda i, f, te, tv: (i, 0)),
        scratch_shapes=[pltpu.VMEM((tm, d), BF16), pltpu.VMEM((tm, d), F32)],
    )
    return pl.pallas_call(
        _ffn_expert_kernel,
        out_shape=jax.ShapeDtypeStruct((np_rows * s, LANES), F32),
        grid_spec=grid_spec,
        compiler_params=_cparams(("arbitrary", "arbitrary")),
        name="ffn_experts",
    )(tile_expert, tile_valid, xs, w1, w3, w2)


def _mla_latent_kernel(x_ref, mod_ref, g_ref, wa_ref, qg_ref, kvg_ref, wqb_ref, wk_ref, wv_ref, tab_ref,
                       q_ref, k_ref, v_ref, *, q_lora, kv_lora):
    mod = mod_ref[0]
    h = _rms(x_ref[0], g_ref[...]) * (1.0 + mod[1:2]) + mod[0:1]
    a = jnp.dot(h.astype(BF16), wa_ref[...], preferred_element_type=F32)
    qn = _rms(a[:, :q_lora], qg_ref[...]).astype(BF16)
    cn = _rms(a[:, q_lora:q_lora + kv_lora], kvg_ref[...]).astype(BF16)
    kpe = a[:, q_lora + kv_lora:]
    tab = tab_ref[...]
    cq, s1q, s2q, ck, s1k, s2k = (tab[:, n * LANES:(n + 1) * LANES] for n in range(6))
    q = jnp.dot(qn, wqb_ref[...], preferred_element_type=F32)
    wq = q.shape[1]
    rep = wq // LANES
    q = (q * jnp.tile(cq, (1, rep)) + pltpu.roll(q, ROPE_AXIS, 1) * jnp.tile(s1q, (1, rep))
         + pltpu.roll(q, wq - ROPE_AXIS, 1) * jnp.tile(s2q, (1, rep)))
    q_ref[0] = q.astype(BF16)
    kr = kpe * ck + pltpu.roll(kpe, ROPE_AXIS, 1) * s1k + pltpu.roll(kpe, LANES - ROPE_AXIS, 1) * s2k
    k = jnp.dot(cn, wk_ref[...], preferred_element_type=F32) + jnp.tile(kr, (1, rep))
    k_ref[0] = k.astype(BF16)
    v_ref[0] = jnp.dot(cn, wv_ref[...], preferred_element_type=F32).astype(BF16)


def _mla_context_kernel(x_ref, mod_ref, g_ref, wa_ref, kvg_ref, wk_ref, wv_ref, k_ref, v_ref, *, kv_lora):
    mod = mod_ref[0]
    h = _rms(x_ref[0], g_ref[...]) * (1.0 + mod[1:2]) + mod[0:1]
    a = jnp.dot(h.astype(BF16), wa_ref[...], preferred_element_type=F32)
    cn = _rms(a[:, :kv_lora], kvg_ref[...]).astype(BF16)
    kpe = a[:, kv_lora:]
    rep = wk_ref.shape[1] // LANES
    k = jnp.dot(cn, wk_ref[...], preferred_element_type=F32) + jnp.tile(kpe, (1, rep))
    k_ref[0] = k.astype(BF16)
    v_ref[0] = jnp.dot(cn, wv_ref[...], preferred_element_type=F32).astype(BF16)


def _rope_tables(l):
    rows = l // GRID_W
    row = jnp.broadcast_to(jnp.arange(rows, dtype=F32)[:, None], (rows, GRID_W)).reshape(l)
    col = jnp.broadcast_to(jnp.arange(GRID_W, dtype=F32)[None, :], (rows, GRID_W)).reshape(l)
    inv = ROPE_BASE ** (-jnp.arange(0, ROPE_AXIS, 2, dtype=F32) / ROPE_AXIS)
    ang = jnp.concatenate([row[:, None] * inv, col[:, None] * inv], axis=-1)
    cos = jnp.cos(ang)
    sin = jnp.sin(ang)
    n = l
    ones = jnp.ones((n, QK_NOPE), F32)
    z16 = jnp.zeros((n, ROPE_AXIS), F32)
    z64 = jnp.zeros((n, QK_NOPE), F32)
    zpad = jnp.zeros((n, HEAD_SLOT - QK_HEAD), F32)
    c = jnp.concatenate([ones, cos, cos, zpad], axis=1)
    s1 = jnp.concatenate([z64, z16, sin, zpad], axis=1)
    s2 = jnp.concatenate([z64, -sin, z16, zpad], axis=1)
    scale = math.log2(math.e) / math.sqrt(QK_HEAD)
    return jnp.concatenate([c * scale, s1 * scale, s2 * scale, c, s1, s2], axis=1)


def _mla_weights(wq_a, wq_b, wkv_a, wkv_b):
    d, q_lora = wq_a.shape
    kv_lora = wkv_a.shape[1] - QK_ROPE
    ev = np.arange(0, QK_ROPE, 2)
    od = np.arange(1, QK_ROPE, 2)
    kpe = wkv_a[:, kv_lora:]
    kpe_slot = jnp.concatenate([jnp.zeros((d, QK_NOPE), F32), kpe[:, ev], kpe[:, od],
                                jnp.zeros((d, HEAD_SLOT - QK_HEAD), F32)], axis=1)
    wa = jnp.concatenate([wq_a, wkv_a[:, :kv_lora], kpe_slot], axis=1)
    qb = wq_b.reshape(q_lora, N_HEADS, QK_HEAD)
    qb = jnp.concatenate([qb[..., :QK_NOPE], qb[..., QK_NOPE + ev], qb[..., QK_NOPE + od],
                          jnp.zeros((q_lora, N_HEADS, HEAD_SLOT - QK_HEAD), F32)], axis=-1)
    kvb = wkv_b.reshape(kv_lora, N_HEADS, QK_NOPE + V_HEAD)
    wk = jnp.concatenate([kvb[..., :QK_NOPE], jnp.zeros((kv_lora, N_HEADS, HEAD_SLOT - QK_NOPE), F32)], axis=-1)
    wv = kvb[..., QK_NOPE:]
    return (wa.astype(BF16), qb.reshape(q_lora, N_HEADS * HEAD_SLOT).astype(BF16),
            wk.reshape(kv_lora, N_HEADS * HEAD_SLOT).astype(BF16),
            wv.reshape(kv_lora, N_HEADS * V_HEAD).astype(BF16))


def _mod_spec(mod):
    per_b = mod.shape[0] > 1
    return pl.BlockSpec((1, MOD_ROWS, mod.shape[2]), (lambda b, i: (b, 0, 0)) if per_b else (lambda b, i: (0, 0, 0)))


def _mla_latent(x, mod, gnorm, wa, q_norm, kv_norm, wqb, wk, wv, tables, tm_pref=512):
    bx, n, d = x.shape
    tm = _pick(n, tm_pref)
    q_lora = q_norm.shape[0]
    kv_lora = kv_norm.shape[0]
    wq = wqb.shape[1]
    wvn = wv.shape[1]
    const = lambda b, i: (0, 0)
    row = lambda width: pl.BlockSpec((1, tm, width), lambda b, i: (b, i, 0))
    return pl.pallas_call(
        functools.partial(_mla_latent_kernel, q_lora=q_lora, kv_lora=kv_lora),
        out_shape=(jax.ShapeDtypeStruct((bx, n, wq), BF16),
                   jax.ShapeDtypeStruct((bx, n, wq), BF16),
                   jax.ShapeDtypeStruct((bx, n, wvn), BF16)),
        grid=(bx, n // tm),
        in_specs=[row(d), _mod_spec(mod),
                  pl.BlockSpec((1, d), const),
                  pl.BlockSpec(wa.shape, const),
                  pl.BlockSpec((1, q_lora), const),
                  pl.BlockSpec((1, kv_lora), const),
                  pl.BlockSpec(wqb.shape, const),
                  pl.BlockSpec(wk.shape, const),
                  pl.BlockSpec(wv.shape, const),
                  pl.BlockSpec((tm, 6 * LANES), lambda b, i: (i, 0))],
        out_specs=(row(wq), row(wq), row(wvn)),
        compiler_params=_cparams(("arbitrary", "arbitrary")),
        name="mla_latent_proj",
    )(x, mod, gnorm.reshape(1, d), wa, q_norm.reshape(1, q_lora), kv_norm.reshape(1, kv_lora),
      wqb, wk, wv, tables)


def _mla_context(ctx, mod, gnorm, wa_kv, kv_norm, wk, wv, tm_pref=512):
    bx, n, d = ctx.shape
    tm = _pick(n, tm_pref)
    kv_lora = kv_norm.shape[0]
    wkn = wk.shape[1]
    wvn = wv.shape[1]
    const = lambda b, i: (0, 0)
    row = lambda width: pl.BlockSpec((1, tm, width), lambda b, i: (b, i, 0))
    return pl.pallas_call(
        functools.partial(_mla_context_kernel, kv_lora=kv_lora),
        out_shape=(jax.ShapeDtypeStruct((bx, n, wkn), BF16),
                   jax.ShapeDtypeStruct((bx, n, wvn), BF16)),
        grid=(bx, n // tm),
        in_specs=[row(d), _mod_spec(mod),
                  pl.BlockSpec((1, d), const),
                  pl.BlockSpec(wa_kv.shape, const),
                  pl.BlockSpec((1, kv_lora), const),
                  pl.BlockSpec(wk.shape, const),
                  pl.BlockSpec(wv.shape, const)],
        out_specs=(row(wkn), row(wvn)),
        compiler_params=_cparams(("arbitrary", "arbitrary")),
        name="mla_context_proj",
    )(ctx, mod, gnorm.reshape(1, d), wa_kv, kv_norm.reshape(1, kv_lora), wk, wv)


ATTN_HEADS_PER_STEP = 8
ATTN_KEY_CHUNK = 256
ATTN_SOFTMAX_LAG = 2
ATTN_PV_LAG = 6
SUM_ROWS = 16


def _attn_kernel(qt_ref, kc_ref, kx_ref, vct_ref, vxt_ref, o_ref, *, heads, ck):
    chunks = ([(kc_ref, vct_ref, c) for c in range(kc_ref.shape[1] // ck)]
              + [(kx_ref, vxt_ref, c) for c in range(kx_ref.shape[1] // ck)])
    items = [(h, ci) for ci in range(len(chunks)) for h in range(heads)]
    state = {h: None for h in range(heads)}
    scores = {}
    probs = {}
    ones = jnp.ones((SUM_ROWS, ck), BF16)

    def qk(t):
        h, ci = items[t]
        kref, _, c = chunks[ci]
        hsl = slice(h * HEAD_SLOT, (h + 1) * HEAD_SLOT)
        scores[t] = jnp.dot(kref[0, c * ck:(c + 1) * ck, hsl], qt_ref[0, hsl, :], preferred_element_type=F32)

    def softmax(t):
        h, _ = items[t]
        s = scores.pop(t)
        mc = jnp.max(s, axis=0, keepdims=True)
        if state[h] is None:
            probs[t] = (jnp.exp2(s - mc).astype(BF16), None)
            state[h] = (mc, None)
        else:
            m, acc = state[h]
            m_new = jnp.maximum(m, mc)
            probs[t] = (jnp.exp2(s - m_new).astype(BF16), jnp.exp2(m - m_new))
            state[h] = (m_new, acc)

    def pv(t):
        h, ci = items[t]
        _, vref, c = chunks[ci]
        p, alpha = probs.pop(t)
        lhs = jnp.concatenate([vref[0, h * V_HEAD:(h + 1) * V_HEAD, c * ck:(c + 1) * ck], ones], axis=0)
        o = jnp.dot(lhs, p, preferred_element_type=F32)
        m, acc = state[h]
        state[h] = (m, o if acc is None else alpha * acc + o)

    n_items = len(items)
    for t in range(n_items + ATTN_PV_LAG):
        if t < n_items:
            qk(t)
        if ATTN_SOFTMAX_LAG <= t < n_items + ATTN_SOFTMAX_LAG:
            softmax(t - ATTN_SOFTMAX_LAG)
        if t >= ATTN_PV_LAG:
            pv(t - ATTN_PV_LAG)

    for hp in range(heads // 2):
        outs = [state[h][1][:V_HEAD] / state[h][1][V_HEAD:V_HEAD + 1] for h in (2 * hp, 2 * hp + 1)]
        pair = jnp.concatenate(outs, axis=0)
        o_ref[0, :, hp * 2 * V_HEAD:(hp + 1) * 2 * V_HEAD] = pair.T.astype(BF16)


def _attention(qt, kc, kx, vct, vxt, tq_pref=256, heads=ATTN_HEADS_PER_STEP):
    bx, _, l = qt.shape
    n_ctx = kc.shape[1]
    tq = _pick(l, tq_pref)
    ck = _pick(n_ctx, ATTN_KEY_CHUNK)
    qk_w = heads * HEAD_SLOT
    v_w = heads * V_HEAD
    return pl.pallas_call(
        functools.partial(_attn_kernel, heads=heads, ck=ck),
        out_shape=jax.ShapeDtypeStruct((bx, l, N_HEADS * V_HEAD), BF16),
        grid=(bx, N_HEADS // heads, l // tq),
        in_specs=[pl.BlockSpec((1, qk_w, tq), lambda b, g, i: (b, g, i)),
                  pl.BlockSpec((1, n_ctx, qk_w), lambda b, g, i: (b, 0, g)),
                  pl.BlockSpec((1, l, qk_w), lambda b, g, i: (b, 0, g)),
                  pl.BlockSpec((1, v_w, n_ctx), lambda b, g, i: (b, g, 0)),
                  pl.BlockSpec((1, v_w, l), lambda b, g, i: (b, g, 0))],
        out_specs=pl.BlockSpec((1, tq, v_w), lambda b, g, i: (b, i, g)),
        compiler_params=_cparams(("arbitrary", "arbitrary", "arbitrary")),
        name="mla_attention",
    )(qt, kc, kx, vct, vxt)


def _route_kernel(x_ref, mod_ref, g_ref, r_ref, tri_ref, h_ref, meta_ref, cnt_ref, carry_ref, *, n_exp):
    i = pl.program_id(0)

    @pl.when(i == 0)
    def _():
        carry_ref[...] = jnp.zeros_like(carry_ref)

    mod = mod_ref[0]
    h = _rms(x_ref[...], g_ref[...]) * (1.0 + mod[4:5]) + mod[3:4]
    _store_token_slabs(h_ref, h)
    h_hi = h.astype(BF16)
    h_lo = (h - h_hi.astype(F32)).astype(BF16)
    r = r_ref[...]
    r_hi = r.astype(BF16)
    r_lo = (r - r_hi.astype(F32)).astype(BF16)
    logits = (jnp.dot(h_hi, r_hi, preferred_element_type=F32) + jnp.dot(h_hi, r_lo, preferred_element_type=F32)
              + jnp.dot(h_lo, r_hi, preferred_element_type=F32))
    lane = lax.broadcasted_iota(jnp.int32, logits.shape, 1)
    lane_f = lane.astype(F32)
    neg = jnp.float32(-jnp.inf)
    lg = jnp.where(lane < n_exp, logits, neg)
    v1 = jnp.max(lg, axis=-1, keepdims=True)
    i1 = jnp.min(jnp.where(lg == v1, lane_f, float(LANES)), axis=-1, keepdims=True)
    oh1 = lane_f == i1
    lg2 = jnp.where(oh1, neg, lg)
    v2 = jnp.max(lg2, axis=-1, keepdims=True)
    i2 = jnp.min(jnp.where(lg2 == v2, lane_f, float(LANES)), axis=-1, keepdims=True)
    oh2 = lane_f == i2
    e = jnp.exp(v2 - v1)
    g1 = 1.0 / (1.0 + e)
    g2 = e / (1.0 + e)
    oh = jnp.where(oh1 | oh2, 1.0, 0.0)
    pref = jnp.dot(tri_ref[...], oh.astype(BF16), preferred_element_type=F32)
    excl = pref - oh + carry_ref[...]
    r1 = jnp.sum(jnp.where(oh1, excl, 0.0), axis=-1, keepdims=True)
    r2 = jnp.sum(jnp.where(oh2, excl, 0.0), axis=-1, keepdims=True)
    carry_ref[...] += jnp.sum(oh, axis=0, keepdims=True)
    cnt_ref[...] = carry_ref[...]
    meta = jnp.where(lane == 0, i1, 0.0)
    meta = jnp.where(lane == 1, i2, meta)
    meta = jnp.where(lane == 2, g1, meta)
    meta = jnp.where(lane == 3, g2, meta)
    meta = jnp.where(lane == 4, r1, meta)
    meta = jnp.where(lane == 5, r2, meta)
    meta_ref[...] = meta


def _route(x_flat, mod, gnorm, router, tokens_per_batch, tr_pref=512):
    n, d = x_flat.shape
    n_exp = router.shape[1]
    tr = _pick(tokens_per_batch, tr_pref)
    per_b = tokens_per_batch // tr
    rpad = jnp.pad(router, ((0, 0), (0, LANES - n_exp)))
    tri = jnp.asarray(np.tril(np.ones((tr, tr), np.float32))).astype(BF16)
    return pl.pallas_call(
        functools.partial(_route_kernel, n_exp=n_exp),
        out_shape=(jax.ShapeDtypeStruct((n * (d // LANES), LANES), F32),
                   jax.ShapeDtypeStruct((n, LANES), F32),
                   jax.ShapeDtypeStruct((1, LANES), F32)),
        grid=(n // tr,),
        in_specs=[pl.BlockSpec((tr, d), lambda i: (i, 0)),
                  pl.BlockSpec((1, MOD_ROWS, d), lambda i: (i // per_b, 0, 0)),
                  pl.BlockSpec((1, d), lambda i: (0, 0)),
                  pl.BlockSpec((d, LANES), lambda i: (0, 0)),
                  pl.BlockSpec((tr, tr), lambda i: (0, 0))],
        out_specs=(pl.BlockSpec((tr * (d // LANES), LANES), lambda i: (i, 0)),
                   pl.BlockSpec((tr, LANES), lambda i: (i, 0)),
                   pl.BlockSpec((1, LANES), lambda i: (0, 0))),
        scratch_shapes=[pltpu.VMEM((1, LANES), F32)],
        compiler_params=_cparams(("arbitrary",)),
        name="moe_route",
    )(x_flat, mod, gnorm.reshape(1, d), rpad, tri)


ROW_DMA_UNROLL = 8


def _wait_rows(any_ref, rows, sem):
    blk = any_ref.at[pl.ds(0, rows)]
    pltpu.make_async_copy(blk, blk, sem).wait()


def _dispatch_kernel(pos_ref, h_ref, xs_in_ref, xs_ref, sem, *, td, s):
    del xs_in_ref

    def issue(t, c):
        src = h_ref.at[pl.ds(pl.multiple_of(t * s, s), s)]
        for k in range(TOP_K):
            dst = xs_ref.at[pl.ds(pl.multiple_of(pos_ref[0, 0, k * td + t], s), s)]
            pltpu.make_async_copy(src, dst, sem).start(priority=k)
        return c

    lax.fori_loop(0, td, issue, 0, unroll=ROW_DMA_UNROLL)
    _wait_rows(xs_ref, TOP_K * td * s, sem)


def _dispatch(h, pos_tiles, np_rows, td, s):
    n = h.shape[0] // s
    xs0 = jnp.zeros((np_rows * s, LANES), F32)
    return pl.pallas_call(
        functools.partial(_dispatch_kernel, td=td, s=s),
        out_shape=jax.ShapeDtypeStruct((np_rows * s, LANES), F32),
        grid=(n // td,),
        in_specs=[pl.BlockSpec((1, 1, TOP_K * td), lambda i: (i, 0, 0), memory_space=pltpu.SMEM),
                  pl.BlockSpec((td * s, LANES), lambda i: (i, 0)),
                  pl.BlockSpec(memory_space=pl.ANY)],
        out_specs=pl.BlockSpec(memory_space=pl.ANY),
        scratch_shapes=[pltpu.SemaphoreType.DMA(())],
        input_output_aliases={2: 0},
        compiler_params=_cparams(("arbitrary",)),
        name="moe_dispatch",
    )(pos_tiles, h, xs0)


def _combine_kernel(pos_ref, posn_ref, ys_ref, meta_ref, x_ref, mod_ref, g_ref, o_ref, buf_ref, sems, *, td, s):
    i = pl.program_id(0)
    n = pl.num_programs(0)
    slot = lax.rem(i, 2)

    def gather(p_ref, sl):
        def issue(t, c):
            for k in range(TOP_K):
                src = ys_ref.at[pl.ds(pl.multiple_of(p_ref[0, 0, k * td + t], s), s)]
                dst = buf_ref.at[sl, k, pl.ds(pl.multiple_of(t * s, s), s)]
                pltpu.make_async_copy(src, dst, sems.at[sl]).start(priority=k)
            return c

        lax.fori_loop(0, td, issue, 0, unroll=ROW_DMA_UNROLL)

    @pl.when(i == 0)
    def _():
        gather(pos_ref, slot)

    @pl.when(i + 1 < n)
    def _():
        gather(posn_ref, 1 - slot)

    _wait_rows(ys_ref, TOP_K * td * s, sems.at[slot])
    meta = meta_ref[...]
    g0 = meta[:, 2:3]
    g1 = meta[:, 3:4]
    gate = mod_ref[0][5:6]
    xs = []
    ssq = jnp.zeros((td, 1), F32)
    for j in range(s):
        lanes = slice(j * LANES, (j + 1) * LANES)
        y = (g0 * _load_token_slab(buf_ref.at[slot, 0], j, td, s)
             + g1 * _load_token_slab(buf_ref.at[slot, 1], j, td, s))
        xj = x_ref[:, lanes] + gate[:, lanes] * y
        ssq = ssq + jnp.sum(xj * xj, axis=-1, keepdims=True)
        xs.append(xj)
    inv = lax.rsqrt(ssq / (s * LANES) + RMS_EPS)
    for j in range(s):
        lanes = slice(j * LANES, (j + 1) * LANES)
        o_ref[:, lanes] = xs[j] * inv * g_ref[:, lanes]


def _combine(ys, pos_tiles, meta, x_flat, mod, norm_final, tokens_per_batch, td):
    n, d = x_flat.shape
    s = d // LANES
    per_b = tokens_per_batch // td
    nt = n // td
    return pl.pallas_call(
        functools.partial(_combine_kernel, td=td, s=s),
        out_shape=jax.ShapeDtypeStruct((n, d), F32),
        grid=(nt,),
        in_specs=[pl.BlockSpec((1, 1, TOP_K * td), lambda i: (i, 0, 0), memory_space=pltpu.SMEM),
                  pl.BlockSpec((1, 1, TOP_K * td), lambda i: (jnp.minimum(i + 1, nt - 1), 0, 0),
                               memory_space=pltpu.SMEM),
                  pl.BlockSpec(memory_space=pl.ANY),
                  pl.BlockSpec((td, LANES), lambda i: (i, 0)),
                  pl.BlockSpec((td, d), lambda i: (i, 0)),
                  pl.BlockSpec((1, MOD_ROWS, d), lambda i: (i // per_b, 0, 0)),
                  pl.BlockSpec((1, d), lambda i: (0, 0))],
        out_specs=pl.BlockSpec((td, d), lambda i: (i, 0)),
        scratch_shapes=[pltpu.VMEM((2, TOP_K, td * s, LANES), F32), pltpu.SemaphoreType.DMA((2,))],
        compiler_params=_cparams(("arbitrary",)),
        name="moe_combine",
    )(pos_tiles, pos_tiles, ys, meta, x_flat, mod, norm_final.reshape(1, d))


EXPERT_TILE_ROWS = 1024


def _moe(x, mod, gnorm, router, w1, w3, w2, norm_final, td=256):
    bx, l, d = x.shape
    n = bx * l
    n_exp = router.shape[1]
    s = d // LANES
    tm = min(EXPERT_TILE_ROWS, n * TOP_K)
    x_flat = x.reshape(n, d)
    h, meta, counts = _route(x_flat, mod, gnorm, router, l)
    idx = meta[:, 0:TOP_K].astype(jnp.int32)
    rank = meta[:, 4:4 + TOP_K].astype(jnp.int32)
    cnt = counts[0, :n_exp].astype(jnp.int32)
    tiles_e = (cnt + tm - 1) // tm
    tile_end = jnp.cumsum(tiles_e)
    start_rows = (tile_end - tiles_e) * tm
    sel = idx[..., None] == jnp.arange(n_exp, dtype=jnp.int32)
    pos = jnp.sum(jnp.where(sel, start_rows, 0), axis=-1) + rank
    n_tiles = -(-(n * TOP_K) // tm) + n_exp
    np_rows = n_tiles * tm
    tile_ids = jnp.arange(n_tiles, dtype=jnp.int32)
    used = tile_end[-1]
    tile_expert = jnp.sum((jnp.minimum(tile_ids, used - 1)[:, None] >= tile_end[None, :]).astype(jnp.int32), axis=1)
    tile_expert = jnp.minimum(tile_expert, n_exp - 1)
    onehot_e = tile_expert[:, None] == jnp.arange(n_exp, dtype=jnp.int32)
    tile_cnt = jnp.sum(jnp.where(onehot_e, cnt, 0), axis=1)
    tile_first = jnp.sum(jnp.where(onehot_e, tile_end - tiles_e, 0), axis=1)
    tile_rows = jnp.clip(tile_cnt - (tile_ids - tile_first) * tm, 0, tm)
    part = tm // EXPERT_TILE_PARTS
    tile_valid = jnp.where(tile_ids < used, (tile_rows + part - 1) // part, 0).astype(jnp.int32)
    pos_tiles = (pos * s).reshape(n // td, td, TOP_K).transpose(0, 2, 1).reshape(n // td, 1, TOP_K * td)
    xs = _dispatch(h, pos_tiles, np_rows, td, s)
    ys = _ffn_experts(xs, tile_expert, tile_valid, w1, w3, w2, tm)
    out = _combine(ys, pos_tiles, meta, x_flat, mod, norm_final, l, td)
    return out.reshape(bx, l, d)


def _mod_rows(m):
    r, n = m.shape
    return jnp.pad(m.reshape(r, N_MOD, n // N_MOD), ((0, 0), (0, MOD_ROWS - N_MOD), (0, 0)))


def _hyena_layer(x, mod, gnorm, in_w, in_b, sc_w, sc_b, spectra, f_bias, out_w, out_b, ct_pref):
    zu = _hyena_in(x, mod, gnorm, in_w, in_b, sc_w, sc_b, _pick(x.shape[2], ct_pref))
    g = _hyena_conv(zu, spectra, f_bias)
    return _matmul_gated_residual(g, out_w, out_b, x, mod, 2)


@jax.jit
def kernel(x, c, ctx, c_ctx, ada_w, ada_b, norm_mix, norm_ffn, hy_in_w, hy_in_b, hy_sc_w, hy_sc_b, hy_f_w0, hy_f_b0, hy_f_wi, hy_f_bi, hy_f_freq, hy_f_wout, hy_f_bias, hy_out_w, hy_out_b, mla_wq_a, mla_q_norm, mla_wq_b, mla_wkv_a, mla_kv_norm, mla_wkv_b, mla_wo, ffn_w1, ffn_w3, ffn_w2, moe_router, moe_w1, moe_w3, moe_w2, norm_final):
    bsz, l, d = x.shape
    n_ctx = ctx.shape[1]
    depth = ada_w.shape[0]
    assert depth == 2, "layer 0 = Hyena + dense SwiGLU, layer 1 = MLA + expert SwiGLU"

    rows = -(-(bsz + 1) // 8) * 8
    cvec = jnp.zeros((rows, d), F32).at[:bsz].set(c).at[bsz].set(c_ctx)
    mods = _ada_mod(cvec, ada_w, ada_b)
    modx = [_mod_rows(mods[i, :bsz]) for i in range(depth)]
    modc = [_mod_rows(mods[i, bsz:bsz + 1]) for i in range(depth)]

    in_w = hy_in_w[0].astype(BF16)
    out_w = hy_out_w[0].astype(BF16)
    fargs = (hy_f_w0[0], hy_f_b0[0], hy_f_wi[0], hy_f_bi[0], hy_f_freq[0], hy_f_wout[0])
    kx = _hyena_filter_spectra(l, d, *fargs)
    kc = _hyena_filter_spectra(n_ctx, d, *fargs)
    x = _hyena_layer(x, modx[0], norm_mix[0], in_w, hy_in_b[0], hy_sc_w[0], hy_sc_b[0], kx,
                     hy_f_bias[0], out_w, hy_out_b[0], 256)
    ctx = _hyena_layer(ctx, modc[0], norm_mix[0], in_w, hy_in_b[0], hy_sc_w[0], hy_sc_b[0], kc,
                       hy_f_bias[0], out_w, hy_out_b[0], 1024)
    x = _ffn_dense(x, modx[0], norm_ffn[0], ffn_w1[0], ffn_w3[0], ffn_w2[0], 3, 4, 5)
    ctx = _ffn_dense(ctx.reshape(1, bsz * n_ctx, d), modc[0], norm_ffn[0], ffn_w1[0], ffn_w3[0], ffn_w2[0],
                     3, 4, 5).reshape(bsz, n_ctx, d)

    wa, wqb, wk, wv = _mla_weights(mla_wq_a[0], mla_wq_b[0], mla_wkv_a[0], mla_wkv_b[0])
    q_lora = mla_q_norm.shape[1]
    q, kx, vx = _mla_latent(x, modx[1], norm_mix[1], wa, mla_q_norm[0], mla_kv_norm[0], wqb, wk, wv,
                            _rope_tables(l))
    kc, vc = _mla_context(ctx, modc[1], norm_mix[1], wa[:, q_lora:], mla_kv_norm[0], wk, wv)
    o = _attention(q.transpose(0, 2, 1), kc, kx, vc.transpose(0, 2, 1), vx.transpose(0, 2, 1))
    x = _matmul_gated_residual(o, mla_wo[0].astype(BF16), jnp.zeros((d,), F32), x, modx[1], 2)
    return _moe(x, modx[1], norm_ffn[1], moe_router[0], moe_w1[0], moe_w3[0], moe_w2[0], norm_final)
```

```python
import functools
import math

import jax
import jax.numpy as jnp
import numpy as np
from jax import lax
from jax.experimental import pallas as pl
from jax.experimental.pallas import tpu as pltpu

F32 = jnp.float32
BF16 = jnp.bfloat16
HIGHEST = lax.Precision.HIGHEST

RMS_EPS = 1e-6
N_MOD = 6
MOD_ROWS = 8
MOD_SH1, MOD_SC1, MOD_G1, MOD_SH2, MOD_SC2, MOD_G2 = range(N_MOD)
GRID_W = 64
SHORT_CONV = 3
FILTER_BANDS = 8
FILTER_EMB = 1 + 2 * FILTER_BANDS
FILTER_EMB_PAD = 32
DECAY_TARGET = 1e-2
FAST_DECAY_PCT = 0.3
SLOW_DECAY_PCT = 1.5
N_HEADS = 16
QK_NOPE = 64
QK_ROPE = 32
QK_HEAD = QK_NOPE + QK_ROPE
V_HEAD = 64
ROPE_AXIS = QK_ROPE // 2
ROPE_BASE = 10000.0
TOP_K = 2
LANES = 128
HEAD_SLOT = 128
VMEM_LIMIT = 56 * 1024 * 1024


def _cparams(sem, vmem=VMEM_LIMIT):
    return pltpu.CompilerParams(dimension_semantics=sem, vmem_limit_bytes=vmem)


def _rms(x, g):
    return x * lax.rsqrt(jnp.mean(x * x, axis=-1, keepdims=True) + RMS_EPS) * g


def _silu(x):
    return x * (1.0 / (1.0 + jnp.exp(-x)))


def _store_token_slabs(ref, value):
    rows, width = value.shape
    s = width // LANES
    for j in range(s):
        ref[pl.ds(j, rows, stride=s), :] = value[:, j * LANES:(j + 1) * LANES]


def _load_token_slab(ref, j, rows, s):
    return ref[pl.ds(j, rows, stride=s), :]


def _pick(total, pref):
    t = min(total, pref)
    while total % t:
        t //= 2
    return t


def _ada_kernel(c_ref, w_ref, b_ref, o_ref):
    c = c_ref[...]
    o_ref[0] = jnp.dot(_silu(c), w_ref[0], precision=HIGHEST, preferred_element_type=F32) + b_ref[0]


def _ada_mod(cvec, ada_w, ada_b):
    depth, d, n = ada_w.shape
    r = cvec.shape[0]
    tn = _pick(n, 1536)
    return pl.pallas_call(
        _ada_kernel,
        out_shape=jax.ShapeDtypeStruct((depth, r, n), F32),
        grid=(depth, n // tn),
        in_specs=[pl.BlockSpec((r, d), lambda i, j: (0, 0)),
                  pl.BlockSpec((1, d, tn), lambda i, j: (i, 0, j)),
                  pl.BlockSpec((1, 1, tn), lambda i, j: (i, 0, j))],
        out_specs=pl.BlockSpec((1, r, tn), lambda i, j: (i, 0, j)),
        compiler_params=_cparams(("arbitrary", "arbitrary")),
        name="ada_mod",
    )(cvec, ada_w, ada_b.reshape(depth, 1, n))


HALO = 8


def _hyena_in_kernel(x_ref, xp_ref, xn_ref, mod_ref, g_ref, w_ref, b_ref, cw_ref, cb_ref, o_ref, *, ct):
    i = pl.program_id(1)
    tm = x_ref.shape[1]
    d = x_ref.shape[2]
    mod = mod_ref[0]
    xe = jnp.concatenate([xp_ref[0], x_ref[0], xn_ref[0]], axis=0)
    h = _rms(xe, g_ref[...]) * (1.0 + mod[1:2]) + mod[0:1]
    z = jnp.dot(h.astype(BF16), w_ref[...], preferred_element_type=F32) + b_ref[...]
    first = jnp.where(i == 0, 0.0, z[:HALO])
    final = jnp.where(i == pl.num_programs(1) - 1, 0.0, z[tm + HALO:])
    z = jnp.concatenate([first, z[HALO:tm + HALO], final], axis=0)
    cw = cw_ref[...]
    conv = (z[HALO - 1:HALO - 1 + tm] * cw[0:1] + z[HALO:HALO + tm] * cw[1:2]
            + z[HALO + 1:HALO + 1 + tm] * cw[2:3] + cb_ref[...])
    x0 = conv[:, :d]
    u = conv[:, 2 * d:] * conv[:, d:2 * d]
    nct = d // ct
    for j in range(nct):
        o_ref[j, 0] = x0[:, j * ct:(j + 1) * ct]
        o_ref[nct + j, 0] = u[:, j * ct:(j + 1) * ct]


def _hyena_in(x, mod, gnorm, w_bf16, bias, sc_w, sc_b, ct, tm_pref=512):
    bx, l, d = x.shape
    n = w_bf16.shape[1]
    tm = _pick(l, tm_pref)
    tpb = tm // HALO
    last = l // HALO - 1
    per_b = mod.shape[0] > 1
    return pl.pallas_call(
        functools.partial(_hyena_in_kernel, ct=ct),
        out_shape=jax.ShapeDtypeStruct((2 * d // ct, bx, l, ct), F32),
        grid=(bx, l // tm),
        in_specs=[pl.BlockSpec((1, tm, d), lambda b, i: (b, i, 0)),
                  pl.BlockSpec((1, HALO, d), lambda b, i: (b, jnp.maximum(i * tpb - 1, 0), 0)),
                  pl.BlockSpec((1, HALO, d), lambda b, i: (b, jnp.minimum((i + 1) * tpb, last), 0)),
                  pl.BlockSpec((1, MOD_ROWS, d), (lambda b, i: (b, 0, 0)) if per_b else (lambda b, i: (0, 0, 0))),
                  pl.BlockSpec((1, d), lambda b, i: (0, 0)),
                  pl.BlockSpec((d, n), lambda b, i: (0, 0)),
                  pl.BlockSpec((1, n), lambda b, i: (0, 0)),
                  pl.BlockSpec((SHORT_CONV, n), lambda b, i: (0, 0)),
                  pl.BlockSpec((1, n), lambda b, i: (0, 0))],
        out_specs=pl.BlockSpec((2 * d // ct, 1, tm, ct), lambda b, i: (0, b, i, 0)),
        compiler_params=_cparams(("arbitrary", "arbitrary")),
        name="hyena_in_proj",
    )(x, x, x, mod, gnorm.reshape(1, d), w_bf16, bias.reshape(1, n), sc_w, sc_b.reshape(1, n))


def _filter_kernel(z_ref, w0_ref, b0_ref, wi_ref, bi_ref, fr_ref, wt_ref, wb_ref, dl_ref, fwd_ref, o_ref, h_ref,
                   *, l, p):
    @pl.when(pl.program_id(0) == 0)
    def _():
        fr = fr_ref[...]
        h = jnp.sin(fr * (jnp.dot(z_ref[...], w0_ref[...], precision=HIGHEST, preferred_element_type=F32)
                          + b0_ref[...]))
        for n in range(wi_ref.shape[0]):
            h = jnp.sin(fr * (jnp.dot(h, wi_ref[n], precision=HIGHEST, preferred_element_type=F32) + bi_ref[n]))
        h_ref[...] = h

    top = jnp.dot(h_ref[:l], wt_ref[...], precision=HIGHEST, preferred_element_type=F32)
    bot = jnp.dot(h_ref[l:], wb_ref[...], precision=HIGHEST, preferred_element_type=F32)
    t = z_ref[:, 0:1]
    rows = lax.broadcasted_iota(jnp.int32, (2 * l, 1), 0)
    decay = jnp.where(rows == l, 0.0, jnp.exp(-t * dl_ref[...]))
    k = jnp.concatenate([top, bot], axis=0) * decay
    k = k / jnp.sum(jnp.abs(k), axis=0, keepdims=True)
    nb = l // p
    for di in range(2 * nb - 1):
        start = (p * (di - nb)) % (2 * l)
        if start + 2 * p <= 2 * l:
            seg = k[start:start + 2 * p]
        else:
            seg = jnp.concatenate([k[start:], k[:start + 2 * p - 2 * l]], axis=0)
        o_ref[di] = jnp.dot(fwd_ref[...], seg.astype(BF16), preferred_element_type=F32)


def _conv_block(l, p_pref=512):
    return min(p_pref, l)


def _hyena_filter_spectra(l, d, w0, b0, wi, bi, freq, wout):
    p = _conv_block(l)
    nd = 2 * (l // p) - 1
    pos = jnp.arange(l, dtype=F32)
    t = (pos / max(l - 1, 1))[:, None]
    w = 2.0 * math.pi * pos / l
    f = jnp.linspace(1e-4, FILTER_BANDS - 1, FILTER_BANDS, dtype=F32)
    ang = w[:, None] * f[None, :]
    z = jnp.concatenate([t, jnp.cos(ang), -jnp.sin(ang)], axis=-1)
    deltas = jnp.abs(jnp.linspace(math.log(DECAY_TARGET) / SLOW_DECAY_PCT,
                                  math.log(DECAY_TARGET) / FAST_DECAY_PCT, d, dtype=F32))
    idx = np.concatenate([np.arange(l), [0], np.arange(l - 1, 0, -1)])
    zc = jnp.pad(z[idx], ((0, 0), (0, FILTER_EMB_PAD - FILTER_EMB)))
    w0p = jnp.pad(w0, ((0, FILTER_EMB_PAD - FILTER_EMB), (0, 0)))
    hid = w0.shape[1]
    n_in = wi.shape[0]
    ct = _pick(d, 256)
    nct = d // ct
    fwd = jnp.asarray(_dft_mats(p)[0]).astype(BF16)
    return pl.pallas_call(
        functools.partial(_filter_kernel, l=l, p=p),
        out_shape=jax.ShapeDtypeStruct((nd, 2 * p, d), F32),
        grid=(nct,),
        in_specs=[pl.BlockSpec((2 * l, FILTER_EMB_PAD), lambda j: (0, 0)),
                  pl.BlockSpec((FILTER_EMB_PAD, hid), lambda j: (0, 0)),
                  pl.BlockSpec((1, hid), lambda j: (0, 0)),
                  pl.BlockSpec((n_in, hid, hid), lambda j: (0, 0, 0)),
                  pl.BlockSpec((n_in, 1, hid), lambda j: (0, 0, 0)),
                  pl.BlockSpec((1, hid), lambda j: (0, 0)),
                  pl.BlockSpec((hid, ct), lambda j: (0, j)),
                  pl.BlockSpec((hid, ct), lambda j: (0, nct + j)),
                  pl.BlockSpec((1, ct), lambda j: (0, j)),
                  pl.BlockSpec((2 * p, 2 * p), lambda j: (0, 0))],
        out_specs=pl.BlockSpec((nd, 2 * p, ct), lambda j: (0, 0, j)),
        scratch_shapes=[pltpu.VMEM((2 * l, hid), F32)],
        compiler_params=_cparams(("arbitrary",)),
        name="hyena_filter",
    )(zc, w0p, b0.reshape(1, hid), wi, bi.reshape(n_in, 1, hid), freq.reshape(1, hid), wout, wout,
      deltas.reshape(1, d), fwd)


@functools.lru_cache(maxsize=None)
def _dft_mats(p):
    n = 2 * p
    f = np.arange(p)[:, None]
    t = np.arange(n)[None, :]
    ang = 2.0 * np.pi * (((2 * f + 1) * t) % (4 * p)) / (4 * p)
    fwd = np.concatenate([np.cos(ang), -np.sin(ang)], axis=0)
    q = np.arange(p)[:, None]
    ff = np.arange(p)[None, :]
    ang2 = 2.0 * np.pi * (((2 * ff + 1) * (q + p)) % (4 * p)) / (4 * p)
    inv = np.concatenate([np.cos(ang2), -np.sin(ang2)], axis=1) / p
    return fwd.astype(np.float32), inv.astype(np.float32)


CONV_INVERSE_PIECES = 4


def _hyena_conv_kernel(x0_ref, u_ref, ks_ref, fb_ref, fwd_ref, inv_ref, o_ref, vs_ref, ys_ref, *, l, p, rc):
    nb = l // p
    for j in range(nb):
        vs_ref[j] = jnp.dot(fwd_ref[...], u_ref[0, 0, j * p:(j + 1) * p, :].astype(BF16),
                            preferred_element_type=F32)

    def spectrum_rows(i, c):
        re = slice(c * rc, (c + 1) * rc)
        im = slice(p + c * rc, p + (c + 1) * rc)
        yr = None
        yi = None
        for j in range(nb):
            di = i - j + nb - 1
            kr = ks_ref[di, re, :]
            ki = ks_ref[di, im, :]
            vr = vs_ref[j, re, :]
            vi = vs_ref[j, im, :]
            tr = kr * vr - ki * vi
            ti = kr * vi + ki * vr
            yr = tr if yr is None else yr + tr
            yi = ti if yi is None else yi + ti
        ys_ref[i, re, :] = yr
        ys_ref[i, im, :] = yi

    fb = fb_ref[...]
    pieces = CONV_INVERSE_PIECES
    pr = p // pieces

    spectra_bf16 = {}

    def inverse_piece(i, q):
        if i not in spectra_bf16:
            spectra_bf16[i] = ys_ref[i].astype(BF16)
        y = jnp.dot(inv_ref[q * pr:(q + 1) * pr, :], spectra_bf16[i], preferred_element_type=F32)
        sl = slice(i * p + q * pr, i * p + (q + 1) * pr)
        o_ref[0, sl, :] = (x0_ref[0, 0, sl, :] * (y + u_ref[0, 0, sl, :] * fb)).astype(BF16)

    chunks = p // rc
    per_piece = chunks // pieces
    for i in range(nb + 1):
        for q in range(pieces):
            if i < nb:
                for c in range(q * per_piece, (q + 1) * per_piece):
                    spectrum_rows(i, c)
            if i >= 1:
                inverse_piece(i - 1, q)


def _hyena_conv(zu, spectra, f_bias):
    nz, bx, l, ct = zu.shape
    nct = nz // 2
    d = nct * ct
    nd, p2, _ = spectra.shape
    p = p2 // 2
    nb = l // p
    fwd, inv = _dft_mats(p)
    fwd = jnp.asarray(fwd[:, :p]).astype(BF16)
    inv = jnp.asarray(inv).astype(BF16)
    zspec = lambda part: pl.BlockSpec((1, 1, l, ct), lambda j, b, part=part: (part * nct + j, b, 0, 0))
    once = pl.Buffered(1)
    return pl.pallas_call(
        functools.partial(_hyena_conv_kernel, l=l, p=p, rc=8),
        out_shape=jax.ShapeDtypeStruct((bx, l, d), BF16),
        grid=(nct, bx),
        in_specs=[zspec(0), zspec(1),
                  pl.BlockSpec((nd, 2 * p, ct), lambda j, b: (0, 0, j), pipeline_mode=once),
                  pl.BlockSpec((1, ct), lambda j, b: (0, j)),
                  pl.BlockSpec((2 * p, p), lambda j, b: (0, 0), pipeline_mode=once),
                  pl.BlockSpec((p, 2 * p), lambda j, b: (0, 0), pipeline_mode=once)],
        out_specs=pl.BlockSpec((1, l, ct), lambda j, b: (b, 0, j)),
        scratch_shapes=[pltpu.VMEM((nb, 2 * p, ct), F32),
                        pltpu.VMEM((nb, 2 * p, ct), F32)],
        compiler_params=_cparams(("arbitrary", "arbitrary")),
        name="hyena_conv",
    )(zu, zu, spectra, f_bias.reshape(1, d), fwd, inv)


def _swiglu_step(h_ref, w1_ref, w3_ref, w2_ref, acc_ref, rows):
    h = h_ref[:rows]
    a = jnp.dot(h, w1_ref[0].astype(BF16), preferred_element_type=F32)
    b = jnp.dot(h, w3_ref[0].astype(BF16), preferred_element_type=F32)
    g = (_silu(a) * b).astype(BF16)
    acc_ref[:rows] += jnp.dot(g, w2_ref[0].astype(BF16), preferred_element_type=F32)


def _ffn_dense_kernel(x_ref, a_ref, pw_ref, pb_ref, mod_ref, g_ref, w1_ref, w3_ref, w2_ref, o_ref, h_ref, acc_ref):
    f = pl.program_id(2)

    @pl.when(f == 0)
    def _():
        mod = mod_ref[0]
        y = jnp.dot(a_ref[0], pw_ref[...], preferred_element_type=F32) + pb_ref[...]
        xr = x_ref[0] + mod[MOD_G1:MOD_G1 + 1] * y
        o_ref[0] = xr
        h = _rms(xr, g_ref[...]) * (1.0 + mod[MOD_SC2:MOD_SC2 + 1]) + mod[MOD_SH2:MOD_SH2 + 1]
        h_ref[...] = h.astype(BF16)
        acc_ref[...] = jnp.zeros_like(acc_ref)

    _swiglu_step(h_ref, w1_ref, w3_ref, w2_ref, acc_ref, h_ref.shape[0])

    @pl.when(f == pl.num_programs(2) - 1)
    def _():
        o_ref[0] = o_ref[0] + mod_ref[0][MOD_G2:MOD_G2 + 1] * acc_ref[...]


def _mixer_proj_ffn_dense(x, a_bf16, pw_bf16, pb, mod, gnorm, w1, w3, w2, tm_pref=1024, tf_pref=512):
    bx, l, d = x.shape
    k = a_bf16.shape[2]
    ff = w1.shape[1]
    tm = _pick(l, tm_pref)
    tf = _pick(ff, tf_pref)
    per_b = mod.shape[0] > 1
    once = pl.Buffered(1)
    return pl.pallas_call(
        _ffn_dense_kernel,
        out_shape=jax.ShapeDtypeStruct((bx, l, d), F32),
        grid=(bx, l // tm, ff // tf),
        in_specs=[pl.BlockSpec((1, tm, d), lambda b, i, f: (b, i, 0)),
                  pl.BlockSpec((1, tm, k), lambda b, i, f: (b, i, 0)),
                  pl.BlockSpec((k, d), lambda b, i, f: (0, 0), pipeline_mode=once),
                  pl.BlockSpec((1, d), lambda b, i, f: (0, 0)),
                  pl.BlockSpec((1, MOD_ROWS, d), (lambda b, i, f: (b, 0, 0)) if per_b else (lambda b, i, f: (0, 0, 0))),
                  pl.BlockSpec((1, d), lambda b, i, f: (0, 0)),
                  pl.BlockSpec((1, d, tf), lambda b, i, f: (0, 0, f)),
                  pl.BlockSpec((1, d, tf), lambda b, i, f: (0, 0, f)),
                  pl.BlockSpec((1, tf, d), lambda b, i, f: (0, f, 0))],
        out_specs=pl.BlockSpec((1, tm, d), lambda b, i, f: (b, i, 0)),
        scratch_shapes=[pltpu.VMEM((tm, d), BF16), pltpu.VMEM((tm, d), F32)],
        compiler_params=_cparams(("arbitrary", "arbitrary", "arbitrary")),
        name="ffn_dense",
    )(x, a_bf16, pw_bf16, pb.reshape(1, d), mod, gnorm.reshape(1, d), w1[None], w3[None], w2[None])


EXPERT_TILE_PARTS = 4


def _ffn_expert_kernel(te_ref, tp_ref, x_ref, w1_ref, w3_ref, w2_ref, o_ref, h_ref, acc_ref):
    i = pl.program_id(0)
    f = pl.program_id(1)
    tm, d = h_ref.shape
    parts = tp_ref[i]

    @pl.when(parts > 0)
    def _():
        @pl.when(f == 0)
        def _():
            s = d // LANES
            for j in range(s):
                h_ref[:, j * LANES:(j + 1) * LANES] = _load_token_slab(x_ref, j, tm, s).astype(BF16)
            acc_ref[...] = jnp.zeros_like(acc_ref)

        for q in range(1, EXPERT_TILE_PARTS + 1):
            @pl.when(parts == q)
            def _(q=q):
                _swiglu_step(h_ref, w1_ref, w3_ref, w2_ref, acc_ref, q * tm // EXPERT_TILE_PARTS)

        @pl.when(f == pl.num_programs(1) - 1)
        def _():
            _store_token_slabs(o_ref, acc_ref[...])

    @pl.when((parts == 0) & (f == 0))
    def _():
        o_ref[...] = jnp.zeros_like(o_ref)


def _ffn_experts(xs, tile_expert, tile_valid, w1, w3, w2, tm, tf_pref=512):
    d = w1.shape[1]
    s = d // LANES
    np_rows = xs.shape[0] // s
    ff = w1.shape[2]
    tf = _pick(ff, tf_pref)
    nf = ff // tf
    fsel = lambda i, f, tv: jnp.where(tv[i] > 0, f, nf - 1)
    grid_spec = pltpu.PrefetchScalarGridSpec(
        num_scalar_prefetch=2,
        grid=(np_rows // tm, nf),
        in_specs=[pl.BlockSpec((tm * s, LANES), lambda i, f, te, tv: (i, 0)),
                  pl.BlockSpec((1, d, tf), lambda i, f, te, tv: (te[i], 0, fsel(i, f, tv))),
                  pl.BlockSpec((1, d, tf), lambda i, f, te, tv: (te[i], 0, fsel(i, f, tv))),
                  pl.BlockSpec((1, tf, d), lambda i, f, te, tv: (te[i], fsel(i, f, tv), 0))],
        out_specs=pl.BlockSpec((tm * s, LANES), lambda i, f, te, tv: (i, 0)),
        scratch_shapes=[pltpu.VMEM((tm, d), BF16), pltpu.VMEM((tm, d), F32)],
    )
    return pl.pallas_call(
        _ffn_expert_kernel,
        out_shape=jax.ShapeDtypeStruct((np_rows * s, LANES), F32),
        grid_spec=grid_spec,
        compiler_params=_cparams(("arbitrary", "arbitrary")),
        name="ffn_experts",
    )(tile_expert, tile_valid, xs, w1, w3, w2)


def _mla_latent_kernel(x_ref, mod_ref, g_ref, wa_ref, qg_ref, kvg_ref, wqb_ref, wk_ref, wv_ref, tab_ref,
                       q_ref, k_ref, v_ref, *, q_lora, kv_lora):
    mod = mod_ref[0]
    h = _rms(x_ref[0], g_ref[...]) * (1.0 + mod[1:2]) + mod[0:1]
    a = jnp.dot(h.astype(BF16), wa_ref[...], preferred_element_type=F32)
    qn = _rms(a[:, :q_lora], qg_ref[...]).astype(BF16)
    cn = _rms(a[:, q_lora:q_lora + kv_lora], kvg_ref[...]).astype(BF16)
    kpe = a[:, q_lora + kv_lora:]
    tab = tab_ref[...]
    cq, s1q, s2q, ck, s1k, s2k = (tab[:, n * LANES:(n + 1) * LANES] for n in range(6))
    q = jnp.dot(qn, wqb_ref[...], preferred_element_type=F32)
    wq = q.shape[1]
    rep = wq // LANES
    q = (q * jnp.tile(cq, (1, rep)) + pltpu.roll(q, ROPE_AXIS, 1) * jnp.tile(s1q, (1, rep))
         + pltpu.roll(q, wq - ROPE_AXIS, 1) * jnp.tile(s2q, (1, rep)))
    q_ref[0] = q.astype(BF16)
    kr = kpe * ck + pltpu.roll(kpe, ROPE_AXIS, 1) * s1k + pltpu.roll(kpe, LANES - ROPE_AXIS, 1) * s2k
    k = jnp.dot(cn, wk_ref[...], preferred_element_type=F32) + jnp.tile(kr, (1, rep))
    k_ref[0] = k.astype(BF16)
    v_ref[0] = jnp.dot(cn, wv_ref[...], preferred_element_type=F32).astype(BF16)


def _mla_context_kernel(x_ref, mod_ref, g_ref, wa_ref, kvg_ref, wk_ref, wv_ref, k_ref, v_ref, *, kv_lora):
    mod = mod_ref[0]
    h = _rms(x_ref[0], g_ref[...]) * (1.0 + mod[1:2]) + mod[0:1]
    a = jnp.dot(h.astype(BF16), wa_ref[...], preferred_element_type=F32)
    cn = _rms(a[:, :kv_lora], kvg_ref[...]).astype(BF16)
    kpe = a[:, kv_lora:]
    rep = wk_ref.shape[1] // LANES
    k = jnp.dot(cn, wk_ref[...], preferred_element_type=F32) + jnp.tile(kpe, (1, rep))
    k_ref[0] = k.astype(BF16)
    v_ref[0] = jnp.dot(cn, wv_ref[...], preferred_element_type=F32).astype(BF16)


def _rope_tables(l):
    rows = l // GRID_W
    row = jnp.broadcast_to(jnp.arange(rows, dtype=F32)[:, None], (rows, GRID_W)).reshape(l)
    col = jnp.broadcast_to(jnp.arange(GRID_W, dtype=F32)[None, :], (rows, GRID_W)).reshape(l)
    inv = ROPE_BASE ** (-jnp.arange(0, ROPE_AXIS, 2, dtype=F32) / ROPE_AXIS)
    ang = jnp.concatenate([row[:, None] * inv, col[:, None] * inv], axis=-1)
    cos = jnp.cos(ang)
    sin = jnp.sin(ang)
    n = l
    ones = jnp.ones((n, QK_NOPE), F32)
    z16 = jnp.zeros((n, ROPE_AXIS), F32)
    z64 = jnp.zeros((n, QK_NOPE), F32)
    zpad = jnp.zeros((n, HEAD_SLOT - QK_HEAD), F32)
    c = jnp.concatenate([ones, cos, cos, zpad], axis=1)
    s1 = jnp.concatenate([z64, z16, sin, zpad], axis=1)
    s2 = jnp.concatenate([z64, -sin, z16, zpad], axis=1)
    scale = math.log2(math.e) / math.sqrt(QK_HEAD)
    return jnp.concatenate([c * scale, s1 * scale, s2 * scale, c, s1, s2], axis=1)


def _mla_weights(wq_a, wq_b, wkv_a, wkv_b):
    d, q_lora = wq_a.shape
    kv_lora = wkv_a.shape[1] - QK_ROPE
    ev = np.arange(0, QK_ROPE, 2)
    od = np.arange(1, QK_ROPE, 2)
    kpe = wkv_a[:, kv_lora:]
    kpe_slot = jnp.concatenate([jnp.zeros((d, QK_NOPE), F32), kpe[:, ev], kpe[:, od],
                                jnp.zeros((d, HEAD_SLOT - QK_HEAD), F32)], axis=1)
    wa = jnp.concatenate([wq_a, wkv_a[:, :kv_lora], kpe_slot], axis=1)
    qb = wq_b.reshape(q_lora, N_HEADS, QK_HEAD)
    qb = jnp.concatenate([qb[..., :QK_NOPE], qb[..., QK_NOPE + ev], qb[..., QK_NOPE + od],
                          jnp.zeros((q_lora, N_HEADS, HEAD_SLOT - QK_HEAD), F32)], axis=-1)
    kvb = wkv_b.reshape(kv_lora, N_HEADS, QK_NOPE + V_HEAD)
    wk = jnp.concatenate([kvb[..., :QK_NOPE], jnp.zeros((kv_lora, N_HEADS, HEAD_SLOT - QK_NOPE), F32)], axis=-1)
    wv = kvb[..., QK_NOPE:]
    return (wa.astype(BF16), qb.reshape(q_lora, N_HEADS * HEAD_SLOT).astype(BF16),
            wk.reshape(kv_lora, N_HEADS * HEAD_SLOT).astype(BF16),
            wv.reshape(kv_lora, N_HEADS * V_HEAD).astype(BF16))


def _mod_spec(mod):
    per_b = mod.shape[0] > 1
    return pl.BlockSpec((1, MOD_ROWS, mod.shape[2]), (lambda b, i: (b, 0, 0)) if per_b else (lambda b, i: (0, 0, 0)))


def _mla_latent(x, mod, gnorm, wa, q_norm, kv_norm, wqb, wk, wv, tables, tm_pref=512):
    bx, n, d = x.shape
    tm = _pick(n, tm_pref)
    q_lora = q_norm.shape[0]
    kv_lora = kv_norm.shape[0]
    wq = wqb.shape[1]
    wvn = wv.shape[1]
    const = lambda b, i: (0, 0)
    row = lambda width: pl.BlockSpec((1, tm, width), lambda b, i: (b, i, 0))
    return pl.pallas_call(
        functools.partial(_mla_latent_kernel, q_lora=q_lora, kv_lora=kv_lora),
        out_shape=(jax.ShapeDtypeStruct((bx, n, wq), BF16),
                   jax.ShapeDtypeStruct((bx, n, wq), BF16),
                   jax.ShapeDtypeStruct((bx, n, wvn), BF16)),
        grid=(bx, n // tm),
        in_specs=[row(d), _mod_spec(mod),
                  pl.BlockSpec((1, d), const),
                  pl.BlockSpec(wa.shape, const),
                  pl.BlockSpec((1, q_lora), const),
                  pl.BlockSpec((1, kv_lora), const),
                  pl.BlockSpec(wqb.shape, const),
                  pl.BlockSpec(wk.shape, const),
                  pl.BlockSpec(wv.shape, const),
                  pl.BlockSpec((tm, 6 * LANES), lambda b, i: (i, 0))],
        out_specs=(row(wq), row(wq), row(wvn)),
        compiler_params=_cparams(("arbitrary", "arbitrary")),
        name="mla_latent_proj",
    )(x, mod, gnorm.reshape(1, d), wa, q_norm.reshape(1, q_lora), kv_norm.reshape(1, kv_lora),
      wqb, wk, wv, tables)


def _mla_context(ctx, mod, gnorm, wa_kv, kv_norm, wk, wv, tm_pref=512):
    bx, n, d = ctx.shape
    tm = _pick(n, tm_pref)
    kv_lora = kv_norm.shape[0]
    wkn = wk.shape[1]
    wvn = wv.shape[1]
    const = lambda b, i: (0, 0)
    row = lambda width: pl.BlockSpec((1, tm, width), lambda b, i: (b, i, 0))
    return pl.pallas_call(
        functools.partial(_mla_context_kernel, kv_lora=kv_lora),
        out_shape=(jax.ShapeDtypeStruct((bx, n, wkn), BF16),
                   jax.ShapeDtypeStruct((bx, n, wvn), BF16)),
        grid=(bx, n // tm),
        in_specs=[row(d), _mod_spec(mod),
                  pl.BlockSpec((1, d), const),
                  pl.BlockSpec(wa_kv.shape, const),
                  pl.BlockSpec((1, kv_lora), const),
                  pl.BlockSpec(wk.shape, const),
                  pl.BlockSpec(wv.shape, const)],
        out_specs=(row(wkn), row(wvn)),
        compiler_params=_cparams(("arbitrary", "arbitrary")),
        name="mla_context_proj",
    )(ctx, mod, gnorm.reshape(1, d), wa_kv, kv_norm.reshape(1, kv_lora), wk, wv)


ATTN_HEADS_PER_STEP = 16
ATTN_KEY_CHUNK = 256
ATTN_SOFTMAX_LAG = 2
ATTN_PV_LAG = 6
SUM_ROWS = 16


def _attn_kernel(qt_ref, kc_ref, kx_ref, vct_ref, vxt_ref, o_ref, *, heads, ck):
    chunks = ([(kc_ref, vct_ref, c) for c in range(kc_ref.shape[1] // ck)]
              + [(kx_ref, vxt_ref, c) for c in range(kx_ref.shape[1] // ck)])
    items = [(h, ci) for ci in range(len(chunks)) for h in range(heads)]
    state = {h: None for h in range(heads)}
    scores = {}
    probs = {}
    ones = jnp.ones((SUM_ROWS, ck), BF16)

    def qk(t):
        h, ci = items[t]
        kref, _, c = chunks[ci]
        hsl = slice(h * HEAD_SLOT, (h + 1) * HEAD_SLOT)
        scores[t] = jnp.dot(kref[0, c * ck:(c + 1) * ck, hsl], qt_ref[0, hsl, :], preferred_element_type=F32)

    def softmax(t):
        h, _ = items[t]
        s = scores.pop(t)
        mc = jnp.max(s, axis=0, keepdims=True)
        if state[h] is None:
            probs[t] = (jnp.exp2(s - mc).astype(BF16), None)
            state[h] = (mc, None)
        else:
            m, acc = state[h]
            m_new = jnp.maximum(m, mc)
            probs[t] = (jnp.exp2(s - m_new).astype(BF16), jnp.exp2(m - m_new))
            state[h] = (m_new, acc)

    def pv(t):
        h, ci = items[t]
        _, vref, c = chunks[ci]
        p, alpha = probs.pop(t)
        lhs = jnp.concatenate([vref[0, h * V_HEAD:(h + 1) * V_HEAD, c * ck:(c + 1) * ck], ones], axis=0)
        o = jnp.dot(lhs, p, preferred_element_type=F32)
        m, acc = state[h]
        state[h] = (m, o if acc is None else alpha * acc + o)

    n_items = len(items)
    for t in range(n_items + ATTN_PV_LAG):
        if t < n_items:
            qk(t)
        if ATTN_SOFTMAX_LAG <= t < n_items + ATTN_SOFTMAX_LAG:
            softmax(t - ATTN_SOFTMAX_LAG)
        if t >= ATTN_PV_LAG:
            pv(t - ATTN_PV_LAG)

    for hp in range(heads // 2):
        outs = [state[h][1][:V_HEAD] / state[h][1][V_HEAD:V_HEAD + 1] for h in (2 * hp, 2 * hp + 1)]
        pair = jnp.concatenate(outs, axis=0)
        o_ref[0, :, hp * 2 * V_HEAD:(hp + 1) * 2 * V_HEAD] = pair.T.astype(BF16)


def _attention(qt, kc, kx, vct, vxt, tq_pref=256, heads=ATTN_HEADS_PER_STEP):
    bx, _, l = qt.shape
    n_ctx = kc.shape[1]
    tq = _pick(l, tq_pref)
    ck = _pick(n_ctx, ATTN_KEY_CHUNK)
    qk_w = heads * HEAD_SLOT
    v_w = heads * V_HEAD
    return pl.pallas_call(
        functools.partial(_attn_kernel, heads=heads, ck=ck),
        out_shape=jax.ShapeDtypeStruct((bx, l, N_HEADS * V_HEAD), BF16),
        grid=(bx, N_HEADS // heads, l // tq),
        in_specs=[pl.BlockSpec((1, qk_w, tq), lambda b, g, i: (b, g, i)),
                  pl.BlockSpec((1, n_ctx, qk_w), lambda b, g, i: (b, 0, g)),
                  pl.BlockSpec((1, l, qk_w), lambda b, g, i: (b, 0, g)),
                  pl.BlockSpec((1, v_w, n_ctx), lambda b, g, i: (b, g, 0)),
                  pl.BlockSpec((1, v_w, l), lambda b, g, i: (b, g, 0))],
        out_specs=pl.BlockSpec((1, tq, v_w), lambda b, g, i: (b, i, g)),
        compiler_params=_cparams(("arbitrary", "arbitrary", "arbitrary")),
        name="mla_attention",
    )(qt, kc, kx, vct, vxt)


ROUTE_SUBTILES = 2


def _route_kernel(x_ref, a_ref, pw_ref, mod_ref, g_ref, r_ref, tri_ref, xr_ref, h_ref, meta_ref, cnt_ref,
                  carry_ref, *, n_exp):
    i = pl.program_id(0)

    @pl.when(i == 0)
    def _():
        carry_ref[...] = jnp.zeros_like(carry_ref)

    mod = mod_ref[0]
    gate = mod[MOD_G1:MOD_G1 + 1]
    tr, d = x_ref.shape
    rs = tri_ref.shape[0]
    s = d // LANES
    proj = [jnp.dot(a_ref[t * rs:(t + 1) * rs, :], pw_ref[...], preferred_element_type=F32) for t in range(tr // rs)]
    r = r_ref[...]
    r_hi = r.astype(BF16)
    r_lo = (r - r_hi.astype(F32)).astype(BF16)
    lane = lax.broadcasted_iota(jnp.int32, (rs, LANES), 1)
    lane_f = lane.astype(F32)
    neg = jnp.float32(-jnp.inf)
    carry = carry_ref[...]
    for t in range(tr // rs):
        rows = slice(t * rs, (t + 1) * rs)
        xr = x_ref[rows, :] + gate * proj[t]
        xr_ref[rows, :] = xr
        h = _rms(xr, g_ref[...]) * (1.0 + mod[MOD_SC2:MOD_SC2 + 1]) + mod[MOD_SH2:MOD_SH2 + 1]
        _store_token_slabs(h_ref.at[pl.ds(t * rs * s, rs * s)], h)
        h_hi = h.astype(BF16)
        h_lo = (h - h_hi.astype(F32)).astype(BF16)
        logits = (jnp.dot(h_hi, r_hi, preferred_element_type=F32) + jnp.dot(h_hi, r_lo, preferred_element_type=F32)
                  + jnp.dot(h_lo, r_hi, preferred_element_type=F32))
        lg = jnp.where(lane < n_exp, logits, neg)
        v1 = jnp.max(lg, axis=-1, keepdims=True)
        i1 = jnp.min(jnp.where(lg == v1, lane_f, float(LANES)), axis=-1, keepdims=True)
        oh1 = lane_f == i1
        lg2 = jnp.where(oh1, neg, lg)
        v2 = jnp.max(lg2, axis=-1, keepdims=True)
        i2 = jnp.min(jnp.where(lg2 == v2, lane_f, float(LANES)), axis=-1, keepdims=True)
        oh2 = lane_f == i2
        e = jnp.exp(v2 - v1)
        g1 = 1.0 / (1.0 + e)
        g2 = e / (1.0 + e)
        oh = jnp.where(oh1 | oh2, 1.0, 0.0)
        pref = jnp.dot(tri_ref[...], oh.astype(BF16), preferred_element_type=F32)
        excl = pref - oh + carry
        r1 = jnp.sum(jnp.where(oh1, excl, 0.0), axis=-1, keepdims=True)
        r2 = jnp.sum(jnp.where(oh2, excl, 0.0), axis=-1, keepdims=True)
        carry = carry + jnp.sum(oh, axis=0, keepdims=True)
        meta = jnp.where(lane == 0, i1, 0.0)
        meta = jnp.where(lane == 1, i2, meta)
        meta = jnp.where(lane == 2, g1, meta)
        meta = jnp.where(lane == 3, g2, meta)
        meta = jnp.where(lane == 4, r1, meta)
        meta = jnp.where(lane == 5, r2, meta)
        meta_ref[rows, :] = meta
    carry_ref[...] = carry
    cnt_ref[...] = carry


def _mixer_proj_route(x_flat, a_flat, pw_bf16, mod, gnorm, router, tokens_per_batch, tr_pref=512):
    n, d = x_flat.shape
    k = a_flat.shape[1]
    n_exp = router.shape[1]
    tr = _pick(tokens_per_batch, tr_pref)
    per_b = tokens_per_batch // tr
    rpad = jnp.pad(router, ((0, 0), (0, LANES - n_exp)))
    rs = tr // ROUTE_SUBTILES
    tri = jnp.asarray(np.tril(np.ones((rs, rs), np.float32))).astype(BF16)
    return pl.pallas_call(
        functools.partial(_route_kernel, n_exp=n_exp),
        out_shape=(jax.ShapeDtypeStruct((n, d), F32),
                   jax.ShapeDtypeStruct((n * (d // LANES), LANES), F32),
                   jax.ShapeDtypeStruct((n, LANES), F32),
                   jax.ShapeDtypeStruct((1, LANES), F32)),
        grid=(n // tr,),
        in_specs=[pl.BlockSpec((tr, d), lambda i: (i, 0)),
                  pl.BlockSpec((tr, k), lambda i: (i, 0)),
                  pl.BlockSpec((k, d), lambda i: (0, 0)),
                  pl.BlockSpec((1, MOD_ROWS, d), lambda i: (i // per_b, 0, 0)),
                  pl.BlockSpec((1, d), lambda i: (0, 0)),
                  pl.BlockSpec((d, LANES), lambda i: (0, 0)),
                  pl.BlockSpec((rs, rs), lambda i: (0, 0))],
        out_specs=(pl.BlockSpec((tr, d), lambda i: (i, 0)),
                   pl.BlockSpec((tr * (d // LANES), LANES), lambda i: (i, 0)),
                   pl.BlockSpec((tr, LANES), lambda i: (i, 0)),
                   pl.BlockSpec((1, LANES), lambda i: (0, 0))),
        scratch_shapes=[pltpu.VMEM((1, LANES), F32)],
        compiler_params=_cparams(("arbitrary",)),
        name="moe_route",
    )(x_flat, a_flat, pw_bf16, mod, gnorm.reshape(1, d), rpad, tri)


ROW_DMA_UNROLL = 8


def _wait_rows(any_ref, rows, sem):
    blk = any_ref.at[pl.ds(0, rows)]
    pltpu.make_async_copy(blk, blk, sem).wait()


def _dispatch_kernel(pos_ref, h_ref, xs_in_ref, xs_ref, sem, *, td, s):
    del xs_in_ref

    def issue(t, c):
        src = h_ref.at[pl.ds(pl.multiple_of(t * s, s), s)]
        for k in range(TOP_K):
            dst = xs_ref.at[pl.ds(pl.multiple_of(pos_ref[0, 0, k * td + t], s), s)]
            pltpu.make_async_copy(src, dst, sem).start(priority=k)
        return c

    lax.fori_loop(0, td, issue, 0, unroll=ROW_DMA_UNROLL)
    _wait_rows(xs_ref, TOP_K * td * s, sem)


def _dispatch(h, pos_tiles, np_rows, td, s):
    n = h.shape[0] // s
    xs0 = jnp.zeros((np_rows * s, LANES), F32)
    return pl.pallas_call(
        functools.partial(_dispatch_kernel, td=td, s=s),
        out_shape=jax.ShapeDtypeStruct((np_rows * s, LANES), F32),
        grid=(n // td,),
        in_specs=[pl.BlockSpec((1, 1, TOP_K * td), lambda i: (i, 0, 0), memory_space=pltpu.SMEM),
                  pl.BlockSpec((td * s, LANES), lambda i: (i, 0)),
                  pl.BlockSpec(memory_space=pl.ANY)],
        out_specs=pl.BlockSpec(memory_space=pl.ANY),
        scratch_shapes=[pltpu.SemaphoreType.DMA(())],
        input_output_aliases={2: 0},
        compiler_params=_cparams(("arbitrary",)),
        name="moe_dispatch",
    )(pos_tiles, h, xs0)


def _combine_kernel(pos_ref, posn_ref, ys_ref, meta_ref, x_ref, mod_ref, g_ref, o_ref, buf_ref, sems, *, td, s):
    i = pl.program_id(0)
    n = pl.num_programs(0)
    slot = lax.rem(i, 2)

    def gather(p_ref, sl):
        def issue(t, c):
            for k in range(TOP_K):
                src = ys_ref.at[pl.ds(pl.multiple_of(p_ref[0, 0, k * td + t], s), s)]
                dst = buf_ref.at[sl, k, pl.ds(pl.multiple_of(t * s, s), s)]
                pltpu.make_async_copy(src, dst, sems.at[sl]).start(priority=k)
            return c

        lax.fori_loop(0, td, issue, 0, unroll=ROW_DMA_UNROLL)

    @pl.when(i == 0)
    def _():
        gather(pos_ref, slot)

    @pl.when(i + 1 < n)
    def _():
        gather(posn_ref, 1 - slot)

    _wait_rows(ys_ref, TOP_K * td * s, sems.at[slot])
    meta = meta_ref[...]
    g0 = meta[:, 2:3]
    g1 = meta[:, 3:4]
    gate = mod_ref[0][5:6]
    xs = []
    ssq = jnp.zeros((td, 1), F32)
    for j in range(s):
        lanes = slice(j * LANES, (j + 1) * LANES)
        y = (g0 * _load_token_slab(buf_ref.at[slot, 0], j, td, s)
             + g1 * _load_token_slab(buf_ref.at[slot, 1], j, td, s))
        xj = x_ref[:, lanes] + gate[:, lanes] * y
        ssq = ssq + jnp.sum(xj * xj, axis=-1, keepdims=True)
        xs.append(xj)
    inv = lax.rsqrt(ssq / (s * LANES) + RMS_EPS)
    for j in range(s):
        lanes = slice(j * LANES, (j + 1) * LANES)
        o_ref[:, lanes] = xs[j] * inv * g_ref[:, lanes]


def _combine(ys, pos_tiles, meta, x_flat, mod, norm_final, tokens_per_batch, td):
    n, d = x_flat.shape
    s = d // LANES
    per_b = tokens_per_batch // td
    nt = n // td
    return pl.pallas_call(
        functools.partial(_combine_kernel, td=td, s=s),
        out_shape=jax.ShapeDtypeStruct((n, d), F32),
        grid=(nt,),
        in_specs=[pl.BlockSpec((1, 1, TOP_K * td), lambda i: (i, 0, 0), memory_space=pltpu.SMEM),
                  pl.BlockSpec((1, 1, TOP_K * td), lambda i: (jnp.minimum(i + 1, nt - 1), 0, 0),
                               memory_space=pltpu.SMEM),
                  pl.BlockSpec(memory_space=pl.ANY),
                  pl.BlockSpec((td, LANES), lambda i: (i, 0)),
                  pl.BlockSpec((td, d), lambda i: (i, 0)),
                  pl.BlockSpec((1, MOD_ROWS, d), lambda i: (i // per_b, 0, 0)),
                  pl.BlockSpec((1, d), lambda i: (0, 0))],
        out_specs=pl.BlockSpec((td, d), lambda i: (i, 0)),
        scratch_shapes=[pltpu.VMEM((2, TOP_K, td * s, LANES), F32), pltpu.SemaphoreType.DMA((2,))],
        compiler_params=_cparams(("arbitrary",)),
        name="moe_combine",
    )(pos_tiles, pos_tiles, ys, meta, x_flat, mod, norm_final.reshape(1, d))


EXPERT_TILE_ROWS = 1024


def _attn_proj_moe(x, o, wo_bf16, mod, gnorm, router, w1, w3, w2, norm_final, td=256):
    bx, l, d = x.shape
    n = bx * l
    n_exp = router.shape[1]
    s = d // LANES
    tm = min(EXPERT_TILE_ROWS, n * TOP_K)
    x_flat, h, meta, counts = _mixer_proj_route(x.reshape(n, d), o.reshape(n, o.shape[2]), wo_bf16, mod, gnorm,
                                                router, l)
    idx = meta[:, 0:TOP_K].astype(jnp.int32)
    rank = meta[:, 4:4 + TOP_K].astype(jnp.int32)
    cnt = counts[0, :n_exp].astype(jnp.int32)
    tiles_e = (cnt + tm - 1) // tm
    tile_end = jnp.cumsum(tiles_e)
    start_rows = (tile_end - tiles_e) * tm
    sel = idx[..., None] == jnp.arange(n_exp, dtype=jnp.int32)
    pos = jnp.sum(jnp.where(sel, start_rows, 0), axis=-1) + rank
    n_tiles = -(-(n * TOP_K) // tm) + n_exp
    np_rows = n_tiles * tm
    tile_ids = jnp.arange(n_tiles, dtype=jnp.int32)
    used = tile_end[-1]
    tile_expert = jnp.sum((jnp.minimum(tile_ids, used - 1)[:, None] >= tile_end[None, :]).astype(jnp.int32), axis=1)
    tile_expert = jnp.minimum(tile_expert, n_exp - 1)
    onehot_e = tile_expert[:, None] == jnp.arange(n_exp, dtype=jnp.int32)
    tile_cnt = jnp.sum(jnp.where(onehot_e, cnt, 0), axis=1)
    tile_first = jnp.sum(jnp.where(onehot_e, tile_end - tiles_e, 0), axis=1)
    tile_rows = jnp.clip(tile_cnt - (tile_ids - tile_first) * tm, 0, tm)
    part = tm // EXPERT_TILE_PARTS
    tile_valid = jnp.where(tile_ids < used, (tile_rows + part - 1) // part, 0).astype(jnp.int32)
    pos_tiles = (pos * s).reshape(n // td, td, TOP_K).transpose(0, 2, 1).reshape(n // td, 1, TOP_K * td)
    xs = _dispatch(h, pos_tiles, np_rows, td, s)
    ys = _ffn_experts(xs, tile_expert, tile_valid, w1, w3, w2, tm)
    out = _combine(ys, pos_tiles, meta, x_flat, mod, norm_final, l, td)
    return out.reshape(bx, l, d)


def _mod_rows(m):
    r, n = m.shape
    return jnp.pad(m.reshape(r, N_MOD, n // N_MOD), ((0, 0), (0, MOD_ROWS - N_MOD), (0, 0)))


def _hyena_mixer(x, mod, gnorm, in_w, in_b, sc_w, sc_b, spectra, f_bias, ct_pref):
    zu = _hyena_in(x, mod, gnorm, in_w, in_b, sc_w, sc_b, _pick(x.shape[2], ct_pref))
    return _hyena_conv(zu, spectra, f_bias)


@jax.jit
def kernel(x, c, ctx, c_ctx, ada_w, ada_b, norm_mix, norm_ffn, hy_in_w, hy_in_b, hy_sc_w, hy_sc_b, hy_f_w0, hy_f_b0, hy_f_wi, hy_f_bi, hy_f_freq, hy_f_wout, hy_f_bias, hy_out_w, hy_out_b, mla_wq_a, mla_q_norm, mla_wq_b, mla_wkv_a, mla_kv_norm, mla_wkv_b, mla_wo, ffn_w1, ffn_w3, ffn_w2, moe_router, moe_w1, moe_w3, moe_w2, norm_final):
    bsz, l, d = x.shape
    n_ctx = ctx.shape[1]
    depth = ada_w.shape[0]
    assert depth == 2, "layer 0 = Hyena + dense SwiGLU, layer 1 = MLA + expert SwiGLU"

    rows = -(-(bsz + 1) // 8) * 8
    cvec = jnp.zeros((rows, d), F32).at[:bsz].set(c).at[bsz].set(c_ctx)
    mods = _ada_mod(cvec, ada_w, ada_b)
    modx = [_mod_rows(mods[i, :bsz]) for i in range(depth)]
    modc = [_mod_rows(mods[i, bsz:bsz + 1]) for i in range(depth)]

    in_w = hy_in_w[0].astype(BF16)
    out_w = hy_out_w[0].astype(BF16)
    fargs = (hy_f_w0[0], hy_f_b0[0], hy_f_wi[0], hy_f_bi[0], hy_f_freq[0], hy_f_wout[0])
    kx = _hyena_filter_spectra(l, d, *fargs)
    kc = _hyena_filter_spectra(n_ctx, d, *fargs)
    gx = _hyena_mixer(x, modx[0], norm_mix[0], in_w, hy_in_b[0], hy_sc_w[0], hy_sc_b[0], kx, hy_f_bias[0], 256)
    gc = _hyena_mixer(ctx, modc[0], norm_mix[0], in_w, hy_in_b[0], hy_sc_w[0], hy_sc_b[0], kc, hy_f_bias[0], 1024)
    ffn = (norm_ffn[0], ffn_w1[0], ffn_w3[0], ffn_w2[0])
    x = _mixer_proj_ffn_dense(x, gx, out_w, hy_out_b[0], modx[0], *ffn)
    ctx = _mixer_proj_ffn_dense(ctx.reshape(1, bsz * n_ctx, d), gc.reshape(1, bsz * n_ctx, d), out_w, hy_out_b[0],
                                modc[0], *ffn).reshape(bsz, n_ctx, d)

    wa, wqb, wk, wv = _mla_weights(mla_wq_a[0], mla_wq_b[0], mla_wkv_a[0], mla_wkv_b[0])
    q_lora = mla_q_norm.shape[1]
    q, kx, vx = _mla_latent(x, modx[1], norm_mix[1], wa, mla_q_norm[0], mla_kv_norm[0], wqb, wk, wv,
                            _rope_tables(l))
    kc, vc = _mla_context(ctx, modc[1], norm_mix[1], wa[:, q_lora:], mla_kv_norm[0], wk, wv)
    o = _attention(q.transpose(0, 2, 1), kc, kx, vc.transpose(0, 2, 1), vx.transpose(0, 2, 1))
    return _attn_proj_moe(x, o, mla_wo[0].astype(BF16), modx[1], norm_ffn[1], moe_router[0],
                          moe_w1[0], moe_w3[0], moe_w2[0], norm_final)
```

```python
import functools
import math

import jax
import jax.numpy as jnp
import numpy as np
from jax import lax
from jax.experimental import pallas as pl
from jax.experimental.pallas import tpu as pltpu

F32 = jnp.float32
BF16 = jnp.bfloat16
HIGHEST = lax.Precision.HIGHEST

RMS_EPS = 1e-6
N_MOD = 6
MOD_ROWS = 8
MOD_SH1, MOD_SC1, MOD_G1, MOD_SH2, MOD_SC2, MOD_G2 = range(N_MOD)
GRID_W = 64
SHORT_CONV = 3
FILTER_BANDS = 8
FILTER_EMB = 1 + 2 * FILTER_BANDS
FILTER_EMB_PAD = 32
DECAY_TARGET = 1e-2
FAST_DECAY_PCT = 0.3
SLOW_DECAY_PCT = 1.5
N_HEADS = 16
QK_NOPE = 64
QK_ROPE = 32
QK_HEAD = QK_NOPE + QK_ROPE
V_HEAD = 64
ROPE_AXIS = QK_ROPE // 2
ROPE_BASE = 10000.0
TOP_K = 2
LANES = 128
HEAD_SLOT = 128
VMEM_LIMIT = 56 * 1024 * 1024


def _cparams(sem, vmem=VMEM_LIMIT):
    return pltpu.CompilerParams(dimension_semantics=sem, vmem_limit_bytes=vmem)


def _rms(x, g):
    return x * lax.rsqrt(jnp.mean(x * x, axis=-1, keepdims=True) + RMS_EPS) * g


def _silu(x):
    return x * (1.0 / (1.0 + jnp.exp(-x)))


def _store_token_slabs(ref, value):
    rows, width = value.shape
    s = width // LANES
    for j in range(s):
        ref[pl.ds(j, rows, stride=s), :] = value[:, j * LANES:(j + 1) * LANES]


def _load_token_slab(ref, j, rows, s):
    return ref[pl.ds(j, rows, stride=s), :]


def _pick(total, pref):
    t = min(total, pref)
    while total % t:
        t //= 2
    return t


def _ada_kernel(c_ref, w_ref, b_ref, o_ref):
    c = c_ref[...]
    o_ref[0] = jnp.dot(_silu(c), w_ref[0], precision=HIGHEST, preferred_element_type=F32) + b_ref[0]


def _ada_mod(cvec, ada_w, ada_b):
    depth, d, n = ada_w.shape
    r = cvec.shape[0]
    tn = _pick(n, 1536)
    return pl.pallas_call(
        _ada_kernel,
        out_shape=jax.ShapeDtypeStruct((depth, r, n), F32),
        grid=(depth, n // tn),
        in_specs=[pl.BlockSpec((r, d), lambda i, j: (0, 0)),
                  pl.BlockSpec((1, d, tn), lambda i, j: (i, 0, j)),
                  pl.BlockSpec((1, 1, tn), lambda i, j: (i, 0, j))],
        out_specs=pl.BlockSpec((1, r, tn), lambda i, j: (i, 0, j)),
        compiler_params=_cparams(("arbitrary", "arbitrary")),
        name="ada_mod",
    )(cvec, ada_w, ada_b.reshape(depth, 1, n))


HALO = 8
IN_PROJ_STAGES = 3


def _hyena_in_kernel(x_ref, xp_ref, xn_ref, mod_ref, g_ref, w_ref, b_ref, cw_ref, cb_ref, o_ref, z_ref, *, ct):
    i = pl.program_id(1)
    tm = x_ref.shape[1]
    d = x_ref.shape[2]
    mod = mod_ref[0]
    xe = jnp.concatenate([xp_ref[0], x_ref[0], xn_ref[0]], axis=0)
    hb = (_rms(xe, g_ref[...]) * (1.0 + mod[MOD_SC1:MOD_SC1 + 1]) + mod[MOD_SH1:MOD_SH1 + 1]).astype(BF16)
    is_first = i == 0
    is_last = i == pl.num_programs(1) - 1

    slots = z_ref.shape[0]
    calls = [0]

    def conv_cols(c0):
        cols = slice(c0, c0 + ct)
        zs = z_ref.at[calls[0] % slots]
        calls[0] += 1
        zs[...] = jnp.dot(hb, w_ref[:, cols], preferred_element_type=F32) + b_ref[:, cols]
        zs[0:HALO, :] = jnp.where(is_first, 0.0, zs[0:HALO, :])
        zs[tm + HALO:, :] = jnp.where(is_last, 0.0, zs[tm + HALO:, :])
        return (zs[HALO - 1:HALO - 1 + tm, :] * cw_ref[0:1, cols] + zs[HALO:HALO + tm, :] * cw_ref[1:2, cols]
                + zs[HALO + 1:HALO + 1 + tm, :] * cw_ref[2:3, cols] + cb_ref[:, cols])

    nct = d // ct
    for j in range(nct):
        o_ref[j, 0] = conv_cols(j * ct)
        o_ref[nct + j, 0] = conv_cols(2 * d + j * ct) * conv_cols(d + j * ct)


def _hyena_in(x, mod, gnorm, w_bf16, bias, sc_w, sc_b, ct, tm_pref=512):
    bx, l, d = x.shape
    n = w_bf16.shape[1]
    tm = _pick(l, tm_pref)
    tpb = tm // HALO
    last = l // HALO - 1
    per_b = mod.shape[0] > 1
    return pl.pallas_call(
        functools.partial(_hyena_in_kernel, ct=ct),
        out_shape=jax.ShapeDtypeStruct((2 * d // ct, bx, l, ct), F32),
        grid=(bx, l // tm),
        in_specs=[pl.BlockSpec((1, tm, d), lambda b, i: (b, i, 0)),
                  pl.BlockSpec((1, HALO, d), lambda b, i: (b, jnp.maximum(i * tpb - 1, 0), 0)),
                  pl.BlockSpec((1, HALO, d), lambda b, i: (b, jnp.minimum((i + 1) * tpb, last), 0)),
                  pl.BlockSpec((1, MOD_ROWS, d), (lambda b, i: (b, 0, 0)) if per_b else (lambda b, i: (0, 0, 0))),
                  pl.BlockSpec((1, d), lambda b, i: (0, 0)),
                  pl.BlockSpec((d, n), lambda b, i: (0, 0)),
                  pl.BlockSpec((1, n), lambda b, i: (0, 0)),
                  pl.BlockSpec((SHORT_CONV, n), lambda b, i: (0, 0)),
                  pl.BlockSpec((1, n), lambda b, i: (0, 0))],
        out_specs=pl.BlockSpec((2 * d // ct, 1, tm, ct), lambda b, i: (0, b, i, 0)),
        scratch_shapes=[pltpu.VMEM((IN_PROJ_STAGES, tm + 2 * HALO, ct), F32)],
        compiler_params=_cparams(("arbitrary", "arbitrary")),
        name="hyena_in_proj",
    )(x, x, x, mod, gnorm.reshape(1, d), w_bf16, bias.reshape(1, n), sc_w, sc_b.reshape(1, n))


def _filter_kernel(z_ref, w0_ref, b0_ref, wi_ref, bi_ref, fr_ref, wt_ref, wb_ref, dl_ref, fwd_ref, o_ref, h_ref,
                   *, l, p):
    @pl.when(pl.program_id(0) == 0)
    def _():
        fr = fr_ref[...]
        h = jnp.sin(fr * (jnp.dot(z_ref[...], w0_ref[...], precision=HIGHEST, preferred_element_type=F32)
                          + b0_ref[...]))
        for n in range(wi_ref.shape[0]):
            h = jnp.sin(fr * (jnp.dot(h, wi_ref[n], precision=HIGHEST, preferred_element_type=F32) + bi_ref[n]))
        h_ref[...] = h

    top = jnp.dot(h_ref[:l], wt_ref[...], precision=HIGHEST, preferred_element_type=F32)
    bot = jnp.dot(h_ref[l:], wb_ref[...], precision=HIGHEST, preferred_element_type=F32)
    t = z_ref[:, 0:1]
    rows = lax.broadcasted_iota(jnp.int32, (2 * l, 1), 0)
    decay = jnp.where(rows == l, 0.0, jnp.exp(-t * dl_ref[...]))
    k = jnp.concatenate([top, bot], axis=0) * decay
    k = k / jnp.sum(jnp.abs(k), axis=0, keepdims=True)
    nb = l // p
    for di in range(2 * nb - 1):
        start = (p * (di - nb)) % (2 * l)
        if start + 2 * p <= 2 * l:
            seg = k[start:start + 2 * p]
        else:
            seg = jnp.concatenate([k[start:], k[:start + 2 * p - 2 * l]], axis=0)
        o_ref[di] = jnp.dot(fwd_ref[...], seg.astype(BF16), preferred_element_type=F32)


def _conv_block(l, p_pref=512):
    return min(p_pref, l)


def _hyena_filter_spectra(l, d, w0, b0, wi, bi, freq, wout):
    p = _conv_block(l)
    nd = 2 * (l // p) - 1
    pos = jnp.arange(l, dtype=F32)
    t = (pos / max(l - 1, 1))[:, None]
    w = 2.0 * math.pi * pos / l
    f = jnp.linspace(1e-4, FILTER_BANDS - 1, FILTER_BANDS, dtype=F32)
    ang = w[:, None] * f[None, :]
    z = jnp.concatenate([t, jnp.cos(ang), -jnp.sin(ang)], axis=-1)
    deltas = jnp.abs(jnp.linspace(math.log(DECAY_TARGET) / SLOW_DECAY_PCT,
                                  math.log(DECAY_TARGET) / FAST_DECAY_PCT, d, dtype=F32))
    idx = np.concatenate([np.arange(l), [0], np.arange(l - 1, 0, -1)])
    zc = jnp.pad(z[idx], ((0, 0), (0, FILTER_EMB_PAD - FILTER_EMB)))
    w0p = jnp.pad(w0, ((0, FILTER_EMB_PAD - FILTER_EMB), (0, 0)))
    hid = w0.shape[1]
    n_in = wi.shape[0]
    ct = _pick(d, 256)
    nct = d // ct
    fwd = jnp.asarray(_dft_mats(p)[0]).astype(BF16)
    return pl.pallas_call(
        functools.partial(_filter_kernel, l=l, p=p),
        out_shape=jax.ShapeDtypeStruct((nd, 2 * p, d), F32),
        grid=(nct,),
        in_specs=[pl.BlockSpec((2 * l, FILTER_EMB_PAD), lambda j: (0, 0)),
                  pl.BlockSpec((FILTER_EMB_PAD, hid), lambda j: (0, 0)),
                  pl.BlockSpec((1, hid), lambda j: (0, 0)),
                  pl.BlockSpec((n_in, hid, hid), lambda j: (0, 0, 0)),
                  pl.BlockSpec((n_in, 1, hid), lambda j: (0, 0, 0)),
                  pl.BlockSpec((1, hid), lambda j: (0, 0)),
                  pl.BlockSpec((hid, ct), lambda j: (0, j)),
                  pl.BlockSpec((hid, ct), lambda j: (0, nct + j)),
                  pl.BlockSpec((1, ct), lambda j: (0, j)),
                  pl.BlockSpec((2 * p, 2 * p), lambda j: (0, 0))],
        out_specs=pl.BlockSpec((nd, 2 * p, ct), lambda j: (0, 0, j)),
        scratch_shapes=[pltpu.VMEM((2 * l, hid), F32)],
        compiler_params=_cparams(("arbitrary",)),
        name="hyena_filter",
    )(zc, w0p, b0.reshape(1, hid), wi, bi.reshape(n_in, 1, hid), freq.reshape(1, hid), wout, wout,
      deltas.reshape(1, d), fwd)


@functools.lru_cache(maxsize=None)
def _dft_mats(p):
    n = 2 * p
    f = np.arange(p)[:, None]
    t = np.arange(n)[None, :]
    ang = 2.0 * np.pi * (((2 * f + 1) * t) % (4 * p)) / (4 * p)
    fwd = np.concatenate([np.cos(ang), -np.sin(ang)], axis=0)
    q = np.arange(p)[:, None]
    ff = np.arange(p)[None, :]
    ang2 = 2.0 * np.pi * (((2 * ff + 1) * (q + p)) % (4 * p)) / (4 * p)
    inv = np.concatenate([np.cos(ang2), -np.sin(ang2)], axis=1) / p
    return fwd.astype(np.float32), inv.astype(np.float32)


CONV_INVERSE_PIECES = 4


CONV_STAGES = 3


def _hyena_conv_kernel(uf_ref, x0_ref, u_ref, ks_ref, fb_ref, fwd_ref, inv_ref, o_ref, vs_ref, ys_ref,
                       *, l, p, rc):
    nb = l // p
    s = pl.program_id(0)

    @pl.when(s == 0)
    def _():
        vs_ref[...] = jnp.zeros_like(vs_ref)
        ys_ref[...] = jnp.zeros_like(ys_ref)

    fb = fb_ref[...]
    pieces = CONV_INVERSE_PIECES
    pr = p // pieces
    chunks = p // rc
    per_piece = chunks // pieces

    def step(cur):
        prv = 1 - cur

        def forward(j):
            vs_ref[cur, j] = jnp.dot(fwd_ref[...], uf_ref[0, 0, j * p:(j + 1) * p, :].astype(BF16),
                                     preferred_element_type=F32)

        def spectrum_rows(i, c):
            re = slice(c * rc, (c + 1) * rc)
            im = slice(p + c * rc, p + (c + 1) * rc)
            yr = None
            yi = None
            for j in range(nb):
                di = i - j + nb - 1
                kr = ks_ref[di, re, :]
                ki = ks_ref[di, im, :]
                vr = vs_ref[prv, j, re, :]
                vi = vs_ref[prv, j, im, :]
                tr = kr * vr - ki * vi
                ti = kr * vi + ki * vr
                yr = tr if yr is None else yr + tr
                yi = ti if yi is None else yi + ti
            ys_ref[prv, i, re, :] = yr
            ys_ref[prv, i, im, :] = yi

        spectra_bf16 = {}

        def inverse_piece(i, q):
            if i not in spectra_bf16:
                spectra_bf16[i] = ys_ref[cur, i].astype(BF16)
            y = jnp.dot(inv_ref[q * pr:(q + 1) * pr, :], spectra_bf16[i], preferred_element_type=F32)
            sl = slice(i * p + q * pr, i * p + (q + 1) * pr)
            o_ref[0, sl, :] = (x0_ref[0, 0, sl, :] * (y + u_ref[0, 0, sl, :] * fb)).astype(BF16)

        for i in range(nb):
            forward(i)
            for q in range(pieces):
                for c in range(q * per_piece, (q + 1) * per_piece):
                    spectrum_rows(i, c)
                inverse_piece(i, q)

    parity = lax.rem(s, 2)
    for cur in range(2):
        pl.when(parity == cur)(functools.partial(step, cur))


def _hyena_conv(zu, spectra, f_bias):
    nz, bx, l, ct = zu.shape
    nct = nz // 2
    d = nct * ct
    nd, p2, _ = spectra.shape
    p = p2 // 2
    nb = l // p
    fwd, inv = _dft_mats(p)
    fwd = jnp.asarray(fwd[:, :p]).astype(BF16)
    inv = jnp.asarray(inv).astype(BF16)
    items = nct * bx
    tile = lambda t: t // bx
    seq = lambda t: lax.rem(t, bx)
    stage = lambda s, k: jnp.clip(s - k, 0, items - 1)
    once = pl.Buffered(1)
    return pl.pallas_call(
        functools.partial(_hyena_conv_kernel, l=l, p=p, rc=8),
        out_shape=jax.ShapeDtypeStruct((bx, l, d), BF16),
        grid=(items + CONV_STAGES - 1,),
        in_specs=[pl.BlockSpec((1, 1, l, ct), lambda s: (nct + tile(stage(s, 0)), seq(stage(s, 0)), 0, 0)),
                  pl.BlockSpec((1, 1, l, ct), lambda s: (tile(stage(s, 2)), seq(stage(s, 2)), 0, 0)),
                  pl.BlockSpec((1, 1, l, ct), lambda s: (nct + tile(stage(s, 2)), seq(stage(s, 2)), 0, 0)),
                  pl.BlockSpec((nd, 2 * p, ct), lambda s: (0, 0, tile(stage(s, 1))), pipeline_mode=once),
                  pl.BlockSpec((1, ct), lambda s: (0, tile(stage(s, 2)))),
                  pl.BlockSpec((2 * p, p), lambda s: (0, 0), pipeline_mode=once),
                  pl.BlockSpec((p, 2 * p), lambda s: (0, 0), pipeline_mode=once)],
        out_specs=pl.BlockSpec((1, l, ct), lambda s: (seq(stage(s, 2)), 0, tile(stage(s, 2)))),
        scratch_shapes=[pltpu.VMEM((2, nb, 2 * p, ct), F32),
                        pltpu.VMEM((2, nb, 2 * p, ct), F32)],
        compiler_params=_cparams(("arbitrary",)),
        name="hyena_conv",
    )(zu, zu, zu, spectra, f_bias.reshape(1, d), fwd, inv)


def _swiglu_step(h_ref, w1_ref, w3_ref, w2_ref, acc_ref, rows):
    h = h_ref[:rows]
    a = jnp.dot(h, w1_ref[0].astype(BF16), preferred_element_type=F32)
    b = jnp.dot(h, w3_ref[0].astype(BF16), preferred_element_type=F32)
    g = (_silu(a) * b).astype(BF16)
    acc_ref[:rows] += jnp.dot(g, w2_ref[0].astype(BF16), preferred_element_type=F32)


def _ffn_dense_kernel(x_ref, a_ref, pw_ref, pb_ref, mod_ref, g_ref, w1_ref, w3_ref, w2_ref, o_ref, h_ref, acc_ref):
    f = pl.program_id(2)

    @pl.when(f == 0)
    def _():
        mod = mod_ref[0]
        y = jnp.dot(a_ref[0], pw_ref[...], preferred_element_type=F32) + pb_ref[...]
        xr = x_ref[0] + mod[MOD_G1:MOD_G1 + 1] * y
        o_ref[0] = xr
        h = _rms(xr, g_ref[...]) * (1.0 + mod[MOD_SC2:MOD_SC2 + 1]) + mod[MOD_SH2:MOD_SH2 + 1]
        h_ref[...] = h.astype(BF16)
        acc_ref[...] = jnp.zeros_like(acc_ref)

    _swiglu_step(h_ref, w1_ref, w3_ref, w2_ref, acc_ref, h_ref.shape[0])

    @pl.when(f == pl.num_programs(2) - 1)
    def _():
        o_ref[0] = o_ref[0] + mod_ref[0][MOD_G2:MOD_G2 + 1] * acc_ref[...]


def _mixer_proj_ffn_dense(x, a_bf16, pw_bf16, pb, mod, gnorm, w1, w3, w2, tm_pref=1024, tf_pref=512):
    bx, l, d = x.shape
    k = a_bf16.shape[2]
    ff = w1.shape[1]
    tm = _pick(l, tm_pref)
    tf = _pick(ff, tf_pref)
    per_b = mod.shape[0] > 1
    once = pl.Buffered(1)
    return pl.pallas_call(
        _ffn_dense_kernel,
        out_shape=jax.ShapeDtypeStruct((bx, l, d), F32),
        grid=(bx, l // tm, ff // tf),
        in_specs=[pl.BlockSpec((1, tm, d), lambda b, i, f: (b, i, 0)),
                  pl.BlockSpec((1, tm, k), lambda b, i, f: (b, i, 0)),
                  pl.BlockSpec((k, d), lambda b, i, f: (0, 0), pipeline_mode=once),
                  pl.BlockSpec((1, d), lambda b, i, f: (0, 0)),
                  pl.BlockSpec((1, MOD_ROWS, d), (lambda b, i, f: (b, 0, 0)) if per_b else (lambda b, i, f: (0, 0, 0))),
                  pl.BlockSpec((1, d), lambda b, i, f: (0, 0)),
                  pl.BlockSpec((1, d, tf), lambda b, i, f: (0, 0, f)),
                  pl.BlockSpec((1, d, tf), lambda b, i, f: (0, 0, f)),
                  pl.BlockSpec((1, tf, d), lambda b, i, f: (0, f, 0))],
        out_specs=pl.BlockSpec((1, tm, d), lambda b, i, f: (b, i, 0)),
        scratch_shapes=[pltpu.VMEM((tm, d), BF16), pltpu.VMEM((tm, d), F32)],
        compiler_params=_cparams(("arbitrary", "arbitrary", "arbitrary")),
        name="ffn_dense",
    )(x, a_bf16, pw_bf16, pb.reshape(1, d), mod, gnorm.reshape(1, d), w1[None], w3[None], w2[None])


EXPERT_TILE_PARTS = 4


def _ffn_expert_kernel(te_ref, tp_ref, x_ref, w1_ref, w3_ref, w2_ref, o_ref, h_ref, acc_ref):
    i = pl.program_id(0)
    f = pl.program_id(1)
    tm, d = h_ref.shape
    parts = tp_ref[i]

    @pl.when(parts > 0)
    def _():
        @pl.when(f == 0)
        def _():
            s = d // LANES
            for j in range(s):
                h_ref[:, j * LANES:(j + 1) * LANES] = _load_token_slab(x_ref, j, tm, s).astype(BF16)
            acc_ref[...] = jnp.zeros_like(acc_ref)

        for q in range(1, EXPERT_TILE_PARTS + 1):
            @pl.when(parts == q)
            def _(q=q):
                _swiglu_step(h_ref, w1_ref, w3_ref, w2_ref, acc_ref, q * tm // EXPERT_TILE_PARTS)

        @pl.when(f == pl.num_programs(1) - 1)
        def _():
            _store_token_slabs(o_ref, acc_ref[...])

    @pl.when((parts == 0) & (f == 0))
    def _():
        o_ref[...] = jnp.zeros_like(o_ref)


def _ffn_experts(xs, tile_expert, tile_valid, w1, w3, w2, tm, tf_pref=512):
    d = w1.shape[1]
    s = d // LANES
    np_rows = xs.shape[0] // s
    ff = w1.shape[2]
    tf = _pick(ff, tf_pref)
    nf = ff // tf
    fsel = lambda i, f, tv: jnp.where(tv[i] > 0, f, nf - 1)
    grid_spec = pltpu.PrefetchScalarGridSpec(
        num_scalar_prefetch=2,
        grid=(np_rows // tm, nf),
        in_specs=[pl.BlockSpec((tm * s, LANES), lambda i, f, te, tv: (i, 0)),
                  pl.BlockSpec((1, d, tf), lambda i, f, te, tv: (te[i], 0, fsel(i, f, tv))),
                  pl.BlockSpec((1, d, tf), lambda i, f, te, tv: (te[i], 0, fsel(i, f, tv))),
                  pl.BlockSpec((1, tf, d), lambda i, f, te, tv: (te[i], fsel(i, f, tv), 0))],
        out_specs=pl.BlockSpec((tm * s, LANES), lambda i, f, te, tv: (i, 0)),
        scratch_shapes=[pltpu.VMEM((tm, d), BF16), pltpu.VMEM((tm, d), F32)],
    )
    return pl.pallas_call(
        _ffn_expert_kernel,
        out_shape=jax.ShapeDtypeStruct((np_rows * s, LANES), F32),
        grid_spec=grid_spec,
        compiler_params=_cparams(("arbitrary", "arbitrary")),
        name="ffn_experts",
    )(tile_expert, tile_valid, xs, w1, w3, w2)


def _mla_latent_kernel(x_ref, mod_ref, g_ref, wa_ref, qg_ref, kvg_ref, wqb_ref, wk_ref, wv_ref, tab_ref,
                       q_ref, k_ref, v_ref, *, q_lora, kv_lora):
    mod = mod_ref[0]
    h = _rms(x_ref[0], g_ref[...]) * (1.0 + mod[1:2]) + mod[0:1]
    a = jnp.dot(h.astype(BF16), wa_ref[...], preferred_element_type=F32)
    qn = _rms(a[:, :q_lora], qg_ref[...]).astype(BF16)
    cn = _rms(a[:, q_lora:q_lora + kv_lora], kvg_ref[...]).astype(BF16)
    kpe = a[:, q_lora + kv_lora:]
    tab = tab_ref[...]
    cq, s1q, s2q, ck, s1k, s2k = (tab[:, n * LANES:(n + 1) * LANES] for n in range(6))
    q = jnp.dot(qn, wqb_ref[...], preferred_element_type=F32)
    wq = q.shape[1]
    rep = wq // LANES
    q = (q * jnp.tile(cq, (1, rep)) + pltpu.roll(q, ROPE_AXIS, 1) * jnp.tile(s1q, (1, rep))
         + pltpu.roll(q, wq - ROPE_AXIS, 1) * jnp.tile(s2q, (1, rep)))
    q_ref[0] = q.astype(BF16)
    kr = kpe * ck + pltpu.roll(kpe, ROPE_AXIS, 1) * s1k + pltpu.roll(kpe, LANES - ROPE_AXIS, 1) * s2k
    k = jnp.dot(cn, wk_ref[...], preferred_element_type=F32) + jnp.tile(kr, (1, rep))
    k_ref[0] = k.astype(BF16)
    v_ref[0] = jnp.dot(cn, wv_ref[...], preferred_element_type=F32).astype(BF16)


def _mla_context_kernel(x_ref, mod_ref, g_ref, wa_ref, kvg_ref, wk_ref, wv_ref, k_ref, v_ref, *, kv_lora):
    mod = mod_ref[0]
    h = _rms(x_ref[0], g_ref[...]) * (1.0 + mod[1:2]) + mod[0:1]
    a = jnp.dot(h.astype(BF16), wa_ref[...], preferred_element_type=F32)
    cn = _rms(a[:, :kv_lora], kvg_ref[...]).astype(BF16)
    kpe = a[:, kv_lora:]
    rep = wk_ref.shape[1] // LANES
    k = jnp.dot(cn, wk_ref[...], preferred_element_type=F32) + jnp.tile(kpe, (1, rep))
    k_ref[0] = k.astype(BF16)
    v_ref[0] = jnp.dot(cn, wv_ref[...], preferred_element_type=F32).astype(BF16)


def _rope_tables(l):
    rows = l // GRID_W
    row = jnp.broadcast_to(jnp.arange(rows, dtype=F32)[:, None], (rows, GRID_W)).reshape(l)
    col = jnp.broadcast_to(jnp.arange(GRID_W, dtype=F32)[None, :], (rows, GRID_W)).reshape(l)
    inv = ROPE_BASE ** (-jnp.arange(0, ROPE_AXIS, 2, dtype=F32) / ROPE_AXIS)
    ang = jnp.concatenate([row[:, None] * inv, col[:, None] * inv], axis=-1)
    cos = jnp.cos(ang)
    sin = jnp.sin(ang)
    n = l
    ones = jnp.ones((n, QK_NOPE), F32)
    z16 = jnp.zeros((n, ROPE_AXIS), F32)
    z64 = jnp.zeros((n, QK_NOPE), F32)
    zpad = jnp.zeros((n, HEAD_SLOT - QK_HEAD), F32)
    c = jnp.concatenate([ones, cos, cos, zpad], axis=1)
    s1 = jnp.concatenate([z64, z16, sin, zpad], axis=1)
    s2 = jnp.concatenate([z64, -sin, z16, zpad], axis=1)
    scale = math.log2(math.e) / math.sqrt(QK_HEAD)
    return jnp.concatenate([c * scale, s1 * scale, s2 * scale, c, s1, s2], axis=1)


def _mla_weights(wq_a, wq_b, wkv_a, wkv_b):
    d, q_lora = wq_a.shape
    kv_lora = wkv_a.shape[1] - QK_ROPE
    ev = np.arange(0, QK_ROPE, 2)
    od = np.arange(1, QK_ROPE, 2)
    kpe = wkv_a[:, kv_lora:]
    kpe_slot = jnp.concatenate([jnp.zeros((d, QK_NOPE), F32), kpe[:, ev], kpe[:, od],
                                jnp.zeros((d, HEAD_SLOT - QK_HEAD), F32)], axis=1)
    wa = jnp.concatenate([wq_a, wkv_a[:, :kv_lora], kpe_slot], axis=1)
    qb = wq_b.reshape(q_lora, N_HEADS, QK_HEAD)
    qb = jnp.concatenate([qb[..., :QK_NOPE], qb[..., QK_NOPE + ev], qb[..., QK_NOPE + od],
                          jnp.zeros((q_lora, N_HEADS, HEAD_SLOT - QK_HEAD), F32)], axis=-1)
    kvb = wkv_b.reshape(kv_lora, N_HEADS, QK_NOPE + V_HEAD)
    wk = jnp.concatenate([kvb[..., :QK_NOPE], jnp.zeros((kv_lora, N_HEADS, HEAD_SLOT - QK_NOPE), F32)], axis=-1)
    wv = kvb[..., QK_NOPE:]
    return (wa.astype(BF16), qb.reshape(q_lora, N_HEADS * HEAD_SLOT).astype(BF16),
            wk.reshape(kv_lora, N_HEADS * HEAD_SLOT).astype(BF16),
            wv.reshape(kv_lora, N_HEADS * V_HEAD).astype(BF16))


def _mod_spec(mod):
    per_b = mod.shape[0] > 1
    return pl.BlockSpec((1, MOD_ROWS, mod.shape[2]), (lambda b, i: (b, 0, 0)) if per_b else (lambda b, i: (0, 0, 0)))


def _mla_latent(x, mod, gnorm, wa, q_norm, kv_norm, wqb, wk, wv, tables, tm_pref=512):
    bx, n, d = x.shape
    tm = _pick(n, tm_pref)
    q_lora = q_norm.shape[0]
    kv_lora = kv_norm.shape[0]
    wq = wqb.shape[1]
    wvn = wv.shape[1]
    const = lambda b, i: (0, 0)
    row = lambda width: pl.BlockSpec((1, tm, width), lambda b, i: (b, i, 0))
    return pl.pallas_call(
        functools.partial(_mla_latent_kernel, q_lora=q_lora, kv_lora=kv_lora),
        out_shape=(jax.ShapeDtypeStruct((bx, n, wq), BF16),
                   jax.ShapeDtypeStruct((bx, n, wq), BF16),
                   jax.ShapeDtypeStruct((bx, n, wvn), BF16)),
        grid=(bx, n // tm),
        in_specs=[row(d), _mod_spec(mod),
                  pl.BlockSpec((1, d), const),
                  pl.BlockSpec(wa.shape, const),
                  pl.BlockSpec((1, q_lora), const),
                  pl.BlockSpec((1, kv_lora), const),
                  pl.BlockSpec(wqb.shape, const),
                  pl.BlockSpec(wk.shape, const),
                  pl.BlockSpec(wv.shape, const),
                  pl.BlockSpec((tm, 6 * LANES), lambda b, i: (i, 0))],
        out_specs=(row(wq), row(wq), row(wvn)),
        compiler_params=_cparams(("arbitrary", "arbitrary")),
        name="mla_latent_proj",
    )(x, mod, gnorm.reshape(1, d), wa, q_norm.reshape(1, q_lora), kv_norm.reshape(1, kv_lora),
      wqb, wk, wv, tables)


def _mla_context(ctx, mod, gnorm, wa_kv, kv_norm, wk, wv, tm_pref=512):
    bx, n, d = ctx.shape
    tm = _pick(n, tm_pref)
    kv_lora = kv_norm.shape[0]
    wkn = wk.shape[1]
    wvn = wv.shape[1]
    const = lambda b, i: (0, 0)
    row = lambda width: pl.BlockSpec((1, tm, width), lambda b, i: (b, i, 0))
    return pl.pallas_call(
        functools.partial(_mla_context_kernel, kv_lora=kv_lora),
        out_shape=(jax.ShapeDtypeStruct((bx, n, wkn), BF16),
                   jax.ShapeDtypeStruct((bx, n, wvn), BF16)),
        grid=(bx, n // tm),
        in_specs=[row(d), _mod_spec(mod),
                  pl.BlockSpec((1, d), const),
                  pl.BlockSpec(wa_kv.shape, const),
                  pl.BlockSpec((1, kv_lora), const),
                  pl.BlockSpec(wk.shape, const),
                  pl.BlockSpec(wv.shape, const)],
        out_specs=(row(wkn), row(wvn)),
        compiler_params=_cparams(("arbitrary", "arbitrary")),
        name="mla_context_proj",
    )(ctx, mod, gnorm.reshape(1, d), wa_kv, kv_norm.reshape(1, kv_lora), wk, wv)


ATTN_HEADS_PER_STEP = 16
ATTN_KEY_CHUNK = 256
ATTN_SOFTMAX_LAG = 2
ATTN_PV_LAG = 6
SUM_ROWS = 16


def _attn_kernel(qt_ref, kc_ref, kx_ref, vct_ref, vxt_ref, o_ref, *, heads, ck):
    chunks = ([(kc_ref, vct_ref, c) for c in range(kc_ref.shape[1] // ck)]
              + [(kx_ref, vxt_ref, c) for c in range(kx_ref.shape[1] // ck)])
    items = [(h, ci) for ci in range(len(chunks)) for h in range(heads)]
    state = {h: None for h in range(heads)}
    scores = {}
    probs = {}
    ones = jnp.ones((SUM_ROWS, ck), BF16)

    def qk(t):
        h, ci = items[t]
        kref, _, c = chunks[ci]
        hsl = slice(h * HEAD_SLOT, (h + 1) * HEAD_SLOT)
        scores[t] = jnp.dot(kref[0, c * ck:(c + 1) * ck, hsl], qt_ref[0, hsl, :], preferred_element_type=F32)

    def softmax(t):
        h, _ = items[t]
        s = scores.pop(t)
        mc = jnp.max(s, axis=0, keepdims=True)
        if state[h] is None:
            probs[t] = (jnp.exp2(s - mc).astype(BF16), None)
            state[h] = (mc, None)
        else:
            m, acc = state[h]
            m_new = jnp.maximum(m, mc)
            probs[t] = (jnp.exp2(s - m_new).astype(BF16), jnp.exp2(m - m_new))
            state[h] = (m_new, acc)

    def pv(t):
        h, ci = items[t]
        _, vref, c = chunks[ci]
        p, alpha = probs.pop(t)
        lhs = jnp.concatenate([vref[0, h * V_HEAD:(h + 1) * V_HEAD, c * ck:(c + 1) * ck], ones], axis=0)
        o = jnp.dot(lhs, p, preferred_element_type=F32)
        m, acc = state[h]
        state[h] = (m, o if acc is None else alpha * acc + o)

    n_items = len(items)
    for t in range(n_items + ATTN_PV_LAG):
        if t < n_items:
            qk(t)
        if ATTN_SOFTMAX_LAG <= t < n_items + ATTN_SOFTMAX_LAG:
            softmax(t - ATTN_SOFTMAX_LAG)
        if t >= ATTN_PV_LAG:
            pv(t - ATTN_PV_LAG)

    for hp in range(heads // 2):
        outs = [state[h][1][:V_HEAD] / state[h][1][V_HEAD:V_HEAD + 1] for h in (2 * hp, 2 * hp + 1)]
        pair = jnp.concatenate(outs, axis=0)
        o_ref[0, :, hp * 2 * V_HEAD:(hp + 1) * 2 * V_HEAD] = pair.T.astype(BF16)


def _attention(qt, kc, kx, vct, vxt, tq_pref=256, heads=ATTN_HEADS_PER_STEP):
    bx, _, l = qt.shape
    n_ctx = kc.shape[1]
    tq = _pick(l, tq_pref)
    ck = _pick(n_ctx, ATTN_KEY_CHUNK)
    qk_w = heads * HEAD_SLOT
    v_w = heads * V_HEAD
    return pl.pallas_call(
        functools.partial(_attn_kernel, heads=heads, ck=ck),
        out_shape=jax.ShapeDtypeStruct((bx, l, N_HEADS * V_HEAD), BF16),
        grid=(bx, N_HEADS // heads, l // tq),
        in_specs=[pl.BlockSpec((1, qk_w, tq), lambda b, g, i: (b, g, i)),
                  pl.BlockSpec((1, n_ctx, qk_w), lambda b, g, i: (b, 0, g)),
                  pl.BlockSpec((1, l, qk_w), lambda b, g, i: (b, 0, g)),
                  pl.BlockSpec((1, v_w, n_ctx), lambda b, g, i: (b, g, 0)),
                  pl.BlockSpec((1, v_w, l), lambda b, g, i: (b, g, 0))],
        out_specs=pl.BlockSpec((1, tq, v_w), lambda b, g, i: (b, i, g)),
        compiler_params=_cparams(("arbitrary", "arbitrary", "arbitrary")),
        name="mla_attention",
    )(qt, kc, kx, vct, vxt)


ROUTE_SUBTILES = 2


def _route_kernel(x_ref, a_ref, pw_ref, mod_ref, g_ref, r_ref, tri_ref, xr_ref, h_ref, meta_ref, cnt_ref,
                  carry_ref, *, n_exp):
    i = pl.program_id(0)

    @pl.when(i == 0)
    def _():
        carry_ref[...] = jnp.zeros_like(carry_ref)

    mod = mod_ref[0]
    gate = mod[MOD_G1:MOD_G1 + 1]
    tr, d = x_ref.shape
    rs = tri_ref.shape[0]
    s = d // LANES
    proj = [jnp.dot(a_ref[t * rs:(t + 1) * rs, :], pw_ref[...], preferred_element_type=F32) for t in range(tr // rs)]
    r = r_ref[...]
    r_hi = r.astype(BF16)
    r_lo = (r - r_hi.astype(F32)).astype(BF16)
    lane = lax.broadcasted_iota(jnp.int32, (rs, LANES), 1)
    lane_f = lane.astype(F32)
    neg = jnp.float32(-jnp.inf)
    carry = carry_ref[...]
    for t in range(tr // rs):
        rows = slice(t * rs, (t + 1) * rs)
        xr = x_ref[rows, :] + gate * proj[t]
        xr_ref[rows, :] = xr
        h = _rms(xr, g_ref[...]) * (1.0 + mod[MOD_SC2:MOD_SC2 + 1]) + mod[MOD_SH2:MOD_SH2 + 1]
        _store_token_slabs(h_ref.at[pl.ds(t * rs * s, rs * s)], h)
        h_hi = h.astype(BF16)
        h_lo = (h - h_hi.astype(F32)).astype(BF16)
        logits = (jnp.dot(h_hi, r_hi, preferred_element_type=F32) + jnp.dot(h_hi, r_lo, preferred_element_type=F32)
                  + jnp.dot(h_lo, r_hi, preferred_element_type=F32))
        lg = jnp.where(lane < n_exp, logits, neg)
        v1 = jnp.max(lg, axis=-1, keepdims=True)
        i1 = jnp.min(jnp.where(lg == v1, lane_f, float(LANES)), axis=-1, keepdims=True)
        oh1 = lane_f == i1
        lg2 = jnp.where(oh1, neg, lg)
        v2 = jnp.max(lg2, axis=-1, keepdims=True)
        i2 = jnp.min(jnp.where(lg2 == v2, lane_f, float(LANES)), axis=-1, keepdims=True)
        oh2 = lane_f == i2
        e = jnp.exp(v2 - v1)
        g1 = 1.0 / (1.0 + e)
        g2 = e / (1.0 + e)
        oh = jnp.where(oh1 | oh2, 1.0, 0.0)
        pref = jnp.dot(tri_ref[...], oh.astype(BF16), preferred_element_type=F32)
        excl = pref - oh + carry
        r1 = jnp.sum(jnp.where(oh1, excl, 0.0), axis=-1, keepdims=True)
        r2 = jnp.sum(jnp.where(oh2, excl, 0.0), axis=-1, keepdims=True)
        carry = carry + jnp.sum(oh, axis=0, keepdims=True)
        meta = jnp.where(lane == 0, i1, 0.0)
        meta = jnp.where(lane == 1, i2, meta)
        meta = jnp.where(lane == 2, g1, meta)
        meta = jnp.where(lane == 3, g2, meta)
        meta = jnp.where(lane == 4, r1, meta)
        meta = jnp.where(lane == 5, r2, meta)
        meta_ref[rows, :] = meta
    carry_ref[...] = carry
    cnt_ref[...] = carry


def _mixer_proj_route(x_flat, a_flat, pw_bf16, mod, gnorm, router, tokens_per_batch, tr_pref=512):
    n, d = x_flat.shape
    k = a_flat.shape[1]
    n_exp = router.shape[1]
    tr = _pick(tokens_per_batch, tr_pref)
    per_b = tokens_per_batch // tr
    rpad = jnp.pad(router, ((0, 0), (0, LANES - n_exp)))
    rs = tr // ROUTE_SUBTILES
    tri = jnp.asarray(np.tril(np.ones((rs, rs), np.float32))).astype(BF16)
    return pl.pallas_call(
        functools.partial(_route_kernel, n_exp=n_exp),
        out_shape=(jax.ShapeDtypeStruct((n, d), F32),
                   jax.ShapeDtypeStruct((n * (d // LANES), LANES), F32),
                   jax.ShapeDtypeStruct((n, LANES), F32),
                   jax.ShapeDtypeStruct((1, LANES), F32)),
        grid=(n // tr,),
        in_specs=[pl.BlockSpec((tr, d), lambda i: (i, 0)),
                  pl.BlockSpec((tr, k), lambda i: (i, 0)),
                  pl.BlockSpec((k, d), lambda i: (0, 0)),
                  pl.BlockSpec((1, MOD_ROWS, d), lambda i: (i // per_b, 0, 0)),
                  pl.BlockSpec((1, d), lambda i: (0, 0)),
                  pl.BlockSpec((d, LANES), lambda i: (0, 0)),
                  pl.BlockSpec((rs, rs), lambda i: (0, 0))],
        out_specs=(pl.BlockSpec((tr, d), lambda i: (i, 0)),
                   pl.BlockSpec((tr * (d // LANES), LANES), lambda i: (i, 0)),
                   pl.BlockSpec((tr, LANES), lambda i: (i, 0)),
                   pl.BlockSpec((1, LANES), lambda i: (0, 0))),
        scratch_shapes=[pltpu.VMEM((1, LANES), F32)],
        compiler_params=_cparams(("arbitrary",)),
        name="moe_route",
    )(x_flat, a_flat, pw_bf16, mod, gnorm.reshape(1, d), rpad, tri)


ROW_DMA_UNROLL = 8


def _wait_rows(any_ref, rows, sem):
    blk = any_ref.at[pl.ds(0, rows)]
    pltpu.make_async_copy(blk, blk, sem).wait()


def _dispatch_kernel(pos_ref, h_ref, xs_in_ref, xs_ref, sem, *, td, s):
    del xs_in_ref

    def issue(t, c):
        src = h_ref.at[pl.ds(pl.multiple_of(t * s, s), s)]
        for k in range(TOP_K):
            dst = xs_ref.at[pl.ds(pl.multiple_of(pos_ref[0, 0, k * td + t], s), s)]
            pltpu.make_async_copy(src, dst, sem).start(priority=k)
        return c

    lax.fori_loop(0, td, issue, 0, unroll=ROW_DMA_UNROLL)
    _wait_rows(xs_ref, TOP_K * td * s, sem)


def _dispatch(h, pos_tiles, np_rows, td, s):
    n = h.shape[0] // s
    xs0 = jnp.zeros((np_rows * s, LANES), F32)
    return pl.pallas_call(
        functools.partial(_dispatch_kernel, td=td, s=s),
        out_shape=jax.ShapeDtypeStruct((np_rows * s, LANES), F32),
        grid=(n // td,),
        in_specs=[pl.BlockSpec((1, 1, TOP_K * td), lambda i: (i, 0, 0), memory_space=pltpu.SMEM),
                  pl.BlockSpec((td * s, LANES), lambda i: (i, 0)),
                  pl.BlockSpec(memory_space=pl.ANY)],
        out_specs=pl.BlockSpec(memory_space=pl.ANY),
        scratch_shapes=[pltpu.SemaphoreType.DMA(())],
        input_output_aliases={2: 0},
        compiler_params=_cparams(("arbitrary",)),
        name="moe_dispatch",
    )(pos_tiles, h, xs0)


def _combine_kernel(pos_ref, posn_ref, ys_ref, meta_ref, x_ref, mod_ref, g_ref, o_ref, buf_ref, sems, *, td, s):
    i = pl.program_id(0)
    n = pl.num_programs(0)
    slot = lax.rem(i, 2)

    def gather(p_ref, sl):
        def issue(t, c):
            for k in range(TOP_K):
                src = ys_ref.at[pl.ds(pl.multiple_of(p_ref[0, 0, k * td + t], s), s)]
                dst = buf_ref.at[sl, k, pl.ds(pl.multiple_of(t * s, s), s)]
                pltpu.make_async_copy(src, dst, sems.at[sl]).start(priority=k)
            return c

        lax.fori_loop(0, td, issue, 0, unroll=ROW_DMA_UNROLL)

    @pl.when(i == 0)
    def _():
        gather(pos_ref, slot)

    @pl.when(i + 1 < n)
    def _():
        gather(posn_ref, 1 - slot)

    _wait_rows(ys_ref, TOP_K * td * s, sems.at[slot])
    meta = meta_ref[...]
    g0 = meta[:, 2:3]
    g1 = meta[:, 3:4]
    gate = mod_ref[0][5:6]
    xs = []
    ssq = jnp.zeros((td, 1), F32)
    for j in range(s):
        lanes = slice(j * LANES, (j + 1) * LANES)
        y = (g0 * _load_token_slab(buf_ref.at[slot, 0], j, td, s)
             + g1 * _load_token_slab(buf_ref.at[slot, 1], j, td, s))
        xj = x_ref[:, lanes] + gate[:, lanes] * y
        ssq = ssq + jnp.sum(xj * xj, axis=-1, keepdims=True)
        xs.append(xj)
    inv = lax.rsqrt(ssq / (s * LANES) + RMS_EPS)
    for j in range(s):
        lanes = slice(j * LANES, (j + 1) * LANES)
        o_ref[:, lanes] = xs[j] * inv * g_ref[:, lanes]


def _combine(ys, pos_tiles, meta, x_flat, mod, norm_final, tokens_per_batch, td):
    n, d = x_flat.shape
    s = d // LANES
    per_b = tokens_per_batch // td
    nt = n // td
    return pl.pallas_call(
        functools.partial(_combine_kernel, td=td, s=s),
        out_shape=jax.ShapeDtypeStruct((n, d), F32),
        grid=(nt,),
        in_specs=[pl.BlockSpec((1, 1, TOP_K * td), lambda i: (i, 0, 0), memory_space=pltpu.SMEM),
                  pl.BlockSpec((1, 1, TOP_K * td), lambda i: (jnp.minimum(i + 1, nt - 1), 0, 0),
                               memory_space=pltpu.SMEM),
                  pl.BlockSpec(memory_space=pl.ANY),
                  pl.BlockSpec((td, LANES), lambda i: (i, 0)),
                  pl.BlockSpec((td, d), lambda i: (i, 0)),
                  pl.BlockSpec((1, MOD_ROWS, d), lambda i: (i // per_b, 0, 0)),
                  pl.BlockSpec((1, d), lambda i: (0, 0))],
        out_specs=pl.BlockSpec((td, d), lambda i: (i, 0)),
        scratch_shapes=[pltpu.VMEM((2, TOP_K, td * s, LANES), F32), pltpu.SemaphoreType.DMA((2,))],
        compiler_params=_cparams(("arbitrary",)),
        name="moe_combine",
    )(pos_tiles, pos_tiles, ys, meta, x_flat, mod, norm_final.reshape(1, d))


EXPERT_TILE_ROWS = 1024


def _attn_proj_moe(x, o, wo_bf16, mod, gnorm, router, w1, w3, w2, norm_final, td=256):
    bx, l, d = x.shape
    n = bx * l
    n_exp = router.shape[1]
    s = d // LANES
    tm = min(EXPERT_TILE_ROWS, n * TOP_K)
    x_flat, h, meta, counts = _mixer_proj_route(x.reshape(n, d), o.reshape(n, o.shape[2]), wo_bf16, mod, gnorm,
                                                router, l)
    idx = meta[:, 0:TOP_K].astype(jnp.int32)
    rank = meta[:, 4:4 + TOP_K].astype(jnp.int32)
    cnt = counts[0, :n_exp].astype(jnp.int32)
    tiles_e = (cnt + tm - 1) // tm
    tile_end = jnp.cumsum(tiles_e)
    start_rows = (tile_end - tiles_e) * tm
    sel = idx[..., None] == jnp.arange(n_exp, dtype=jnp.int32)
    pos = jnp.sum(jnp.where(sel, start_rows, 0), axis=-1) + rank
    n_tiles = -(-(n * TOP_K) // tm) + n_exp
    np_rows = n_tiles * tm
    tile_ids = jnp.arange(n_tiles, dtype=jnp.int32)
    used = tile_end[-1]
    tile_expert = jnp.sum((jnp.minimum(tile_ids, used - 1)[:, None] >= tile_end[None, :]).astype(jnp.int32), axis=1)
    tile_expert = jnp.minimum(tile_expert, n_exp - 1)
    onehot_e = tile_expert[:, None] == jnp.arange(n_exp, dtype=jnp.int32)
    tile_cnt = jnp.sum(jnp.where(onehot_e, cnt, 0), axis=1)
    tile_first = jnp.sum(jnp.where(onehot_e, tile_end - tiles_e, 0), axis=1)
    tile_rows = jnp.clip(tile_cnt - (tile_ids - tile_first) * tm, 0, tm)
    part = tm // EXPERT_TILE_PARTS
    tile_valid = jnp.where(tile_ids < used, (tile_rows + part - 1) // part, 0).astype(jnp.int32)
    pos_tiles = (pos * s).reshape(n // td, td, TOP_K).transpose(0, 2, 1).reshape(n // td, 1, TOP_K * td)
    xs = _dispatch(h, pos_tiles, np_rows, td, s)
    ys = _ffn_experts(xs, tile_expert, tile_valid, w1, w3, w2, tm)
    out = _combine(ys, pos_tiles, meta, x_flat, mod, norm_final, l, td)
    return out.reshape(bx, l, d)


def _mod_rows(m):
    r, n = m.shape
    return jnp.pad(m.reshape(r, N_MOD, n // N_MOD), ((0, 0), (0, MOD_ROWS - N_MOD), (0, 0)))


def _hyena_mixer(x, mod, gnorm, in_w, in_b, sc_w, sc_b, spectra, f_bias, ct_pref):
    zu = _hyena_in(x, mod, gnorm, in_w, in_b, sc_w, sc_b, _pick(x.shape[2], ct_pref))
    return _hyena_conv(zu, spectra, f_bias)


@jax.jit
def kernel(x, c, ctx, c_ctx, ada_w, ada_b, norm_mix, norm_ffn, hy_in_w, hy_in_b, hy_sc_w, hy_sc_b, hy_f_w0, hy_f_b0, hy_f_wi, hy_f_bi, hy_f_freq, hy_f_wout, hy_f_bias, hy_out_w, hy_out_b, mla_wq_a, mla_q_norm, mla_wq_b, mla_wkv_a, mla_kv_norm, mla_wkv_b, mla_wo, ffn_w1, ffn_w3, ffn_w2, moe_router, moe_w1, moe_w3, moe_w2, norm_final):
    bsz, l, d = x.shape
    n_ctx = ctx.shape[1]
    depth = ada_w.shape[0]
    assert depth == 2, "layer 0 = Hyena + dense SwiGLU, layer 1 = MLA + expert SwiGLU"

    rows = -(-(bsz + 1) // 8) * 8
    cvec = jnp.zeros((rows, d), F32).at[:bsz].set(c).at[bsz].set(c_ctx)
    mods = _ada_mod(cvec, ada_w, ada_b)
    modx = [_mod_rows(mods[i, :bsz]) for i in range(depth)]
    modc = [_mod_rows(mods[i, bsz:bsz + 1]) for i in range(depth)]

    in_w = hy_in_w[0].astype(BF16)
    out_w = hy_out_w[0].astype(BF16)
    fargs = (hy_f_w0[0], hy_f_b0[0], hy_f_wi[0], hy_f_bi[0], hy_f_freq[0], hy_f_wout[0])
    kx = _hyena_filter_spectra(l, d, *fargs)
    kc = _hyena_filter_spectra(n_ctx, d, *fargs)
    gx = _hyena_mixer(x, modx[0], norm_mix[0], in_w, hy_in_b[0], hy_sc_w[0], hy_sc_b[0], kx, hy_f_bias[0], 256)
    gc = _hyena_mixer(ctx, modc[0], norm_mix[0], in_w, hy_in_b[0], hy_sc_w[0], hy_sc_b[0], kc, hy_f_bias[0], 1024)
    ffn = (norm_ffn[0], ffn_w1[0], ffn_w3[0], ffn_w2[0])
    x = _mixer_proj_ffn_dense(x, gx, out_w, hy_out_b[0], modx[0], *ffn)
    ctx = _mixer_proj_ffn_dense(ctx.reshape(1, bsz * n_ctx, d), gc.reshape(1, bsz * n_ctx, d), out_w, hy_out_b[0],
                                modc[0], *ffn).reshape(bsz, n_ctx, d)

    wa, wqb, wk, wv = _mla_weights(mla_wq_a[0], mla_wq_b[0], mla_wkv_a[0], mla_wkv_b[0])
    q_lora = mla_q_norm.shape[1]
    q, kx, vx = _mla_latent(x, modx[1], norm_mix[1], wa, mla_q_norm[0], mla_kv_norm[0], wqb, wk, wv,
                            _rope_tables(l))
    kc, vc = _mla_context(ctx, modc[1], norm_mix[1], wa[:, q_lora:], mla_kv_norm[0], wk, wv)
    o = _attention(q.transpose(0, 2, 1), kc, kx, vc.transpose(0, 2, 1), vx.transpose(0, 2, 1))
    return _attn_proj_moe(x, o, mla_wo[0].astype(BF16), modx[1], norm_ffn[1], moe_router[0],
                          moe_w1[0], moe_w3[0], moe_w2[0], norm_final)
```

```python
import functools
import math

import jax
import jax.numpy as jnp
import numpy as np
from jax import lax
from jax.experimental import pallas as pl
from jax.experimental.pallas import tpu as pltpu

F32 = jnp.float32
BF16 = jnp.bfloat16
HIGHEST = lax.Precision.HIGHEST

RMS_EPS = 1e-6
N_MOD = 6
MOD_ROWS = 8
MOD_SH1, MOD_SC1, MOD_G1, MOD_SH2, MOD_SC2, MOD_G2 = range(N_MOD)
GRID_W = 64
SHORT_CONV = 3
FILTER_BANDS = 8
FILTER_EMB = 1 + 2 * FILTER_BANDS
FILTER_EMB_PAD = 32
DECAY_TARGET = 1e-2
FAST_DECAY_PCT = 0.3
SLOW_DECAY_PCT = 1.5
N_HEADS = 16
QK_NOPE = 64
QK_ROPE = 32
QK_HEAD = QK_NOPE + QK_ROPE
V_HEAD = 64
ROPE_AXIS = QK_ROPE // 2
ROPE_BASE = 10000.0
TOP_K = 2
LANES = 128
HEAD_SLOT = 128
VMEM_LIMIT = 56 * 1024 * 1024


def _cparams(sem, vmem=VMEM_LIMIT):
    return pltpu.CompilerParams(dimension_semantics=sem, vmem_limit_bytes=vmem)


def _rms(x, g):
    return x * lax.rsqrt(jnp.mean(x * x, axis=-1, keepdims=True) + RMS_EPS) * g


def _silu(x):
    return x * (1.0 / (1.0 + jnp.exp(-x)))


def _store_token_slabs(ref, value):
    rows, width = value.shape
    s = width // LANES
    for j in range(s):
        ref[pl.ds(j, rows, stride=s), :] = value[:, j * LANES:(j + 1) * LANES]


def _load_token_slab(ref, j, rows, s):
    return ref[pl.ds(j, rows, stride=s), :]


def _pick(total, pref):
    t = min(total, pref)
    while total % t:
        t //= 2
    return t


def _ada_kernel(c_ref, w_ref, b_ref, o_ref):
    c = c_ref[...]
    o_ref[0] = jnp.dot(_silu(c), w_ref[0], precision=HIGHEST, preferred_element_type=F32) + b_ref[0]


def _ada_mod(cvec, ada_w, ada_b):
    depth, d, n = ada_w.shape
    r = cvec.shape[0]
    tn = _pick(n, 1536)
    return pl.pallas_call(
        _ada_kernel,
        out_shape=jax.ShapeDtypeStruct((depth, r, n), F32),
        grid=(depth, n // tn),
        in_specs=[pl.BlockSpec((r, d), lambda i, j: (0, 0)),
                  pl.BlockSpec((1, d, tn), lambda i, j: (i, 0, j)),
                  pl.BlockSpec((1, 1, tn), lambda i, j: (i, 0, j))],
        out_specs=pl.BlockSpec((1, r, tn), lambda i, j: (i, 0, j)),
        compiler_params=_cparams(("arbitrary", "arbitrary")),
        name="ada_mod",
    )(cvec, ada_w, ada_b.reshape(depth, 1, n))


HALO = 8
IN_PROJ_STAGES = 3


def _hyena_in_kernel(x_ref, xp_ref, xn_ref, mod_ref, g_ref, w_ref, b_ref, cw_ref, cb_ref, o_ref, z_ref, *, ct):
    i = pl.program_id(1)
    tm = x_ref.shape[1]
    d = x_ref.shape[2]
    mod = mod_ref[0]
    xe = jnp.concatenate([xp_ref[0], x_ref[0], xn_ref[0]], axis=0)
    hb = (_rms(xe, g_ref[...]) * (1.0 + mod[MOD_SC1:MOD_SC1 + 1]) + mod[MOD_SH1:MOD_SH1 + 1]).astype(BF16)
    is_first = i == 0
    is_last = i == pl.num_programs(1) - 1

    slots = z_ref.shape[0]
    calls = [0]

    def conv_cols(c0):
        cols = slice(c0, c0 + ct)
        zs = z_ref.at[calls[0] % slots]
        calls[0] += 1
        zs[...] = jnp.dot(hb, w_ref[:, cols], preferred_element_type=F32) + b_ref[:, cols]
        zs[0:HALO, :] = jnp.where(is_first, 0.0, zs[0:HALO, :])
        zs[tm + HALO:, :] = jnp.where(is_last, 0.0, zs[tm + HALO:, :])
        return (zs[HALO - 1:HALO - 1 + tm, :] * cw_ref[0:1, cols] + zs[HALO:HALO + tm, :] * cw_ref[1:2, cols]
                + zs[HALO + 1:HALO + 1 + tm, :] * cw_ref[2:3, cols] + cb_ref[:, cols])

    nct = d // ct
    for j in range(nct):
        o_ref[j, 0] = conv_cols(j * ct)
        o_ref[nct + j, 0] = conv_cols(2 * d + j * ct) * conv_cols(d + j * ct)


def _hyena_in(x, mod, gnorm, w_bf16, bias, sc_w, sc_b, ct, tm_pref=512):
    bx, l, d = x.shape
    n = w_bf16.shape[1]
    tm = _pick(l, tm_pref)
    tpb = tm // HALO
    last = l // HALO - 1
    per_b = mod.shape[0] > 1
    return pl.pallas_call(
        functools.partial(_hyena_in_kernel, ct=ct),
        out_shape=jax.ShapeDtypeStruct((2 * d // ct, bx, l, ct), F32),
        grid=(bx, l // tm),
        in_specs=[pl.BlockSpec((1, tm, d), lambda b, i: (b, i, 0)),
                  pl.BlockSpec((1, HALO, d), lambda b, i: (b, jnp.maximum(i * tpb - 1, 0), 0)),
                  pl.BlockSpec((1, HALO, d), lambda b, i: (b, jnp.minimum((i + 1) * tpb, last), 0)),
                  pl.BlockSpec((1, MOD_ROWS, d), (lambda b, i: (b, 0, 0)) if per_b else (lambda b, i: (0, 0, 0))),
                  pl.BlockSpec((1, d), lambda b, i: (0, 0)),
                  pl.BlockSpec((d, n), lambda b, i: (0, 0)),
                  pl.BlockSpec((1, n), lambda b, i: (0, 0)),
                  pl.BlockSpec((SHORT_CONV, n), lambda b, i: (0, 0)),
                  pl.BlockSpec((1, n), lambda b, i: (0, 0))],
        out_specs=pl.BlockSpec((2 * d // ct, 1, tm, ct), lambda b, i: (0, b, i, 0)),
        scratch_shapes=[pltpu.VMEM((IN_PROJ_STAGES, tm + 2 * HALO, ct), F32)],
        compiler_params=_cparams(("arbitrary", "arbitrary")),
        name="hyena_in_proj",
    )(x, x, x, mod, gnorm.reshape(1, d), w_bf16, bias.reshape(1, n), sc_w, sc_b.reshape(1, n))


def _filter_kernel(z_ref, w0_ref, b0_ref, wi_ref, bi_ref, fr_ref, wt_ref, wb_ref, dl_ref, fwd_ref, o_ref, h_ref,
                   *, l, p):
    @pl.when(pl.program_id(0) == 0)
    def _():
        fr = fr_ref[...]
        h = jnp.sin(fr * (jnp.dot(z_ref[...], w0_ref[...], precision=HIGHEST, preferred_element_type=F32)
                          + b0_ref[...]))
        for n in range(wi_ref.shape[0]):
            h = jnp.sin(fr * (jnp.dot(h, wi_ref[n], precision=HIGHEST, preferred_element_type=F32) + bi_ref[n]))
        h_ref[...] = h

    top = jnp.dot(h_ref[:l], wt_ref[...], precision=HIGHEST, preferred_element_type=F32)
    bot = jnp.dot(h_ref[l:], wb_ref[...], precision=HIGHEST, preferred_element_type=F32)
    t = z_ref[:, 0:1]
    rows = lax.broadcasted_iota(jnp.int32, (2 * l, 1), 0)
    decay = jnp.where(rows == l, 0.0, jnp.exp(-t * dl_ref[...]))
    k = jnp.concatenate([top, bot], axis=0) * decay
    k = k / jnp.sum(jnp.abs(k), axis=0, keepdims=True)
    nb = l // p
    for di in range(2 * nb - 1):
        start = (p * (di - nb)) % (2 * l)
        if start + 2 * p <= 2 * l:
            seg = k[start:start + 2 * p]
        else:
            seg = jnp.concatenate([k[start:], k[:start + 2 * p - 2 * l]], axis=0)
        o_ref[di] = jnp.dot(fwd_ref[...], seg.astype(BF16), preferred_element_type=F32)


def _conv_block(l, p_pref=512):
    return min(p_pref, l)


def _hyena_filter_spectra(l, d, w0, b0, wi, bi, freq, wout):
    p = _conv_block(l)
    nd = 2 * (l // p) - 1
    pos = jnp.arange(l, dtype=F32)
    t = (pos / max(l - 1, 1))[:, None]
    w = 2.0 * math.pi * pos / l
    f = jnp.linspace(1e-4, FILTER_BANDS - 1, FILTER_BANDS, dtype=F32)
    ang = w[:, None] * f[None, :]
    z = jnp.concatenate([t, jnp.cos(ang), -jnp.sin(ang)], axis=-1)
    deltas = jnp.abs(jnp.linspace(math.log(DECAY_TARGET) / SLOW_DECAY_PCT,
                                  math.log(DECAY_TARGET) / FAST_DECAY_PCT, d, dtype=F32))
    idx = np.concatenate([np.arange(l), [0], np.arange(l - 1, 0, -1)])
    zc = jnp.pad(z[idx], ((0, 0), (0, FILTER_EMB_PAD - FILTER_EMB)))
    w0p = jnp.pad(w0, ((0, FILTER_EMB_PAD - FILTER_EMB), (0, 0)))
    hid = w0.shape[1]
    n_in = wi.shape[0]
    ct = _pick(d, 256)
    nct = d // ct
    fwd = jnp.asarray(_dft_mats(p)[0]).astype(BF16)
    return pl.pallas_call(
        functools.partial(_filter_kernel, l=l, p=p),
        out_shape=jax.ShapeDtypeStruct((nd, 2 * p, d), F32),
        grid=(nct,),
        in_specs=[pl.BlockSpec((2 * l, FILTER_EMB_PAD), lambda j: (0, 0)),
                  pl.BlockSpec((FILTER_EMB_PAD, hid), lambda j: (0, 0)),
                  pl.BlockSpec((1, hid), lambda j: (0, 0)),
                  pl.BlockSpec((n_in, hid, hid), lambda j: (0, 0, 0)),
                  pl.BlockSpec((n_in, 1, hid), lambda j: (0, 0, 0)),
                  pl.BlockSpec((1, hid), lambda j: (0, 0)),
                  pl.BlockSpec((hid, ct), lambda j: (0, j)),
                  pl.BlockSpec((hid, ct), lambda j: (0, nct + j)),
                  pl.BlockSpec((1, ct), lambda j: (0, j)),
                  pl.BlockSpec((2 * p, 2 * p), lambda j: (0, 0))],
        out_specs=pl.BlockSpec((nd, 2 * p, ct), lambda j: (0, 0, j)),
        scratch_shapes=[pltpu.VMEM((2 * l, hid), F32)],
        compiler_params=_cparams(("arbitrary",)),
        name="hyena_filter",
    )(zc, w0p, b0.reshape(1, hid), wi, bi.reshape(n_in, 1, hid), freq.reshape(1, hid), wout, wout,
      deltas.reshape(1, d), fwd)


@functools.lru_cache(maxsize=None)
def _dft_mats(p):
    n = 2 * p
    f = np.arange(p)[:, None]
    t = np.arange(n)[None, :]
    ang = 2.0 * np.pi * (((2 * f + 1) * t) % (4 * p)) / (4 * p)
    fwd = np.concatenate([np.cos(ang), -np.sin(ang)], axis=0)
    q = np.arange(p)[:, None]
    ff = np.arange(p)[None, :]
    ang2 = 2.0 * np.pi * (((2 * ff + 1) * (q + p)) % (4 * p)) / (4 * p)
    inv = np.concatenate([np.cos(ang2), -np.sin(ang2)], axis=1) / p
    return fwd.astype(np.float32), inv.astype(np.float32)


CONV_INVERSE_PIECES = 4


CONV_STAGES = 3


def _hyena_conv_kernel(uf_ref, x0_ref, u_ref, ks_ref, fb_ref, fwd_ref, inv_ref, o_ref, vs_ref, ys_ref,
                       *, l, p, rc):
    nb = l // p
    s = pl.program_id(0)

    @pl.when(s == 0)
    def _():
        vs_ref[...] = jnp.zeros_like(vs_ref)
        ys_ref[...] = jnp.zeros_like(ys_ref)

    fb = fb_ref[...]
    pieces = CONV_INVERSE_PIECES
    pr = p // pieces
    chunks = p // rc
    per_piece = chunks // pieces

    def step(cur):
        prv = 1 - cur

        def forward(j):
            vs_ref[cur, j] = jnp.dot(fwd_ref[...], uf_ref[0, 0, j * p:(j + 1) * p, :].astype(BF16),
                                     preferred_element_type=F32)

        def spectrum_rows(i, c):
            re = slice(c * rc, (c + 1) * rc)
            im = slice(p + c * rc, p + (c + 1) * rc)
            yr = None
            yi = None
            for j in range(nb):
                di = i - j + nb - 1
                kr = ks_ref[di, re, :]
                ki = ks_ref[di, im, :]
                vr = vs_ref[prv, j, re, :]
                vi = vs_ref[prv, j, im, :]
                tr = kr * vr - ki * vi
                ti = kr * vi + ki * vr
                yr = tr if yr is None else yr + tr
                yi = ti if yi is None else yi + ti
            ys_ref[prv, i, re, :] = yr
            ys_ref[prv, i, im, :] = yi

        spectra_bf16 = {}

        def inverse_piece(i, q):
            if i not in spectra_bf16:
                spectra_bf16[i] = ys_ref[cur, i].astype(BF16)
            y = jnp.dot(inv_ref[q * pr:(q + 1) * pr, :], spectra_bf16[i], preferred_element_type=F32)
            sl = slice(i * p + q * pr, i * p + (q + 1) * pr)
            o_ref[0, sl, :] = (x0_ref[0, 0, sl, :] * (y + u_ref[0, 0, sl, :] * fb)).astype(BF16)

        for i in range(nb):
            forward(i)
            for q in range(pieces):
                for c in range(q * per_piece, (q + 1) * per_piece):
                    spectrum_rows(i, c)
                inverse_piece(i, q)

    parity = lax.rem(s, 2)
    for cur in range(2):
        pl.when(parity == cur)(functools.partial(step, cur))


def _hyena_conv(zu, spectra, f_bias):
    nz, bx, l, ct = zu.shape
    nct = nz // 2
    d = nct * ct
    nd, p2, _ = spectra.shape
    p = p2 // 2
    nb = l // p
    fwd, inv = _dft_mats(p)
    fwd = jnp.asarray(fwd[:, :p]).astype(BF16)
    inv = jnp.asarray(inv).astype(BF16)
    items = nct * bx
    tile = lambda t: t // bx
    seq = lambda t: lax.rem(t, bx)
    stage = lambda s, k: jnp.clip(s - k, 0, items - 1)
    once = pl.Buffered(1)
    return pl.pallas_call(
        functools.partial(_hyena_conv_kernel, l=l, p=p, rc=8),
        out_shape=jax.ShapeDtypeStruct((bx, l, d), BF16),
        grid=(items + CONV_STAGES - 1,),
        in_specs=[pl.BlockSpec((1, 1, l, ct), lambda s: (nct + tile(stage(s, 0)), seq(stage(s, 0)), 0, 0)),
                  pl.BlockSpec((1, 1, l, ct), lambda s: (tile(stage(s, 2)), seq(stage(s, 2)), 0, 0)),
                  pl.BlockSpec((1, 1, l, ct), lambda s: (nct + tile(stage(s, 2)), seq(stage(s, 2)), 0, 0)),
                  pl.BlockSpec((nd, 2 * p, ct), lambda s: (0, 0, tile(stage(s, 1))), pipeline_mode=once),
                  pl.BlockSpec((1, ct), lambda s: (0, tile(stage(s, 2)))),
                  pl.BlockSpec((2 * p, p), lambda s: (0, 0), pipeline_mode=once),
                  pl.BlockSpec((p, 2 * p), lambda s: (0, 0), pipeline_mode=once)],
        out_specs=pl.BlockSpec((1, l, ct), lambda s: (seq(stage(s, 2)), 0, tile(stage(s, 2)))),
        scratch_shapes=[pltpu.VMEM((2, nb, 2 * p, ct), F32),
                        pltpu.VMEM((2, nb, 2 * p, ct), F32)],
        compiler_params=_cparams(("arbitrary",)),
        name="hyena_conv",
    )(zu, zu, zu, spectra, f_bias.reshape(1, d), fwd, inv)


def _swiglu_step(h_ref, w1_ref, w3_ref, w2_ref, acc_ref, rows):
    h = h_ref[:rows]
    a = jnp.dot(h, w1_ref[0].astype(BF16), preferred_element_type=F32)
    b = jnp.dot(h, w3_ref[0].astype(BF16), preferred_element_type=F32)
    g = (_silu(a) * b).astype(BF16)
    acc_ref[:rows] += jnp.dot(g, w2_ref[0].astype(BF16), preferred_element_type=F32)


def _ffn_dense_kernel(x_ref, a_ref, pw_ref, pb_ref, mod_ref, g_ref, w1_ref, w3_ref, w2_ref, o_ref, h_ref, acc_ref):
    f = pl.program_id(2)

    @pl.when(f == 0)
    def _():
        mod = mod_ref[0]
        tm = h_ref.shape[0]
        half = tm // 2
        proj = [jnp.dot(a_ref[0, t * half:(t + 1) * half, :], pw_ref[...], preferred_element_type=F32)
                for t in range(2)]
        for t in range(2):
            rows = slice(t * half, (t + 1) * half)
            xr = x_ref[0, rows, :] + mod[MOD_G1:MOD_G1 + 1] * (proj[t] + pb_ref[...])
            o_ref[0, rows, :] = xr
            h = _rms(xr, g_ref[...]) * (1.0 + mod[MOD_SC2:MOD_SC2 + 1]) + mod[MOD_SH2:MOD_SH2 + 1]
            h_ref[rows, :] = h.astype(BF16)
        acc_ref[...] = jnp.zeros_like(acc_ref)

    _swiglu_step(h_ref, w1_ref, w3_ref, w2_ref, acc_ref, h_ref.shape[0])

    @pl.when(f == pl.num_programs(2) - 1)
    def _():
        o_ref[0] = o_ref[0] + mod_ref[0][MOD_G2:MOD_G2 + 1] * acc_ref[...]


def _mixer_proj_ffn_dense(x, a_bf16, pw_bf16, pb, mod, gnorm, w1, w3, w2, tm_pref=1024, tf_pref=512):
    bx, l, d = x.shape
    k = a_bf16.shape[2]
    ff = w1.shape[1]
    tm = _pick(l, tm_pref)
    tf = _pick(ff, tf_pref)
    per_b = mod.shape[0] > 1
    once = pl.Buffered(1)
    return pl.pallas_call(
        _ffn_dense_kernel,
        out_shape=jax.ShapeDtypeStruct((bx, l, d), F32),
        grid=(bx, l // tm, ff // tf),
        in_specs=[pl.BlockSpec((1, tm, d), lambda b, i, f: (b, i, 0)),
                  pl.BlockSpec((1, tm, k), lambda b, i, f: (b, i, 0)),
                  pl.BlockSpec((k, d), lambda b, i, f: (0, 0), pipeline_mode=once),
                  pl.BlockSpec((1, d), lambda b, i, f: (0, 0)),
                  pl.BlockSpec((1, MOD_ROWS, d), (lambda b, i, f: (b, 0, 0)) if per_b else (lambda b, i, f: (0, 0, 0))),
                  pl.BlockSpec((1, d), lambda b, i, f: (0, 0)),
                  pl.BlockSpec((1, d, tf), lambda b, i, f: (0, 0, f)),
                  pl.BlockSpec((1, d, tf), lambda b, i, f: (0, 0, f)),
                  pl.BlockSpec((1, tf, d), lambda b, i, f: (0, f, 0))],
        out_specs=pl.BlockSpec((1, tm, d), lambda b, i, f: (b, i, 0)),
        scratch_shapes=[pltpu.VMEM((tm, d), BF16), pltpu.VMEM((tm, d), F32)],
        compiler_params=_cparams(("arbitrary", "arbitrary", "arbitrary")),
        name="ffn_dense",
    )(x, a_bf16, pw_bf16, pb.reshape(1, d), mod, gnorm.reshape(1, d), w1[None], w3[None], w2[None])


EXPERT_TILE_PARTS = 8


def _ffn_expert_kernel(te_ref, tp_ref, x_ref, w1_ref, w3_ref, w2_ref, o_ref, h_ref, acc_ref):
    i = pl.program_id(0)
    f = pl.program_id(1)
    tm, d = h_ref.shape
    parts = tp_ref[i]

    @pl.when(parts > 0)
    def _():
        @pl.when(f == 0)
        def _():
            s = d // LANES
            for j in range(s):
                h_ref[:, j * LANES:(j + 1) * LANES] = _load_token_slab(x_ref, j, tm, s).astype(BF16)
            acc_ref[...] = jnp.zeros_like(acc_ref)

        for q in range(1, EXPERT_TILE_PARTS + 1):
            @pl.when(parts == q)
            def _(q=q):
                _swiglu_step(h_ref, w1_ref, w3_ref, w2_ref, acc_ref, q * tm // EXPERT_TILE_PARTS)

        @pl.when(f == pl.num_programs(1) - 1)
        def _():
            _store_token_slabs(o_ref, acc_ref[...])

    @pl.when((parts == 0) & (f == 0))
    def _():
        o_ref[...] = jnp.zeros_like(o_ref)


def _ffn_experts(xs, tile_expert, tile_valid, w1, w3, w2, tm, tf_pref=512):
    d = w1.shape[1]
    s = d // LANES
    np_rows = xs.shape[0] // s
    ff = w1.shape[2]
    tf = _pick(ff, tf_pref)
    nf = ff // tf
    fsel = lambda i, f, tv: jnp.where(tv[i] > 0, f, nf - 1)
    grid_spec = pltpu.PrefetchScalarGridSpec(
        num_scalar_prefetch=2,
        grid=(np_rows // tm, nf),
        in_specs=[pl.BlockSpec((tm * s, LANES), lambda i, f, te, tv: (i, 0)),
                  pl.BlockSpec((1, d, tf), lambda i, f, te, tv: (te[i], 0, fsel(i, f, tv))),
                  pl.BlockSpec((1, d, tf), lambda i, f, te, tv: (te[i], 0, fsel(i, f, tv))),
                  pl.BlockSpec((1, tf, d), lambda i, f, te, tv: (te[i], fsel(i, f, tv), 0))],
        out_specs=pl.BlockSpec((tm * s, LANES), lambda i, f, te, tv: (i, 0)),
        scratch_shapes=[pltpu.VMEM((tm, d), BF16), pltpu.VMEM((tm, d), F32)],
    )
    return pl.pallas_call(
        _ffn_expert_kernel,
        out_shape=jax.ShapeDtypeStruct((np_rows * s, LANES), F32),
        grid_spec=grid_spec,
        compiler_params=_cparams(("arbitrary", "arbitrary")),
        name="ffn_experts",
    )(tile_expert, tile_valid, xs, w1, w3, w2)


def _mla_latent_kernel(x_ref, mod_ref, g_ref, wa_ref, qg_ref, kvg_ref, wqb_ref, wk_ref, wv_ref, tab_ref,
                       q_ref, k_ref, v_ref, *, q_lora, kv_lora):
    mod = mod_ref[0]
    h = _rms(x_ref[0], g_ref[...]) * (1.0 + mod[1:2]) + mod[0:1]
    a = jnp.dot(h.astype(BF16), wa_ref[...], preferred_element_type=F32)
    qn = _rms(a[:, :q_lora], qg_ref[...]).astype(BF16)
    cn = _rms(a[:, q_lora:q_lora + kv_lora], kvg_ref[...]).astype(BF16)
    kpe = a[:, q_lora + kv_lora:]
    tab = tab_ref[...]
    cq, sq, ck, sk = (tab[:, n * HEAD_SLOT:(n + 1) * HEAD_SLOT] for n in range(4))
    q = jnp.dot(qn, wqb_ref[...], preferred_element_type=F32)
    rep = q.shape[1] // HEAD_SLOT
    for hd in range(rep):
        lanes = slice(hd * HEAD_SLOT, (hd + 1) * HEAD_SLOT)
        qh = q[:, lanes]
        q_ref[0, :, lanes] = (qh * cq + pltpu.roll(qh, ROT_HALF, 1) * sq).astype(BF16)
    kr = kpe * ck + pltpu.roll(kpe, ROT_HALF, 1) * sk
    k = jnp.dot(cn, wk_ref[...], preferred_element_type=F32) + jnp.tile(kr, (1, rep))
    k_ref[0] = k.astype(BF16)
    v_ref[0] = jnp.dot(cn, wv_ref[...], preferred_element_type=F32).astype(BF16)


def _mla_context_kernel(x_ref, mod_ref, g_ref, wa_ref, kvg_ref, wk_ref, wv_ref, k_ref, v_ref, *, kv_lora):
    mod = mod_ref[0]
    h = _rms(x_ref[0], g_ref[...]) * (1.0 + mod[1:2]) + mod[0:1]
    a = jnp.dot(h.astype(BF16), wa_ref[...], preferred_element_type=F32)
    cn = _rms(a[:, :kv_lora], kvg_ref[...]).astype(BF16)
    kpe = a[:, kv_lora:]
    rep = wk_ref.shape[1] // LANES
    k = jnp.dot(cn, wk_ref[...], preferred_element_type=F32) + jnp.tile(kpe, (1, rep))
    k_ref[0] = k.astype(BF16)
    v_ref[0] = jnp.dot(cn, wv_ref[...], preferred_element_type=F32).astype(BF16)


ROT_HALF = HEAD_SLOT // 2


def _head_slot(nope, first, second):
    lo = ROT_HALF - ROPE_AXIS
    pad = jnp.zeros(nope.shape[:-1] + (HEAD_SLOT - QK_HEAD,), nope.dtype)
    return jnp.concatenate([first, nope[..., :lo], second, nope[..., lo:], pad], axis=-1)


def _rope_tables(l):
    rows = l // GRID_W
    row = jnp.broadcast_to(jnp.arange(rows, dtype=F32)[:, None], (rows, GRID_W)).reshape(l)
    col = jnp.broadcast_to(jnp.arange(GRID_W, dtype=F32)[None, :], (rows, GRID_W)).reshape(l)
    inv = ROPE_BASE ** (-jnp.arange(0, ROPE_AXIS, 2, dtype=F32) / ROPE_AXIS)
    ang = jnp.concatenate([row[:, None] * inv, col[:, None] * inv], axis=-1)
    cos = jnp.cos(ang)
    sin = jnp.sin(ang)
    c = _head_slot(jnp.ones((l, QK_NOPE), F32), cos, cos)
    s = _head_slot(jnp.zeros((l, QK_NOPE), F32), -sin, sin)
    scale = math.log2(math.e) / math.sqrt(QK_HEAD)
    return jnp.concatenate([c * scale, s * scale, c, s], axis=1)


def _mla_weights(wq_a, wq_b, wkv_a, wkv_b):
    d, q_lora = wq_a.shape
    kv_lora = wkv_a.shape[1] - QK_ROPE
    ev = np.arange(0, QK_ROPE, 2)
    od = np.arange(1, QK_ROPE, 2)
    kpe = wkv_a[:, kv_lora:]
    kpe_slot = _head_slot(jnp.zeros((d, QK_NOPE), F32), kpe[:, ev], kpe[:, od])
    wa = jnp.concatenate([wq_a, wkv_a[:, :kv_lora], kpe_slot], axis=1)
    qb = wq_b.reshape(q_lora, N_HEADS, QK_HEAD)
    qb = _head_slot(qb[..., :QK_NOPE], qb[..., QK_NOPE + ev], qb[..., QK_NOPE + od])
    kvb = wkv_b.reshape(kv_lora, N_HEADS, QK_NOPE + V_HEAD)
    no_rope = jnp.zeros((kv_lora, N_HEADS, ROPE_AXIS), F32)
    wk = _head_slot(kvb[..., :QK_NOPE], no_rope, no_rope)
    wv = kvb[..., QK_NOPE:]
    return (wa.astype(BF16), qb.reshape(q_lora, N_HEADS * HEAD_SLOT).astype(BF16),
            wk.reshape(kv_lora, N_HEADS * HEAD_SLOT).astype(BF16),
            wv.reshape(kv_lora, N_HEADS * V_HEAD).astype(BF16))


def _mod_spec(mod):
    per_b = mod.shape[0] > 1
    return pl.BlockSpec((1, MOD_ROWS, mod.shape[2]), (lambda b, i: (b, 0, 0)) if per_b else (lambda b, i: (0, 0, 0)))


def _mla_latent(x, mod, gnorm, wa, q_norm, kv_norm, wqb, wk, wv, tables, tm_pref=512):
    bx, n, d = x.shape
    tm = _pick(n, tm_pref)
    q_lora = q_norm.shape[0]
    kv_lora = kv_norm.shape[0]
    wq = wqb.shape[1]
    wvn = wv.shape[1]
    const = lambda b, i: (0, 0)
    row = lambda width: pl.BlockSpec((1, tm, width), lambda b, i: (b, i, 0))
    return pl.pallas_call(
        functools.partial(_mla_latent_kernel, q_lora=q_lora, kv_lora=kv_lora),
        out_shape=(jax.ShapeDtypeStruct((bx, n, wq), BF16),
                   jax.ShapeDtypeStruct((bx, n, wq), BF16),
                   jax.ShapeDtypeStruct((bx, n, wvn), BF16)),
        grid=(bx, n // tm),
        in_specs=[row(d), _mod_spec(mod),
                  pl.BlockSpec((1, d), const),
                  pl.BlockSpec(wa.shape, const),
                  pl.BlockSpec((1, q_lora), const),
                  pl.BlockSpec((1, kv_lora), const),
                  pl.BlockSpec(wqb.shape, const),
                  pl.BlockSpec(wk.shape, const),
                  pl.BlockSpec(wv.shape, const),
                  pl.BlockSpec((tm, tables.shape[1]), lambda b, i: (i, 0))],
        out_specs=(row(wq), row(wq), row(wvn)),
        compiler_params=_cparams(("arbitrary", "arbitrary")),
        name="mla_latent_proj",
    )(x, mod, gnorm.reshape(1, d), wa, q_norm.reshape(1, q_lora), kv_norm.reshape(1, kv_lora),
      wqb, wk, wv, tables)


def _mla_context(ctx, mod, gnorm, wa_kv, kv_norm, wk, wv, tm_pref=512):
    bx, n, d = ctx.shape
    tm = _pick(n, tm_pref)
    kv_lora = kv_norm.shape[0]
    wkn = wk.shape[1]
    wvn = wv.shape[1]
    const = lambda b, i: (0, 0)
    row = lambda width: pl.BlockSpec((1, tm, width), lambda b, i: (b, i, 0))
    return pl.pallas_call(
        functools.partial(_mla_context_kernel, kv_lora=kv_lora),
        out_shape=(jax.ShapeDtypeStruct((bx, n, wkn), BF16),
                   jax.ShapeDtypeStruct((bx, n, wvn), BF16)),
        grid=(bx, n // tm),
        in_specs=[row(d), _mod_spec(mod),
                  pl.BlockSpec((1, d), const),
                  pl.BlockSpec(wa_kv.shape, const),
                  pl.BlockSpec((1, kv_lora), const),
                  pl.BlockSpec(wk.shape, const),
                  pl.BlockSpec(wv.shape, const)],
        out_specs=(row(wkn), row(wvn)),
        compiler_params=_cparams(("arbitrary", "arbitrary")),
        name="mla_context_proj",
    )(ctx, mod, gnorm.reshape(1, d), wa_kv, kv_norm.reshape(1, kv_lora), wk, wv)


ATTN_HEADS_PER_STEP = 16
ATTN_KEY_CHUNK = 256
ATTN_SOFTMAX_LAG = 2
ATTN_PV_LAG = 6
SUM_ROWS = 16


def _attn_kernel(qt_ref, kc_ref, kx_ref, vct_ref, vxt_ref, o_ref, *, heads, ck):
    chunks = ([(kc_ref, vct_ref, c) for c in range(kc_ref.shape[1] // ck)]
              + [(kx_ref, vxt_ref, c) for c in range(kx_ref.shape[1] // ck)])
    items = [(h, ci) for ci in range(len(chunks)) for h in range(heads)]
    state = {h: None for h in range(heads)}
    scores = {}
    probs = {}
    ones = jnp.ones((SUM_ROWS, ck), BF16)

    def qk(t):
        h, ci = items[t]
        kref, _, c = chunks[ci]
        hsl = slice(h * HEAD_SLOT, (h + 1) * HEAD_SLOT)
        scores[t] = jnp.dot(kref[0, c * ck:(c + 1) * ck, hsl], qt_ref[0, hsl, :], preferred_element_type=F32)

    def softmax(t):
        h, _ = items[t]
        s = scores.pop(t)
        mc = jnp.max(s, axis=0, keepdims=True)
        if state[h] is None:
            probs[t] = (jnp.exp2(s - mc).astype(BF16), None)
            state[h] = (mc, None)
        else:
            m, acc = state[h]
            m_new = jnp.maximum(m, mc)
            probs[t] = (jnp.exp2(s - m_new).astype(BF16), jnp.exp2(m - m_new))
            state[h] = (m_new, acc)

    def pv(t):
        h, ci = items[t]
        _, vref, c = chunks[ci]
        p, alpha = probs.pop(t)
        lhs = jnp.concatenate([vref[0, h * V_HEAD:(h + 1) * V_HEAD, c * ck:(c + 1) * ck], ones], axis=0)
        o = jnp.dot(lhs, p, preferred_element_type=F32)
        m, acc = state[h]
        state[h] = (m, o if acc is None else alpha * acc + o)

    n_items = len(items)
    for t in range(n_items + ATTN_PV_LAG):
        if t < n_items:
            qk(t)
        if ATTN_SOFTMAX_LAG <= t < n_items + ATTN_SOFTMAX_LAG:
            softmax(t - ATTN_SOFTMAX_LAG)
        if t >= ATTN_PV_LAG:
            pv(t - ATTN_PV_LAG)

    for hp in range(heads // 2):
        outs = [state[h][1][:V_HEAD] / state[h][1][V_HEAD:V_HEAD + 1] for h in (2 * hp, 2 * hp + 1)]
        pair = jnp.concatenate(outs, axis=0)
        o_ref[0, :, hp * 2 * V_HEAD:(hp + 1) * 2 * V_HEAD] = pair.T.astype(BF16)


def _attention(qt, kc, kx, vct, vxt, tq_pref=256, heads=ATTN_HEADS_PER_STEP):
    bx, _, l = qt.shape
    n_ctx = kc.shape[1]
    tq = _pick(l, tq_pref)
    ck = _pick(n_ctx, ATTN_KEY_CHUNK)
    qk_w = heads * HEAD_SLOT
    v_w = heads * V_HEAD
    return pl.pallas_call(
        functools.partial(_attn_kernel, heads=heads, ck=ck),
        out_shape=jax.ShapeDtypeStruct((bx, l, N_HEADS * V_HEAD), BF16),
        grid=(bx, N_HEADS // heads, l // tq),
        in_specs=[pl.BlockSpec((1, qk_w, tq), lambda b, g, i: (b, g, i)),
                  pl.BlockSpec((1, n_ctx, qk_w), lambda b, g, i: (b, 0, g)),
                  pl.BlockSpec((1, l, qk_w), lambda b, g, i: (b, 0, g)),
                  pl.BlockSpec((1, v_w, n_ctx), lambda b, g, i: (b, g, 0)),
                  pl.BlockSpec((1, v_w, l), lambda b, g, i: (b, g, 0))],
        out_specs=pl.BlockSpec((1, tq, v_w), lambda b, g, i: (b, i, g)),
        compiler_params=_cparams(("arbitrary", "arbitrary", "arbitrary")),
        name="mla_attention",
    )(qt, kc, kx, vct, vxt)


ROUTE_SUBTILES = 2


def _route_kernel(x_ref, a_ref, pw_ref, mod_ref, g_ref, r_ref, tri_ref, xr_ref, h_ref, meta_ref, cnt_ref,
                  carry_ref, *, n_exp):
    i = pl.program_id(0)

    @pl.when(i == 0)
    def _():
        carry_ref[...] = jnp.zeros_like(carry_ref)

    mod = mod_ref[0]
    gate = mod[MOD_G1:MOD_G1 + 1]
    tr, d = x_ref.shape
    rs = tri_ref.shape[0]
    s = d // LANES
    proj = [jnp.dot(a_ref[t * rs:(t + 1) * rs, :], pw_ref[...], preferred_element_type=F32) for t in range(tr // rs)]
    r = r_ref[...]
    r_hi = r.astype(BF16)
    r_lo = (r - r_hi.astype(F32)).astype(BF16)
    lane = lax.broadcasted_iota(jnp.int32, (rs, LANES), 1)
    lane_f = lane.astype(F32)
    neg = jnp.float32(-jnp.inf)
    carry = carry_ref[...]
    for t in range(tr // rs):
        rows = slice(t * rs, (t + 1) * rs)
        xr = x_ref[rows, :] + gate * proj[t]
        xr_ref[rows, :] = xr
        h = _rms(xr, g_ref[...]) * (1.0 + mod[MOD_SC2:MOD_SC2 + 1]) + mod[MOD_SH2:MOD_SH2 + 1]
        _store_token_slabs(h_ref.at[pl.ds(t * rs * s, rs * s)], h)
        h_hi = h.astype(BF16)
        h_lo = (h - h_hi.astype(F32)).astype(BF16)
        logits = (jnp.dot(h_hi, r_hi, preferred_element_type=F32) + jnp.dot(h_hi, r_lo, preferred_element_type=F32)
                  + jnp.dot(h_lo, r_hi, preferred_element_type=F32))
        lg = jnp.where(lane < n_exp, logits, neg)
        v1 = jnp.max(lg, axis=-1, keepdims=True)
        i1 = jnp.min(jnp.where(lg == v1, lane_f, float(LANES)), axis=-1, keepdims=True)
        oh1 = lane_f == i1
        lg2 = jnp.where(oh1, neg, lg)
        v2 = jnp.max(lg2, axis=-1, keepdims=True)
        i2 = jnp.min(jnp.where(lg2 == v2, lane_f, float(LANES)), axis=-1, keepdims=True)
        oh2 = lane_f == i2
        e = jnp.exp(v2 - v1)
        g1 = 1.0 / (1.0 + e)
        g2 = e / (1.0 + e)
        oh = jnp.where(oh1 | oh2, 1.0, 0.0)
        pref = jnp.dot(tri_ref[...], oh.astype(BF16), preferred_element_type=F32)
        excl = pref - oh + carry
        r1 = jnp.sum(jnp.where(oh1, excl, 0.0), axis=-1, keepdims=True)
        r2 = jnp.sum(jnp.where(oh2, excl, 0.0), axis=-1, keepdims=True)
        carry = carry + jnp.sum(oh, axis=0, keepdims=True)
        meta = jnp.where(lane == 0, i1, 0.0)
        meta = jnp.where(lane == 1, i2, meta)
        meta = jnp.where(lane == 2, g1, meta)
        meta = jnp.where(lane == 3, g2, meta)
        meta = jnp.where(lane == 4, r1, meta)
        meta = jnp.where(lane == 5, r2, meta)
        meta_ref[rows, :] = meta
    carry_ref[...] = carry
    cnt_ref[...] = carry


def _mixer_proj_route(x_flat, a_flat, pw_bf16, mod, gnorm, router, tokens_per_batch, tr_pref=512):
    n, d = x_flat.shape
    k = a_flat.shape[1]
    n_exp = router.shape[1]
    tr = _pick(tokens_per_batch, tr_pref)
    per_b = tokens_per_batch // tr
    rpad = jnp.pad(router, ((0, 0), (0, LANES - n_exp)))
    rs = tr // ROUTE_SUBTILES
    tri = jnp.asarray(np.tril(np.ones((rs, rs), np.float32))).astype(BF16)
    return pl.pallas_call(
        functools.partial(_route_kernel, n_exp=n_exp),
        out_shape=(jax.ShapeDtypeStruct((n, d), F32),
                   jax.ShapeDtypeStruct((n * (d // LANES), LANES), F32),
                   jax.ShapeDtypeStruct((n, LANES), F32),
                   jax.ShapeDtypeStruct((1, LANES), F32)),
        grid=(n // tr,),
        in_specs=[pl.BlockSpec((tr, d), lambda i: (i, 0)),
                  pl.BlockSpec((tr, k), lambda i: (i, 0)),
                  pl.BlockSpec((k, d), lambda i: (0, 0)),
                  pl.BlockSpec((1, MOD_ROWS, d), lambda i: (i // per_b, 0, 0)),
                  pl.BlockSpec((1, d), lambda i: (0, 0)),
                  pl.BlockSpec((d, LANES), lambda i: (0, 0)),
                  pl.BlockSpec((rs, rs), lambda i: (0, 0))],
        out_specs=(pl.BlockSpec((tr, d), lambda i: (i, 0)),
                   pl.BlockSpec((tr * (d // LANES), LANES), lambda i: (i, 0)),
                   pl.BlockSpec((tr, LANES), lambda i: (i, 0)),
                   pl.BlockSpec((1, LANES), lambda i: (0, 0))),
        scratch_shapes=[pltpu.VMEM((1, LANES), F32)],
        compiler_params=_cparams(("arbitrary",)),
        name="moe_route",
    )(x_flat, a_flat, pw_bf16, mod, gnorm.reshape(1, d), rpad, tri)


ROW_DMA_UNROLL = 8


def _wait_rows(any_ref, rows, sem):
    blk = any_ref.at[pl.ds(0, rows)]
    pltpu.make_async_copy(blk, blk, sem).wait()


def _dispatch_kernel(pos_ref, h_ref, xs_in_ref, xs_ref, sem, *, td, s):
    del xs_in_ref

    def issue(t, c):
        src = h_ref.at[pl.ds(pl.multiple_of(t * s, s), s)]
        for k in range(TOP_K):
            dst = xs_ref.at[pl.ds(pl.multiple_of(pos_ref[0, 0, k * td + t], s), s)]
            pltpu.make_async_copy(src, dst, sem).start(priority=k)
        return c

    lax.fori_loop(0, td, issue, 0, unroll=ROW_DMA_UNROLL)
    _wait_rows(xs_ref, TOP_K * td * s, sem)


def _dispatch(h, pos_tiles, np_rows, td, s):
    n = h.shape[0] // s
    xs0 = jnp.zeros((np_rows * s, LANES), F32)
    return pl.pallas_call(
        functools.partial(_dispatch_kernel, td=td, s=s),
        out_shape=jax.ShapeDtypeStruct((np_rows * s, LANES), F32),
        grid=(n // td,),
        in_specs=[pl.BlockSpec((1, 1, TOP_K * td), lambda i: (i, 0, 0), memory_space=pltpu.SMEM),
                  pl.BlockSpec((td * s, LANES), lambda i: (i, 0)),
                  pl.BlockSpec(memory_space=pl.ANY)],
        out_specs=pl.BlockSpec(memory_space=pl.ANY),
        scratch_shapes=[pltpu.SemaphoreType.DMA(())],
        input_output_aliases={2: 0},
        compiler_params=_cparams(("arbitrary",)),
        name="moe_dispatch",
    )(pos_tiles, h, xs0)


def _combine_kernel(pos_ref, posn_ref, ys_ref, meta_ref, x_ref, mod_ref, g_ref, o_ref, buf_ref, sems, *, td, s):
    i = pl.program_id(0)
    n = pl.num_programs(0)
    slot = lax.rem(i, 2)

    def gather(p_ref, sl):
        def issue(t, c):
            for k in range(TOP_K):
                src = ys_ref.at[pl.ds(pl.multiple_of(p_ref[0, 0, k * td + t], s), s)]
                dst = buf_ref.at[sl, k, pl.ds(pl.multiple_of(t * s, s), s)]
                pltpu.make_async_copy(src, dst, sems.at[sl]).start(priority=k)
            return c

        lax.fori_loop(0, td, issue, 0, unroll=ROW_DMA_UNROLL)

    @pl.when(i == 0)
    def _():
        gather(pos_ref, slot)

    @pl.when(i + 1 < n)
    def _():
        gather(posn_ref, 1 - slot)

    _wait_rows(ys_ref, TOP_K * td * s, sems.at[slot])
    meta = meta_ref[...]
    g0 = meta[:, 2:3]
    g1 = meta[:, 3:4]
    gate = mod_ref[0][5:6]
    xs = []
    ssq = jnp.zeros((td, 1), F32)
    for j in range(s):
        lanes = slice(j * LANES, (j + 1) * LANES)
        y = (g0 * _load_token_slab(buf_ref.at[slot, 0], j, td, s)
             + g1 * _load_token_slab(buf_ref.at[slot, 1], j, td, s))
        xj = x_ref[:, lanes] + gate[:, lanes] * y
        ssq = ssq + jnp.sum(xj * xj, axis=-1, keepdims=True)
        xs.append(xj)
    inv = lax.rsqrt(ssq / (s * LANES) + RMS_EPS)
    for j in range(s):
        lanes = slice(j * LANES, (j + 1) * LANES)
        o_ref[:, lanes] = xs[j] * inv * g_ref[:, lanes]


def _combine(ys, pos_tiles, meta, x_flat, mod, norm_final, tokens_per_batch, td):
    n, d = x_flat.shape
    s = d // LANES
    per_b = tokens_per_batch // td
    nt = n // td
    return pl.pallas_call(
        functools.partial(_combine_kernel, td=td, s=s),
        out_shape=jax.ShapeDtypeStruct((n, d), F32),
        grid=(nt,),
        in_specs=[pl.BlockSpec((1, 1, TOP_K * td), lambda i: (i, 0, 0), memory_space=pltpu.SMEM),
                  pl.BlockSpec((1, 1, TOP_K * td), lambda i: (jnp.minimum(i + 1, nt - 1), 0, 0),
                               memory_space=pltpu.SMEM),
                  pl.BlockSpec(memory_space=pl.ANY),
                  pl.BlockSpec((td, LANES), lambda i: (i, 0)),
                  pl.BlockSpec((td, d), lambda i: (i, 0)),
                  pl.BlockSpec((1, MOD_ROWS, d), lambda i: (i // per_b, 0, 0)),
                  pl.BlockSpec((1, d), lambda i: (0, 0))],
        out_specs=pl.BlockSpec((td, d), lambda i: (i, 0)),
        scratch_shapes=[pltpu.VMEM((2, TOP_K, td * s, LANES), F32), pltpu.SemaphoreType.DMA((2,))],
        compiler_params=_cparams(("arbitrary",)),
        name="moe_combine",
    )(pos_tiles, pos_tiles, ys, meta, x_flat, mod, norm_final.reshape(1, d))


EXPERT_TILE_ROWS = 1024


def _attn_proj_moe(x, o, wo_bf16, mod, gnorm, router, w1, w3, w2, norm_final, td=256):
    bx, l, d = x.shape
    n = bx * l
    n_exp = router.shape[1]
    s = d // LANES
    tm = min(EXPERT_TILE_ROWS, n * TOP_K)
    x_flat, h, meta, counts = _mixer_proj_route(x.reshape(n, d), o.reshape(n, o.shape[2]), wo_bf16, mod, gnorm,
                                                router, l)
    idx = meta[:, 0:TOP_K].astype(jnp.int32)
    rank = meta[:, 4:4 + TOP_K].astype(jnp.int32)
    cnt = counts[0, :n_exp].astype(jnp.int32)
    tiles_e = (cnt + tm - 1) // tm
    tile_end = jnp.cumsum(tiles_e)
    start_rows = (tile_end - tiles_e) * tm
    sel = idx[..., None] == jnp.arange(n_exp, dtype=jnp.int32)
    pos = jnp.sum(jnp.where(sel, start_rows, 0), axis=-1) + rank
    n_tiles = -(-(n * TOP_K) // tm) + n_exp
    np_rows = n_tiles * tm
    tile_ids = jnp.arange(n_tiles, dtype=jnp.int32)
    used = tile_end[-1]
    tile_expert = jnp.sum((jnp.minimum(tile_ids, used - 1)[:, None] >= tile_end[None, :]).astype(jnp.int32), axis=1)
    tile_expert = jnp.minimum(tile_expert, n_exp - 1)
    onehot_e = tile_expert[:, None] == jnp.arange(n_exp, dtype=jnp.int32)
    tile_cnt = jnp.sum(jnp.where(onehot_e, cnt, 0), axis=1)
    tile_first = jnp.sum(jnp.where(onehot_e, tile_end - tiles_e, 0), axis=1)
    tile_rows = jnp.clip(tile_cnt - (tile_ids - tile_first) * tm, 0, tm)
    part = tm // EXPERT_TILE_PARTS
    tile_valid = jnp.where(tile_ids < used, (tile_rows + part - 1) // part, 0).astype(jnp.int32)
    pos_tiles = (pos * s).reshape(n // td, td, TOP_K).transpose(0, 2, 1).reshape(n // td, 1, TOP_K * td)
    xs = _dispatch(h, pos_tiles, np_rows, td, s)
    ys = _ffn_experts(xs, tile_expert, tile_valid, w1, w3, w2, tm)
    out = _combine(ys, pos_tiles, meta, x_flat, mod, norm_final, l, td)
    return out.reshape(bx, l, d)


def _mod_rows(m):
    r, n = m.shape
    return jnp.pad(m.reshape(r, N_MOD, n // N_MOD), ((0, 0), (0, MOD_ROWS - N_MOD), (0, 0)))


def _hyena_mixer(x, mod, gnorm, in_w, in_b, sc_w, sc_b, spectra, f_bias, ct_pref):
    zu = _hyena_in(x, mod, gnorm, in_w, in_b, sc_w, sc_b, _pick(x.shape[2], ct_pref))
    return _hyena_conv(zu, spectra, f_bias)


@jax.jit
def kernel(x, c, ctx, c_ctx, ada_w, ada_b, norm_mix, norm_ffn, hy_in_w, hy_in_b, hy_sc_w, hy_sc_b, hy_f_w0, hy_f_b0, hy_f_wi, hy_f_bi, hy_f_freq, hy_f_wout, hy_f_bias, hy_out_w, hy_out_b, mla_wq_a, mla_q_norm, mla_wq_b, mla_wkv_a, mla_kv_norm, mla_wkv_b, mla_wo, ffn_w1, ffn_w3, ffn_w2, moe_router, moe_w1, moe_w3, moe_w2, norm_final):
    bsz, l, d = x.shape
    n_ctx = ctx.shape[1]
    depth = ada_w.shape[0]
    assert depth == 2, "layer 0 = Hyena + dense SwiGLU, layer 1 = MLA + expert SwiGLU"

    rows = -(-(bsz + 1) // 8) * 8
    cvec = jnp.zeros((rows, d), F32).at[:bsz].set(c).at[bsz].set(c_ctx)
    mods = _ada_mod(cvec, ada_w, ada_b)
    modx = [_mod_rows(mods[i, :bsz]) for i in range(depth)]
    modc = [_mod_rows(mods[i, bsz:bsz + 1]) for i in range(depth)]

    in_w = hy_in_w[0].astype(BF16)
    out_w = hy_out_w[0].astype(BF16)
    fargs = (hy_f_w0[0], hy_f_b0[0], hy_f_wi[0], hy_f_bi[0], hy_f_freq[0], hy_f_wout[0])
    kx = _hyena_filter_spectra(l, d, *fargs)
    kc = _hyena_filter_spectra(n_ctx, d, *fargs)
    gx = _hyena_mixer(x, modx[0], norm_mix[0], in_w, hy_in_b[0], hy_sc_w[0], hy_sc_b[0], kx, hy_f_bias[0], 256)
    gc = _hyena_mixer(ctx, modc[0], norm_mix[0], in_w, hy_in_b[0], hy_sc_w[0], hy_sc_b[0], kc, hy_f_bias[0], 1024)
    ffn = (norm_ffn[0], ffn_w1[0], ffn_w3[0], ffn_w2[0])
    x = _mixer_proj_ffn_dense(x, gx, out_w, hy_out_b[0], modx[0], *ffn)
    ctx = _mixer_proj_ffn_dense(ctx.reshape(1, bsz * n_ctx, d), gc.reshape(1, bsz * n_ctx, d), out_w, hy_out_b[0],
                                modc[0], *ffn).reshape(bsz, n_ctx, d)

    wa, wqb, wk, wv = _mla_weights(mla_wq_a[0], mla_wq_b[0], mla_wkv_a[0], mla_wkv_b[0])
    q_lora = mla_q_norm.shape[1]
    q, kx, vx = _mla_latent(x, modx[1], norm_mix[1], wa, mla_q_norm[0], mla_kv_norm[0], wqb, wk, wv,
                            _rope_tables(l))
    kc, vc = _mla_context(ctx, modc[1], norm_mix[1], wa[:, q_lora:], mla_kv_norm[0], wk, wv)
    o = _attention(q.transpose(0, 2, 1), kc, kx, vc.transpose(0, 2, 1), vx.transpose(0, 2, 1))
    return _attn_proj_moe(x, o, mla_wo[0].astype(BF16), modx[1], norm_ffn[1], moe_router[0],
                          moe_w1[0], moe_w3[0], moe_w2[0], norm_final)
```

```python
import functools
import math

import jax
import jax.numpy as jnp
import numpy as np
from jax import lax
from jax.experimental import pallas as pl
from jax.experimental.pallas import tpu as pltpu

F32 = jnp.float32
BF16 = jnp.bfloat16
HIGHEST = lax.Precision.HIGHEST

RMS_EPS = 1e-6
N_MOD = 6
MOD_ROWS = 8
MOD_SH1, MOD_SC1, MOD_G1, MOD_SH2, MOD_SC2, MOD_G2 = range(N_MOD)
GRID_W = 64
SHORT_CONV = 3
FILTER_BANDS = 8
FILTER_EMB = 1 + 2 * FILTER_BANDS
FILTER_EMB_PAD = 32
DECAY_TARGET = 1e-2
FAST_DECAY_PCT = 0.3
SLOW_DECAY_PCT = 1.5
N_HEADS = 16
QK_NOPE = 64
QK_ROPE = 32
QK_HEAD = QK_NOPE + QK_ROPE
V_HEAD = 64
ROPE_AXIS = QK_ROPE // 2
ROPE_BASE = 10000.0
TOP_K = 2
LANES = 128
HEAD_SLOT = 128
VMEM_LIMIT = 56 * 1024 * 1024


def _cparams(sem, vmem=VMEM_LIMIT):
    return pltpu.CompilerParams(dimension_semantics=sem, vmem_limit_bytes=vmem)


def _rms(x, g):
    return x * lax.rsqrt(jnp.mean(x * x, axis=-1, keepdims=True) + RMS_EPS) * g


def _silu(x):
    return x * (1.0 / (1.0 + jnp.exp(-x)))


def _store_token_slabs(ref, value):
    rows, width = value.shape
    s = width // LANES
    for j in range(s):
        ref[pl.ds(j, rows, stride=s), :] = value[:, j * LANES:(j + 1) * LANES]


def _load_token_slab(ref, j, rows, s):
    return ref[pl.ds(j, rows, stride=s), :]


def _pick(total, pref):
    t = min(total, pref)
    while total % t:
        t //= 2
    return t


def _ada_kernel(c_ref, w_ref, b_ref, o_ref):
    c = c_ref[...]
    o_ref[0] = jnp.dot(_silu(c), w_ref[0], precision=HIGHEST, preferred_element_type=F32) + b_ref[0]


def _ada_mod(cvec, ada_w, ada_b):
    depth, d, n = ada_w.shape
    r = cvec.shape[0]
    tn = _pick(n, 1536)
    return pl.pallas_call(
        _ada_kernel,
        out_shape=jax.ShapeDtypeStruct((depth, r, n), F32),
        grid=(depth, n // tn),
        in_specs=[pl.BlockSpec((r, d), lambda i, j: (0, 0)),
                  pl.BlockSpec((1, d, tn), lambda i, j: (i, 0, j)),
                  pl.BlockSpec((1, 1, tn), lambda i, j: (i, 0, j))],
        out_specs=pl.BlockSpec((1, r, tn), lambda i, j: (i, 0, j)),
        compiler_params=_cparams(("arbitrary", "arbitrary")),
        name="ada_mod",
    )(cvec, ada_w, ada_b.reshape(depth, 1, n))


HALO = 8
IN_PROJ_STAGES = 3


def _hyena_in_kernel(x_ref, xp_ref, xn_ref, mod_ref, g_ref, w_ref, b_ref, cw_ref, cb_ref, o_ref, z_ref, *, ct):
    i = pl.program_id(1)
    tm = x_ref.shape[1]
    d = x_ref.shape[2]
    mod = mod_ref[0]
    xe = jnp.concatenate([xp_ref[0], x_ref[0], xn_ref[0]], axis=0)
    hb = (_rms(xe, g_ref[...]) * (1.0 + mod[MOD_SC1:MOD_SC1 + 1]) + mod[MOD_SH1:MOD_SH1 + 1]).astype(BF16)
    is_first = i == 0
    is_last = i == pl.num_programs(1) - 1

    slots = z_ref.shape[0]
    calls = [0]

    def conv_cols(c0):
        cols = slice(c0, c0 + ct)
        zs = z_ref.at[calls[0] % slots]
        calls[0] += 1
        zs[...] = jnp.dot(hb, w_ref[:, cols], preferred_element_type=F32) + b_ref[:, cols]
        zs[0:HALO, :] = jnp.where(is_first, 0.0, zs[0:HALO, :])
        zs[tm + HALO:, :] = jnp.where(is_last, 0.0, zs[tm + HALO:, :])
        return (zs[HALO - 1:HALO - 1 + tm, :] * cw_ref[0:1, cols] + zs[HALO:HALO + tm, :] * cw_ref[1:2, cols]
                + zs[HALO + 1:HALO + 1 + tm, :] * cw_ref[2:3, cols] + cb_ref[:, cols])

    nct = d // ct
    for j in range(nct):
        o_ref[j, 0] = conv_cols(j * ct)
        o_ref[nct + j, 0] = conv_cols(2 * d + j * ct) * conv_cols(d + j * ct)


def _hyena_in(x, mod, gnorm, w_bf16, bias, sc_w, sc_b, ct, tm_pref=512):
    bx, l, d = x.shape
    n = w_bf16.shape[1]
    tm = _pick(l, tm_pref)
    tpb = tm // HALO
    last = l // HALO - 1
    per_b = mod.shape[0] > 1
    return pl.pallas_call(
        functools.partial(_hyena_in_kernel, ct=ct),
        out_shape=jax.ShapeDtypeStruct((2 * d // ct, bx, l, ct), F32),
        grid=(bx, l // tm),
        in_specs=[pl.BlockSpec((1, tm, d), lambda b, i: (b, i, 0)),
                  pl.BlockSpec((1, HALO, d), lambda b, i: (b, jnp.maximum(i * tpb - 1, 0), 0)),
                  pl.BlockSpec((1, HALO, d), lambda b, i: (b, jnp.minimum((i + 1) * tpb, last), 0)),
                  pl.BlockSpec((1, MOD_ROWS, d), (lambda b, i: (b, 0, 0)) if per_b else (lambda b, i: (0, 0, 0))),
                  pl.BlockSpec((1, d), lambda b, i: (0, 0)),
                  pl.BlockSpec((d, n), lambda b, i: (0, 0)),
                  pl.BlockSpec((1, n), lambda b, i: (0, 0)),
                  pl.BlockSpec((SHORT_CONV, n), lambda b, i: (0, 0)),
                  pl.BlockSpec((1, n), lambda b, i: (0, 0))],
        out_specs=pl.BlockSpec((2 * d // ct, 1, tm, ct), lambda b, i: (0, b, i, 0)),
        scratch_shapes=[pltpu.VMEM((IN_PROJ_STAGES, tm + 2 * HALO, ct), F32)],
        compiler_params=_cparams(("arbitrary", "arbitrary")),
        name="hyena_in_proj",
    )(x, x, x, mod, gnorm.reshape(1, d), w_bf16, bias.reshape(1, n), sc_w, sc_b.reshape(1, n))


def _filter_kernel(z_ref, w0_ref, b0_ref, wi_ref, bi_ref, fr_ref, wt_ref, wb_ref, dl_ref, fwd_ref, o_ref, h_ref,
                   *, l, p):
    @pl.when(pl.program_id(0) == 0)
    def _():
        fr = fr_ref[...]
        h = jnp.sin(fr * (jnp.dot(z_ref[...], w0_ref[...], precision=HIGHEST, preferred_element_type=F32)
                          + b0_ref[...]))
        for n in range(wi_ref.shape[0]):
            h = jnp.sin(fr * (jnp.dot(h, wi_ref[n], precision=HIGHEST, preferred_element_type=F32) + bi_ref[n]))
        h_ref[...] = h

    top = jnp.dot(h_ref[:l], wt_ref[...], precision=HIGHEST, preferred_element_type=F32)
    bot = jnp.dot(h_ref[l:], wb_ref[...], precision=HIGHEST, preferred_element_type=F32)
    t = z_ref[:, 0:1]
    rows = lax.broadcasted_iota(jnp.int32, (2 * l, 1), 0)
    decay = jnp.where(rows == l, 0.0, jnp.exp(-t * dl_ref[...]))
    k = jnp.concatenate([top, bot], axis=0) * decay
    k = k / jnp.sum(jnp.abs(k), axis=0, keepdims=True)
    nb = l // p
    for di in range(2 * nb - 1):
        start = (p * (di - nb)) % (2 * l)
        if start + 2 * p <= 2 * l:
            seg = k[start:start + 2 * p]
        else:
            seg = jnp.concatenate([k[start:], k[:start + 2 * p - 2 * l]], axis=0)
        o_ref[di] = jnp.dot(fwd_ref[...], seg.astype(BF16), preferred_element_type=F32)


def _conv_block(l, p_pref=512):
    return min(p_pref, l)


@functools.lru_cache(maxsize=None)
def _filter_positions(l, d):
    f32 = np.float32
    pos = np.arange(l, dtype=np.float64)
    t = (pos / max(l - 1, 1))[:, None]
    w = 2.0 * math.pi * pos / l
    f = np.linspace(1e-4, FILTER_BANDS - 1, FILTER_BANDS)
    ang = w[:, None] * f[None, :]
    z = np.concatenate([t, np.cos(ang), -np.sin(ang)], axis=-1).astype(f32)
    deltas = np.abs(np.linspace(math.log(DECAY_TARGET) / SLOW_DECAY_PCT,
                                math.log(DECAY_TARGET) / FAST_DECAY_PCT, d)).astype(f32)
    idx = np.concatenate([np.arange(l), [0], np.arange(l - 1, 0, -1)])
    zc = np.pad(z[idx], ((0, 0), (0, FILTER_EMB_PAD - FILTER_EMB)))
    return zc, deltas


def _hyena_filter_spectra(l, d, w0, b0, wi, bi, freq, wout):
    p = _conv_block(l)
    nd = 2 * (l // p) - 1
    zc, deltas = _filter_positions(l, d)
    w0p = jnp.pad(w0, ((0, FILTER_EMB_PAD - FILTER_EMB), (0, 0)))
    hid = w0.shape[1]
    n_in = wi.shape[0]
    ct = _pick(d, 256)
    nct = d // ct
    fwd = jnp.asarray(_dft_mats(p)[0]).astype(BF16)
    return pl.pallas_call(
        functools.partial(_filter_kernel, l=l, p=p),
        out_shape=jax.ShapeDtypeStruct((nd, 2 * p, d), F32),
        grid=(nct,),
        in_specs=[pl.BlockSpec((2 * l, FILTER_EMB_PAD), lambda j: (0, 0)),
                  pl.BlockSpec((FILTER_EMB_PAD, hid), lambda j: (0, 0)),
                  pl.BlockSpec((1, hid), lambda j: (0, 0)),
                  pl.BlockSpec((n_in, hid, hid), lambda j: (0, 0, 0)),
                  pl.BlockSpec((n_in, 1, hid), lambda j: (0, 0, 0)),
                  pl.BlockSpec((1, hid), lambda j: (0, 0)),
                  pl.BlockSpec((hid, ct), lambda j: (0, j)),
                  pl.BlockSpec((hid, ct), lambda j: (0, nct + j)),
                  pl.BlockSpec((1, ct), lambda j: (0, j)),
                  pl.BlockSpec((2 * p, 2 * p), lambda j: (0, 0))],
        out_specs=pl.BlockSpec((nd, 2 * p, ct), lambda j: (0, 0, j)),
        scratch_shapes=[pltpu.VMEM((2 * l, hid), F32)],
        compiler_params=_cparams(("arbitrary",)),
        name="hyena_filter",
    )(zc, w0p, b0.reshape(1, hid), wi, bi.reshape(n_in, 1, hid), freq.reshape(1, hid), wout, wout,
      deltas.reshape(1, d), fwd)


@functools.lru_cache(maxsize=None)
def _dft_mats(p):
    n = 2 * p
    f = np.arange(p)[:, None]
    t = np.arange(n)[None, :]
    ang = 2.0 * np.pi * (((2 * f + 1) * t) % (4 * p)) / (4 * p)
    fwd = np.concatenate([np.cos(ang), -np.sin(ang)], axis=0)
    q = np.arange(p)[:, None]
    ff = np.arange(p)[None, :]
    ang2 = 2.0 * np.pi * (((2 * ff + 1) * (q + p)) % (4 * p)) / (4 * p)
    inv = np.concatenate([np.cos(ang2), -np.sin(ang2)], axis=1) / p
    return fwd.astype(np.float32), inv.astype(np.float32)


CONV_INVERSE_PIECES = 4


CONV_STAGES = 3


def _hyena_conv_kernel(uf_ref, x0_ref, u_ref, ks_ref, fb_ref, fwd_ref, inv_ref, o_ref, vs_ref, ys_ref,
                       *, l, p, rc):
    nb = l // p
    s = pl.program_id(0)

    @pl.when(s == 0)
    def _():
        vs_ref[...] = jnp.zeros_like(vs_ref)
        ys_ref[...] = jnp.zeros_like(ys_ref)

    fb = fb_ref[...]
    pieces = CONV_INVERSE_PIECES
    pr = p // pieces
    chunks = p // rc
    per_piece = chunks // pieces

    def step(cur):
        prv = 1 - cur

        def forward(j):
            vs_ref[cur, j] = jnp.dot(fwd_ref[...], uf_ref[0, 0, j * p:(j + 1) * p, :].astype(BF16),
                                     preferred_element_type=F32)

        def spectrum_rows(i, c):
            re = slice(c * rc, (c + 1) * rc)
            im = slice(p + c * rc, p + (c + 1) * rc)
            yr = None
            yi = None
            for j in range(nb):
                di = i - j + nb - 1
                kr = ks_ref[di, re, :]
                ki = ks_ref[di, im, :]
                vr = vs_ref[prv, j, re, :]
                vi = vs_ref[prv, j, im, :]
                tr = kr * vr - ki * vi
                ti = kr * vi + ki * vr
                yr = tr if yr is None else yr + tr
                yi = ti if yi is None else yi + ti
            ys_ref[prv, i, re, :] = yr
            ys_ref[prv, i, im, :] = yi

        spectra_bf16 = {}

        def inverse_piece(i, q):
            if i not in spectra_bf16:
                spectra_bf16[i] = ys_ref[cur, i].astype(BF16)
            y = jnp.dot(inv_ref[q * pr:(q + 1) * pr, :], spectra_bf16[i], preferred_element_type=F32)
            sl = slice(i * p + q * pr, i * p + (q + 1) * pr)
            o_ref[0, sl, :] = (x0_ref[0, 0, sl, :] * (y + u_ref[0, 0, sl, :] * fb)).astype(BF16)

        for i in range(nb):
            forward(i)
            for q in range(pieces):
                for c in range(q * per_piece, (q + 1) * per_piece):
                    spectrum_rows(i, c)
                inverse_piece(i, q)

    parity = lax.rem(s, 2)
    for cur in range(2):
        pl.when(parity == cur)(functools.partial(step, cur))


def _hyena_conv(zu, spectra, f_bias):
    nz, bx, l, ct = zu.shape
    nct = nz // 2
    d = nct * ct
    nd, p2, _ = spectra.shape
    p = p2 // 2
    nb = l // p
    fwd, inv = _dft_mats(p)
    fwd = jnp.asarray(fwd[:, :p]).astype(BF16)
    inv = jnp.asarray(inv).astype(BF16)
    items = nct * bx
    tile = lambda t: t // bx
    seq = lambda t: lax.rem(t, bx)
    stage = lambda s, k: jnp.clip(s - k, 0, items - 1)
    once = pl.Buffered(1)
    return pl.pallas_call(
        functools.partial(_hyena_conv_kernel, l=l, p=p, rc=8),
        out_shape=jax.ShapeDtypeStruct((bx, l, d), BF16),
        grid=(items + CONV_STAGES - 1,),
        in_specs=[pl.BlockSpec((1, 1, l, ct), lambda s: (nct + tile(stage(s, 0)), seq(stage(s, 0)), 0, 0)),
                  pl.BlockSpec((1, 1, l, ct), lambda s: (tile(stage(s, 2)), seq(stage(s, 2)), 0, 0)),
                  pl.BlockSpec((1, 1, l, ct), lambda s: (nct + tile(stage(s, 2)), seq(stage(s, 2)), 0, 0)),
                  pl.BlockSpec((nd, 2 * p, ct), lambda s: (0, 0, tile(stage(s, 1))), pipeline_mode=once),
                  pl.BlockSpec((1, ct), lambda s: (0, tile(stage(s, 2)))),
                  pl.BlockSpec((2 * p, p), lambda s: (0, 0), pipeline_mode=once),
                  pl.BlockSpec((p, 2 * p), lambda s: (0, 0), pipeline_mode=once)],
        out_specs=pl.BlockSpec((1, l, ct), lambda s: (seq(stage(s, 2)), 0, tile(stage(s, 2)))),
        scratch_shapes=[pltpu.VMEM((2, nb, 2 * p, ct), F32),
                        pltpu.VMEM((2, nb, 2 * p, ct), F32)],
        compiler_params=_cparams(("arbitrary",)),
        name="hyena_conv",
    )(zu, zu, zu, spectra, f_bias.reshape(1, d), fwd, inv)


def _swiglu_step(h_ref, w1_ref, w3_ref, w2_ref, acc_ref, rows):
    h = h_ref[:rows]
    a = jnp.dot(h, w1_ref[0].astype(BF16), preferred_element_type=F32)
    b = jnp.dot(h, w3_ref[0].astype(BF16), preferred_element_type=F32)
    g = (_silu(a) * b).astype(BF16)
    acc_ref[:rows] += jnp.dot(g, w2_ref[0].astype(BF16), preferred_element_type=F32)


def _ffn_dense_kernel(x_ref, a_ref, pw_ref, pb_ref, mod_ref, g_ref, w1_ref, w3_ref, w2_ref, o_ref, h_ref, acc_ref):
    f = pl.program_id(2)

    @pl.when(f == 0)
    def _():
        mod = mod_ref[0]
        tm = h_ref.shape[0]
        half = tm // 2
        proj = [jnp.dot(a_ref[0, t * half:(t + 1) * half, :], pw_ref[...], preferred_element_type=F32)
                for t in range(2)]
        for t in range(2):
            rows = slice(t * half, (t + 1) * half)
            xr = x_ref[0, rows, :] + mod[MOD_G1:MOD_G1 + 1] * (proj[t] + pb_ref[...])
            o_ref[0, rows, :] = xr
            h = _rms(xr, g_ref[...]) * (1.0 + mod[MOD_SC2:MOD_SC2 + 1]) + mod[MOD_SH2:MOD_SH2 + 1]
            h_ref[rows, :] = h.astype(BF16)
        acc_ref[...] = jnp.zeros_like(acc_ref)

    _swiglu_step(h_ref, w1_ref, w3_ref, w2_ref, acc_ref, h_ref.shape[0])

    @pl.when(f == pl.num_programs(2) - 1)
    def _():
        o_ref[0] = o_ref[0] + mod_ref[0][MOD_G2:MOD_G2 + 1] * acc_ref[...]


def _mixer_proj_ffn_dense(x, a_bf16, pw_bf16, pb, mod, gnorm, w1, w3, w2, tm_pref=1024, tf_pref=512):
    bx, l, d = x.shape
    k = a_bf16.shape[2]
    ff = w1.shape[1]
    tm = _pick(l, tm_pref)
    tf = _pick(ff, tf_pref)
    per_b = mod.shape[0] > 1
    once = pl.Buffered(1)
    return pl.pallas_call(
        _ffn_dense_kernel,
        out_shape=jax.ShapeDtypeStruct((bx, l, d), F32),
        grid=(bx, l // tm, ff // tf),
        in_specs=[pl.BlockSpec((1, tm, d), lambda b, i, f: (b, i, 0)),
                  pl.BlockSpec((1, tm, k), lambda b, i, f: (b, i, 0)),
                  pl.BlockSpec((k, d), lambda b, i, f: (0, 0), pipeline_mode=once),
                  pl.BlockSpec((1, d), lambda b, i, f: (0, 0)),
                  pl.BlockSpec((1, MOD_ROWS, d), (lambda b, i, f: (b, 0, 0)) if per_b else (lambda b, i, f: (0, 0, 0))),
                  pl.BlockSpec((1, d), lambda b, i, f: (0, 0)),
                  pl.BlockSpec((1, d, tf), lambda b, i, f: (0, 0, f)),
                  pl.BlockSpec((1, d, tf), lambda b, i, f: (0, 0, f)),
                  pl.BlockSpec((1, tf, d), lambda b, i, f: (0, f, 0))],
        out_specs=pl.BlockSpec((1, tm, d), lambda b, i, f: (b, i, 0)),
        scratch_shapes=[pltpu.VMEM((tm, d), BF16), pltpu.VMEM((tm, d), F32)],
        compiler_params=_cparams(("arbitrary", "arbitrary", "arbitrary")),
        name="ffn_dense",
    )(x, a_bf16, pw_bf16, pb.reshape(1, d), mod, gnorm.reshape(1, d), w1[None], w3[None], w2[None])


EXPERT_TILE_PARTS = 8


def _ffn_expert_kernel(te_ref, tp_ref, x_ref, w1_ref, w3_ref, w2_ref, o_ref, h_ref, acc_ref):
    i = pl.program_id(0)
    f = pl.program_id(1)
    tm, d = h_ref.shape
    parts = tp_ref[i]

    @pl.when(parts > 0)
    def _():
        @pl.when(f == 0)
        def _():
            s = d // LANES
            for j in range(s):
                h_ref[:, j * LANES:(j + 1) * LANES] = _load_token_slab(x_ref, j, tm, s).astype(BF16)
            acc_ref[...] = jnp.zeros_like(acc_ref)

        for q in range(1, EXPERT_TILE_PARTS + 1):
            @pl.when(parts == q)
            def _(q=q):
                _swiglu_step(h_ref, w1_ref, w3_ref, w2_ref, acc_ref, q * tm // EXPERT_TILE_PARTS)

        @pl.when(f == pl.num_programs(1) - 1)
        def _():
            _store_token_slabs(o_ref, acc_ref[...])

    @pl.when((parts == 0) & (f == 0))
    def _():
        o_ref[...] = jnp.zeros_like(o_ref)


def _ffn_experts(xs, tile_expert, tile_valid, w1, w3, w2, tm, tf_pref=512):
    d = w1.shape[1]
    s = d // LANES
    np_rows = xs.shape[0] // s
    ff = w1.shape[2]
    tf = _pick(ff, tf_pref)
    nf = ff // tf
    fsel = lambda i, f, tv: jnp.where(tv[i] > 0, f, nf - 1)
    grid_spec = pltpu.PrefetchScalarGridSpec(
        num_scalar_prefetch=2,
        grid=(np_rows // tm, nf),
        in_specs=[pl.BlockSpec((tm * s, LANES), lambda i, f, te, tv: (i, 0)),
                  pl.BlockSpec((1, d, tf), lambda i, f, te, tv: (te[i], 0, fsel(i, f, tv))),
                  pl.BlockSpec((1, d, tf), lambda i, f, te, tv: (te[i], 0, fsel(i, f, tv))),
                  pl.BlockSpec((1, tf, d), lambda i, f, te, tv: (te[i], fsel(i, f, tv), 0))],
        out_specs=pl.BlockSpec((tm * s, LANES), lambda i, f, te, tv: (i, 0)),
        scratch_shapes=[pltpu.VMEM((tm, d), BF16), pltpu.VMEM((tm, d), F32)],
    )
    return pl.pallas_call(
        _ffn_expert_kernel,
        out_shape=jax.ShapeDtypeStruct((np_rows * s, LANES), F32),
        grid_spec=grid_spec,
        compiler_params=_cparams(("arbitrary", "arbitrary")),
        name="ffn_experts",
    )(tile_expert, tile_valid, xs, w1, w3, w2)


def _mla_latent_kernel(x_ref, mod_ref, g_ref, wa_ref, qg_ref, kvg_ref, wqb_ref, wk_ref, wv_ref, tab_ref,
                       q_ref, k_ref, v_ref, *, q_lora, kv_lora):
    mod = mod_ref[0]
    h = _rms(x_ref[0], g_ref[...]) * (1.0 + mod[1:2]) + mod[0:1]
    a = jnp.dot(h.astype(BF16), wa_ref[...], preferred_element_type=F32)
    qn = _rms(a[:, :q_lora], qg_ref[...]).astype(BF16)
    cn = _rms(a[:, q_lora:q_lora + kv_lora], kvg_ref[...]).astype(BF16)
    kpe = a[:, q_lora + kv_lora:]
    tab = tab_ref[...]
    cq, sq, ck, sk = (tab[:, n * HEAD_SLOT:(n + 1) * HEAD_SLOT] for n in range(4))
    q = jnp.dot(qn, wqb_ref[...], preferred_element_type=F32)
    rep = q.shape[1] // HEAD_SLOT
    for hd in range(rep):
        lanes = slice(hd * HEAD_SLOT, (hd + 1) * HEAD_SLOT)
        qh = q[:, lanes]
        q_ref[0, :, lanes] = (qh * cq + pltpu.roll(qh, ROT_HALF, 1) * sq).astype(BF16)
    kr = kpe * ck + pltpu.roll(kpe, ROT_HALF, 1) * sk
    k = jnp.dot(cn, wk_ref[...], preferred_element_type=F32) + jnp.tile(kr, (1, rep))
    k_ref[0] = k.astype(BF16)
    v_ref[0] = jnp.dot(cn, wv_ref[...], preferred_element_type=F32).astype(BF16)


def _mla_context_kernel(x_ref, mod_ref, g_ref, wa_ref, kvg_ref, wk_ref, wv_ref, k_ref, v_ref, *, kv_lora):
    mod = mod_ref[0]
    h = _rms(x_ref[0], g_ref[...]) * (1.0 + mod[1:2]) + mod[0:1]
    a = jnp.dot(h.astype(BF16), wa_ref[...], preferred_element_type=F32)
    cn = _rms(a[:, :kv_lora], kvg_ref[...]).astype(BF16)
    kpe = a[:, kv_lora:]
    rep = wk_ref.shape[1] // LANES
    k = jnp.dot(cn, wk_ref[...], preferred_element_type=F32) + jnp.tile(kpe, (1, rep))
    k_ref[0] = k.astype(BF16)
    v_ref[0] = jnp.dot(cn, wv_ref[...], preferred_element_type=F32).astype(BF16)


ROT_HALF = HEAD_SLOT // 2


def _head_slot(nope, first, second, xp=jnp):
    lo = ROT_HALF - ROPE_AXIS
    pad = xp.zeros(nope.shape[:-1] + (HEAD_SLOT - QK_HEAD,), nope.dtype)
    return xp.concatenate([first, nope[..., :lo], second, nope[..., lo:], pad], axis=-1)


@functools.lru_cache(maxsize=None)
def _rope_tables(l):
    rows = l // GRID_W
    row = np.repeat(np.arange(rows, dtype=np.float64), GRID_W)
    col = np.tile(np.arange(GRID_W, dtype=np.float64), rows)
    inv = ROPE_BASE ** (-np.arange(0, ROPE_AXIS, 2, dtype=np.float64) / ROPE_AXIS)
    ang = np.concatenate([row[:, None] * inv, col[:, None] * inv], axis=-1)
    cos = np.cos(ang)
    sin = np.sin(ang)
    c = _head_slot(np.ones((l, QK_NOPE)), cos, cos, xp=np)
    s = _head_slot(np.zeros((l, QK_NOPE)), -sin, sin, xp=np)
    scale = math.log2(math.e) / math.sqrt(QK_HEAD)
    return np.concatenate([c * scale, s * scale, c, s], axis=1).astype(np.float32)


def _mla_weights(wq_a, wq_b, wkv_a, wkv_b):
    d, q_lora = wq_a.shape
    kv_lora = wkv_a.shape[1] - QK_ROPE
    ev = np.arange(0, QK_ROPE, 2)
    od = np.arange(1, QK_ROPE, 2)
    kpe = wkv_a[:, kv_lora:]
    kpe_slot = _head_slot(jnp.zeros((d, QK_NOPE), F32), kpe[:, ev], kpe[:, od])
    wa = jnp.concatenate([wq_a, wkv_a[:, :kv_lora], kpe_slot], axis=1)
    qb = wq_b.reshape(q_lora, N_HEADS, QK_HEAD)
    qb = _head_slot(qb[..., :QK_NOPE], qb[..., QK_NOPE + ev], qb[..., QK_NOPE + od])
    kvb = wkv_b.reshape(kv_lora, N_HEADS, QK_NOPE + V_HEAD)
    no_rope = jnp.zeros((kv_lora, N_HEADS, ROPE_AXIS), F32)
    wk = _head_slot(kvb[..., :QK_NOPE], no_rope, no_rope)
    wv = kvb[..., QK_NOPE:]
    return (wa.astype(BF16), qb.reshape(q_lora, N_HEADS * HEAD_SLOT).astype(BF16),
            wk.reshape(kv_lora, N_HEADS * HEAD_SLOT).astype(BF16),
            wv.reshape(kv_lora, N_HEADS * V_HEAD).astype(BF16))


def _mod_spec(mod):
    per_b = mod.shape[0] > 1
    return pl.BlockSpec((1, MOD_ROWS, mod.shape[2]), (lambda b, i: (b, 0, 0)) if per_b else (lambda b, i: (0, 0, 0)))


def _mla_latent(x, mod, gnorm, wa, q_norm, kv_norm, wqb, wk, wv, tables, tm_pref=512):
    bx, n, d = x.shape
    tm = _pick(n, tm_pref)
    q_lora = q_norm.shape[0]
    kv_lora = kv_norm.shape[0]
    wq = wqb.shape[1]
    wvn = wv.shape[1]
    const = lambda b, i: (0, 0)
    row = lambda width: pl.BlockSpec((1, tm, width), lambda b, i: (b, i, 0))
    return pl.pallas_call(
        functools.partial(_mla_latent_kernel, q_lora=q_lora, kv_lora=kv_lora),
        out_shape=(jax.ShapeDtypeStruct((bx, n, wq), BF16),
                   jax.ShapeDtypeStruct((bx, n, wq), BF16),
                   jax.ShapeDtypeStruct((bx, n, wvn), BF16)),
        grid=(bx, n // tm),
        in_specs=[row(d), _mod_spec(mod),
                  pl.BlockSpec((1, d), const),
                  pl.BlockSpec(wa.shape, const),
                  pl.BlockSpec((1, q_lora), const),
                  pl.BlockSpec((1, kv_lora), const),
                  pl.BlockSpec(wqb.shape, const),
                  pl.BlockSpec(wk.shape, const),
                  pl.BlockSpec(wv.shape, const),
                  pl.BlockSpec((tm, tables.shape[1]), lambda b, i: (i, 0))],
        out_specs=(row(wq), row(wq), row(wvn)),
        compiler_params=_cparams(("arbitrary", "arbitrary")),
        name="mla_latent_proj",
    )(x, mod, gnorm.reshape(1, d), wa, q_norm.reshape(1, q_lora), kv_norm.reshape(1, kv_lora),
      wqb, wk, wv, tables)


def _mla_context(ctx, mod, gnorm, wa_kv, kv_norm, wk, wv, tm_pref=512):
    bx, n, d = ctx.shape
    tm = _pick(n, tm_pref)
    kv_lora = kv_norm.shape[0]
    wkn = wk.shape[1]
    wvn = wv.shape[1]
    const = lambda b, i: (0, 0)
    row = lambda width: pl.BlockSpec((1, tm, width), lambda b, i: (b, i, 0))
    return pl.pallas_call(
        functools.partial(_mla_context_kernel, kv_lora=kv_lora),
        out_shape=(jax.ShapeDtypeStruct((bx, n, wkn), BF16),
                   jax.ShapeDtypeStruct((bx, n, wvn), BF16)),
        grid=(bx, n // tm),
        in_specs=[row(d), _mod_spec(mod),
                  pl.BlockSpec((1, d), const),
                  pl.BlockSpec(wa_kv.shape, const),
                  pl.BlockSpec((1, kv_lora), const),
                  pl.BlockSpec(wk.shape, const),
                  pl.BlockSpec(wv.shape, const)],
        out_specs=(row(wkn), row(wvn)),
        compiler_params=_cparams(("arbitrary", "arbitrary")),
        name="mla_context_proj",
    )(ctx, mod, gnorm.reshape(1, d), wa_kv, kv_norm.reshape(1, kv_lora), wk, wv)


ATTN_HEADS_PER_STEP = 16
ATTN_KEY_CHUNK = 256
ATTN_SOFTMAX_LAG = 2
ATTN_PV_LAG = 6
SUM_ROWS = 16


def _attn_kernel(qt_ref, kc_ref, kx_ref, vct_ref, vxt_ref, o_ref, *, heads, ck):
    chunks = ([(kc_ref, vct_ref, c) for c in range(kc_ref.shape[1] // ck)]
              + [(kx_ref, vxt_ref, c) for c in range(kx_ref.shape[1] // ck)])
    items = [(h, ci) for ci in range(len(chunks)) for h in range(heads)]
    state = {h: None for h in range(heads)}
    scores = {}
    probs = {}
    ones = jnp.ones((SUM_ROWS, ck), BF16)

    def qk(t):
        h, ci = items[t]
        kref, _, c = chunks[ci]
        hsl = slice(h * HEAD_SLOT, (h + 1) * HEAD_SLOT)
        scores[t] = jnp.dot(kref[0, c * ck:(c + 1) * ck, hsl], qt_ref[0, hsl, :], preferred_element_type=F32)

    def softmax(t):
        h, _ = items[t]
        s = scores.pop(t)
        mc = jnp.max(s, axis=0, keepdims=True)
        if state[h] is None:
            probs[t] = (jnp.exp2(s - mc).astype(BF16), None)
            state[h] = (mc, None)
        else:
            m, acc = state[h]
            m_new = jnp.maximum(m, mc)
            probs[t] = (jnp.exp2(s - m_new).astype(BF16), jnp.exp2(m - m_new))
            state[h] = (m_new, acc)

    def pv(t):
        h, ci = items[t]
        _, vref, c = chunks[ci]
        p, alpha = probs.pop(t)
        lhs = jnp.concatenate([vref[0, h * V_HEAD:(h + 1) * V_HEAD, c * ck:(c + 1) * ck], ones], axis=0)
        o = jnp.dot(lhs, p, preferred_element_type=F32)
        m, acc = state[h]
        state[h] = (m, o if acc is None else alpha * acc + o)

    n_items = len(items)
    for t in range(n_items + ATTN_PV_LAG):
        if t < n_items:
            qk(t)
        if ATTN_SOFTMAX_LAG <= t < n_items + ATTN_SOFTMAX_LAG:
            softmax(t - ATTN_SOFTMAX_LAG)
        if t >= ATTN_PV_LAG:
            pv(t - ATTN_PV_LAG)

    for hp in range(heads // 2):
        outs = [state[h][1][:V_HEAD] / state[h][1][V_HEAD:V_HEAD + 1] for h in (2 * hp, 2 * hp + 1)]
        pair = jnp.concatenate(outs, axis=0)
        o_ref[0, :, hp * 2 * V_HEAD:(hp + 1) * 2 * V_HEAD] = pair.T.astype(BF16)


def _attention(qt, kc, kx, vct, vxt, tq_pref=256, heads=ATTN_HEADS_PER_STEP):
    bx, _, l = qt.shape
    n_ctx = kc.shape[1]
    tq = _pick(l, tq_pref)
    ck = _pick(n_ctx, ATTN_KEY_CHUNK)
    qk_w = heads * HEAD_SLOT
    v_w = heads * V_HEAD
    return pl.pallas_call(
        functools.partial(_attn_kernel, heads=heads, ck=ck),
        out_shape=jax.ShapeDtypeStruct((bx, l, N_HEADS * V_HEAD), BF16),
        grid=(bx, N_HEADS // heads, l // tq),
        in_specs=[pl.BlockSpec((1, qk_w, tq), lambda b, g, i: (b, g, i)),
                  pl.BlockSpec((1, n_ctx, qk_w), lambda b, g, i: (b, 0, g)),
                  pl.BlockSpec((1, l, qk_w), lambda b, g, i: (b, 0, g)),
                  pl.BlockSpec((1, v_w, n_ctx), lambda b, g, i: (b, g, 0)),
                  pl.BlockSpec((1, v_w, l), lambda b, g, i: (b, g, 0))],
        out_specs=pl.BlockSpec((1, tq, v_w), lambda b, g, i: (b, i, g)),
        compiler_params=_cparams(("arbitrary", "arbitrary", "arbitrary")),
        name="mla_attention",
    )(qt, kc, kx, vct, vxt)


ROUTE_SUBTILES = 2


def _route_kernel(x_ref, a_ref, pw_ref, mod_ref, g_ref, r_ref, tri_ref, xr_ref, h_ref, meta_ref, cnt_ref,
                  carry_ref, *, n_exp):
    i = pl.program_id(0)

    @pl.when(i == 0)
    def _():
        carry_ref[...] = jnp.zeros_like(carry_ref)

    mod = mod_ref[0]
    gate = mod[MOD_G1:MOD_G1 + 1]
    tr, d = x_ref.shape
    rs = tri_ref.shape[0]
    s = d // LANES
    proj = [jnp.dot(a_ref[t * rs:(t + 1) * rs, :], pw_ref[...], preferred_element_type=F32) for t in range(tr // rs)]
    r = r_ref[...]
    r_hi = r.astype(BF16)
    r_lo = (r - r_hi.astype(F32)).astype(BF16)
    lane = lax.broadcasted_iota(jnp.int32, (rs, LANES), 1)
    lane_f = lane.astype(F32)
    neg = jnp.float32(-jnp.inf)
    carry = carry_ref[...]
    for t in range(tr // rs):
        rows = slice(t * rs, (t + 1) * rs)
        xr = x_ref[rows, :] + gate * proj[t]
        xr_ref[rows, :] = xr
        h = _rms(xr, g_ref[...]) * (1.0 + mod[MOD_SC2:MOD_SC2 + 1]) + mod[MOD_SH2:MOD_SH2 + 1]
        _store_token_slabs(h_ref.at[pl.ds(t * rs * s, rs * s)], h)
        h_hi = h.astype(BF16)
        h_lo = (h - h_hi.astype(F32)).astype(BF16)
        logits = (jnp.dot(h_hi, r_hi, preferred_element_type=F32) + jnp.dot(h_hi, r_lo, preferred_element_type=F32)
                  + jnp.dot(h_lo, r_hi, preferred_element_type=F32))
        lg = jnp.where(lane < n_exp, logits, neg)
        v1 = jnp.max(lg, axis=-1, keepdims=True)
        i1 = jnp.min(jnp.where(lg == v1, lane_f, float(LANES)), axis=-1, keepdims=True)
        oh1 = lane_f == i1
        lg2 = jnp.where(oh1, neg, lg)
        v2 = jnp.max(lg2, axis=-1, keepdims=True)
        i2 = jnp.min(jnp.where(lg2 == v2, lane_f, float(LANES)), axis=-1, keepdims=True)
        oh2 = lane_f == i2
        e = jnp.exp(v2 - v1)
        g1 = 1.0 / (1.0 + e)
        g2 = e / (1.0 + e)
        oh = jnp.where(oh1 | oh2, 1.0, 0.0)
        pref = jnp.dot(tri_ref[...], oh.astype(BF16), preferred_element_type=F32)
        excl = pref - oh + carry
        r1 = jnp.sum(jnp.where(oh1, excl, 0.0), axis=-1, keepdims=True)
        r2 = jnp.sum(jnp.where(oh2, excl, 0.0), axis=-1, keepdims=True)
        carry = carry + jnp.sum(oh, axis=0, keepdims=True)
        meta = jnp.where(lane == 0, i1, 0.0)
        meta = jnp.where(lane == 1, i2, meta)
        meta = jnp.where(lane == 2, g1, meta)
        meta = jnp.where(lane == 3, g2, meta)
        meta = jnp.where(lane == 4, r1, meta)
        meta = jnp.where(lane == 5, r2, meta)
        meta_ref[rows, :] = meta
    carry_ref[...] = carry
    cnt_ref[...] = carry


def _mixer_proj_route(x_flat, a_flat, pw_bf16, mod, gnorm, router, tokens_per_batch, tr_pref=512):
    n, d = x_flat.shape
    k = a_flat.shape[1]
    n_exp = router.shape[1]
    tr = _pick(tokens_per_batch, tr_pref)
    per_b = tokens_per_batch // tr
    rpad = jnp.pad(router, ((0, 0), (0, LANES - n_exp)))
    rs = tr // ROUTE_SUBTILES
    tri = jnp.asarray(np.tril(np.ones((rs, rs), np.float32))).astype(BF16)
    return pl.pallas_call(
        functools.partial(_route_kernel, n_exp=n_exp),
        out_shape=(jax.ShapeDtypeStruct((n, d), F32),
                   jax.ShapeDtypeStruct((n * (d // LANES), LANES), F32),
                   jax.ShapeDtypeStruct((n, LANES), F32),
                   jax.ShapeDtypeStruct((1, LANES), F32)),
        grid=(n // tr,),
        in_specs=[pl.BlockSpec((tr, d), lambda i: (i, 0)),
                  pl.BlockSpec((tr, k), lambda i: (i, 0)),
                  pl.BlockSpec((k, d), lambda i: (0, 0)),
                  pl.BlockSpec((1, MOD_ROWS, d), lambda i: (i // per_b, 0, 0)),
                  pl.BlockSpec((1, d), lambda i: (0, 0)),
                  pl.BlockSpec((d, LANES), lambda i: (0, 0)),
                  pl.BlockSpec((rs, rs), lambda i: (0, 0))],
        out_specs=(pl.BlockSpec((tr, d), lambda i: (i, 0)),
                   pl.BlockSpec((tr * (d // LANES), LANES), lambda i: (i, 0)),
                   pl.BlockSpec((tr, LANES), lambda i: (i, 0)),
                   pl.BlockSpec((1, LANES), lambda i: (0, 0))),
        scratch_shapes=[pltpu.VMEM((1, LANES), F32)],
        compiler_params=_cparams(("arbitrary",)),
        name="moe_route",
    )(x_flat, a_flat, pw_bf16, mod, gnorm.reshape(1, d), rpad, tri)


ROW_DMA_UNROLL = 8


def _wait_rows(any_ref, rows, sem):
    blk = any_ref.at[pl.ds(0, rows)]
    pltpu.make_async_copy(blk, blk, sem).wait()


def _dispatch_kernel(pos_ref, h_ref, xs_in_ref, xs_ref, sem, *, td, s):
    del xs_in_ref

    def issue(t, c):
        src = h_ref.at[pl.ds(pl.multiple_of(t * s, s), s)]
        for k in range(TOP_K):
            dst = xs_ref.at[pl.ds(pl.multiple_of(pos_ref[0, 0, k * td + t], s), s)]
            pltpu.make_async_copy(src, dst, sem).start(priority=k)
        return c

    lax.fori_loop(0, td, issue, 0, unroll=ROW_DMA_UNROLL)
    _wait_rows(xs_ref, TOP_K * td * s, sem)


def _dispatch(h, pos_tiles, np_rows, td, s):
    n = h.shape[0] // s
    xs0 = jnp.zeros((np_rows * s, LANES), F32)
    return pl.pallas_call(
        functools.partial(_dispatch_kernel, td=td, s=s),
        out_shape=jax.ShapeDtypeStruct((np_rows * s, LANES), F32),
        grid=(n // td,),
        in_specs=[pl.BlockSpec((1, 1, TOP_K * td), lambda i: (i, 0, 0), memory_space=pltpu.SMEM),
                  pl.BlockSpec((td * s, LANES), lambda i: (i, 0)),
                  pl.BlockSpec(memory_space=pl.ANY)],
        out_specs=pl.BlockSpec(memory_space=pl.ANY),
        scratch_shapes=[pltpu.SemaphoreType.DMA(())],
        input_output_aliases={2: 0},
        compiler_params=_cparams(("arbitrary",)),
        name="moe_dispatch",
    )(pos_tiles, h, xs0)


def _combine_kernel(pos_ref, posn_ref, ys_ref, meta_ref, x_ref, mod_ref, g_ref, o_ref, buf_ref, sems, *, td, s):
    i = pl.program_id(0)
    n = pl.num_programs(0)
    slot = lax.rem(i, 2)

    def gather(p_ref, sl):
        def issue(t, c):
            for k in range(TOP_K):
                src = ys_ref.at[pl.ds(pl.multiple_of(p_ref[0, 0, k * td + t], s), s)]
                dst = buf_ref.at[sl, k, pl.ds(pl.multiple_of(t * s, s), s)]
                pltpu.make_async_copy(src, dst, sems.at[sl]).start(priority=k)
            return c

        lax.fori_loop(0, td, issue, 0, unroll=ROW_DMA_UNROLL)

    @pl.when(i == 0)
    def _():
        gather(pos_ref, slot)

    @pl.when(i + 1 < n)
    def _():
        gather(posn_ref, 1 - slot)

    _wait_rows(ys_ref, TOP_K * td * s, sems.at[slot])
    meta = meta_ref[...]
    g0 = meta[:, 2:3]
    g1 = meta[:, 3:4]
    gate = mod_ref[0][5:6]
    xs = []
    ssq = jnp.zeros((td, 1), F32)
    for j in range(s):
        lanes = slice(j * LANES, (j + 1) * LANES)
        y = (g0 * _load_token_slab(buf_ref.at[slot, 0], j, td, s)
             + g1 * _load_token_slab(buf_ref.at[slot, 1], j, td, s))
        xj = x_ref[:, lanes] + gate[:, lanes] * y
        ssq = ssq + jnp.sum(xj * xj, axis=-1, keepdims=True)
        xs.append(xj)
    inv = lax.rsqrt(ssq / (s * LANES) + RMS_EPS)
    for j in range(s):
        lanes = slice(j * LANES, (j + 1) * LANES)
        o_ref[:, lanes] = xs[j] * inv * g_ref[:, lanes]


def _combine(ys, pos_tiles, meta, x_flat, mod, norm_final, tokens_per_batch, td):
    n, d = x_flat.shape
    s = d // LANES
    per_b = tokens_per_batch // td
    nt = n // td
    return pl.pallas_call(
        functools.partial(_combine_kernel, td=td, s=s),
        out_shape=jax.ShapeDtypeStruct((n, d), F32),
        grid=(nt,),
        in_specs=[pl.BlockSpec((1, 1, TOP_K * td), lambda i: (i, 0, 0), memory_space=pltpu.SMEM),
                  pl.BlockSpec((1, 1, TOP_K * td), lambda i: (jnp.minimum(i + 1, nt - 1), 0, 0),
                               memory_space=pltpu.SMEM),
                  pl.BlockSpec(memory_space=pl.ANY),
                  pl.BlockSpec((td, LANES), lambda i: (i, 0)),
                  pl.BlockSpec((td, d), lambda i: (i, 0)),
                  pl.BlockSpec((1, MOD_ROWS, d), lambda i: (i // per_b, 0, 0)),
                  pl.BlockSpec((1, d), lambda i: (0, 0))],
        out_specs=pl.BlockSpec((td, d), lambda i: (i, 0)),
        scratch_shapes=[pltpu.VMEM((2, TOP_K, td * s, LANES), F32), pltpu.SemaphoreType.DMA((2,))],
        compiler_params=_cparams(("arbitrary",)),
        name="moe_combine",
    )(pos_tiles, pos_tiles, ys, meta, x_flat, mod, norm_final.reshape(1, d))


EXPERT_TILE_ROWS = 1024


def _attn_proj_moe(x, o, wo_bf16, mod, gnorm, router, w1, w3, w2, norm_final, td=256):
    bx, l, d = x.shape
    n = bx * l
    n_exp = router.shape[1]
    s = d // LANES
    tm = min(EXPERT_TILE_ROWS, n * TOP_K)
    x_flat, h, meta, counts = _mixer_proj_route(x.reshape(n, d), o.reshape(n, o.shape[2]), wo_bf16, mod, gnorm,
                                                router, l)
    idx = meta[:, 0:TOP_K].astype(jnp.int32)
    rank = meta[:, 4:4 + TOP_K].astype(jnp.int32)
    cnt = counts[0, :n_exp].astype(jnp.int32)
    tiles_e = (cnt + tm - 1) // tm
    tile_end = jnp.cumsum(tiles_e)
    start_rows = (tile_end - tiles_e) * tm
    sel = idx[..., None] == jnp.arange(n_exp, dtype=jnp.int32)
    pos = jnp.sum(jnp.where(sel, start_rows, 0), axis=-1) + rank
    n_tiles = -(-(n * TOP_K) // tm) + n_exp
    np_rows = n_tiles * tm
    tile_ids = jnp.arange(n_tiles, dtype=jnp.int32)
    used = tile_end[-1]
    tile_expert = jnp.sum((jnp.minimum(tile_ids, used - 1)[:, None] >= tile_end[None, :]).astype(jnp.int32), axis=1)
    tile_expert = jnp.minimum(tile_expert, n_exp - 1)
    onehot_e = tile_expert[:, None] == jnp.arange(n_exp, dtype=jnp.int32)
    tile_cnt = jnp.sum(jnp.where(onehot_e, cnt, 0), axis=1)
    tile_first = jnp.sum(jnp.where(onehot_e, tile_end - tiles_e, 0), axis=1)
    tile_rows = jnp.clip(tile_cnt - (tile_ids - tile_first) * tm, 0, tm)
    part = tm // EXPERT_TILE_PARTS
    tile_valid = jnp.where(tile_ids < used, (tile_rows + part - 1) // part, 0).astype(jnp.int32)
    pos_tiles = (pos * s).reshape(n // td, td, TOP_K).transpose(0, 2, 1).reshape(n // td, 1, TOP_K * td)
    xs = _dispatch(h, pos_tiles, np_rows, td, s)
    ys = _ffn_experts(xs, tile_expert, tile_valid, w1, w3, w2, tm)
    out = _combine(ys, pos_tiles, meta, x_flat, mod, norm_final, l, td)
    return out.reshape(bx, l, d)


def _mod_rows(m):
    r, n = m.shape
    return jnp.pad(m.reshape(r, N_MOD, n // N_MOD), ((0, 0), (0, MOD_ROWS - N_MOD), (0, 0)))


def _hyena_mixer(x, mod, gnorm, in_w, in_b, sc_w, sc_b, spectra, f_bias, ct_pref):
    zu = _hyena_in(x, mod, gnorm, in_w, in_b, sc_w, sc_b, _pick(x.shape[2], ct_pref))
    return _hyena_conv(zu, spectra, f_bias)


@jax.jit
def kernel(x, c, ctx, c_ctx, ada_w, ada_b, norm_mix, norm_ffn, hy_in_w, hy_in_b, hy_sc_w, hy_sc_b, hy_f_w0, hy_f_b0, hy_f_wi, hy_f_bi, hy_f_freq, hy_f_wout, hy_f_bias, hy_out_w, hy_out_b, mla_wq_a, mla_q_norm, mla_wq_b, mla_wkv_a, mla_kv_norm, mla_wkv_b, mla_wo, ffn_w1, ffn_w3, ffn_w2, moe_router, moe_w1, moe_w3, moe_w2, norm_final):
    bsz, l, d = x.shape
    n_ctx = ctx.shape[1]
    depth = ada_w.shape[0]
    assert depth == 2, "layer 0 = Hyena + dense SwiGLU, layer 1 = MLA + expert SwiGLU"

    rows = -(-(bsz + 1) // 8) * 8
    cvec = jnp.zeros((rows, d), F32).at[:bsz].set(c).at[bsz].set(c_ctx)
    mods = _ada_mod(cvec, ada_w, ada_b)
    modx = [_mod_rows(mods[i, :bsz]) for i in range(depth)]
    modc = [_mod_rows(mods[i, bsz:bsz + 1]) for i in range(depth)]

    in_w = hy_in_w[0].astype(BF16)
    out_w = hy_out_w[0].astype(BF16)
    fargs = (hy_f_w0[0], hy_f_b0[0], hy_f_wi[0], hy_f_bi[0], hy_f_freq[0], hy_f_wout[0])
    kx = _hyena_filter_spectra(l, d, *fargs)
    kc = _hyena_filter_spectra(n_ctx, d, *fargs)
    gx = _hyena_mixer(x, modx[0], norm_mix[0], in_w, hy_in_b[0], hy_sc_w[0], hy_sc_b[0], kx, hy_f_bias[0], 256)
    gc = _hyena_mixer(ctx, modc[0], norm_mix[0], in_w, hy_in_b[0], hy_sc_w[0], hy_sc_b[0], kc, hy_f_bias[0], 1024)
    ffn = (norm_ffn[0], ffn_w1[0], ffn_w3[0], ffn_w2[0])
    x = _mixer_proj_ffn_dense(x, gx, out_w, hy_out_b[0], modx[0], *ffn)
    ctx = _mixer_proj_ffn_dense(ctx.reshape(1, bsz * n_ctx, d), gc.reshape(1, bsz * n_ctx, d), out_w, hy_out_b[0],
                                modc[0], *ffn).reshape(bsz, n_ctx, d)

    wa, wqb, wk, wv = _mla_weights(mla_wq_a[0], mla_wq_b[0], mla_wkv_a[0], mla_wkv_b[0])
    q_lora = mla_q_norm.shape[1]
    q, kx, vx = _mla_latent(x, modx[1], norm_mix[1], wa, mla_q_norm[0], mla_kv_norm[0], wqb, wk, wv,
                            _rope_tables(l))
    kc, vc = _mla_context(ctx, modc[1], norm_mix[1], wa[:, q_lora:], mla_kv_norm[0], wk, wv)
    o = _attention(q.transpose(0, 2, 1), kc, kx, vc.transpose(0, 2, 1), vx.transpose(0, 2, 1))
    return _attn_proj_moe(x, o, mla_wo[0].astype(BF16), modx[1], norm_ffn[1], moe_router[0],
                          moe_w1[0], moe_w3[0], moe_w2[0], norm_final)
```
